```python
import jax, jax.numpy as jnp
from jax import lax
import numpy as np

D_MODEL = 2048
BATCH = 8
SEQ = 8192
DEPTH = 4

CHUNK = 64
HEAD_DIM = 128
D_MIX = D_MODEL
W_A = 6 * HEAD_DIM
W_B = 6 * HEAD_DIM
N_POOL_GROUPS = 4
POOL_WINDOWS = (2, 4, 8, 16)
G_C = HEAD_DIM
W_C = N_POOL_GROUPS * G_C
D_IN_TOT = 3 * W_A + 2 * W_B + W_C
K_SHORT = 3
K_CONFORMER = 31
K_FFN = 3
D_FF = 5632
RMS_EPS = 1e-6
LN_EPS = 1e-5

kernel_name = "hybrid_conv_pool_streaming_trunk"


def rms_norm(x, g):
    xf = x.astype(jnp.float32)
    y = xf * lax.rsqrt(jnp.mean(xf * xf, axis=-1, keepdims=True) + RMS_EPS)
    return (y * g.astype(jnp.float32)).astype(x.dtype)


def layer_norm(x, g, b):
    xf = x.astype(jnp.float32)
    mu = jnp.mean(xf, axis=-1, keepdims=True)
    var = jnp.mean(jnp.square(xf - mu), axis=-1, keepdims=True)
    y = (xf - mu) * lax.rsqrt(var + LN_EPS)
    return (y * g.astype(jnp.float32) + b.astype(jnp.float32)).astype(x.dtype)


def causal_depthwise_conv(x, w):
    k, c = w.shape
    return lax.conv_general_dilated(
        x, w[:, None, :].astype(x.dtype),
        window_strides=(1,), padding=[(k - 1, 0)],
        dimension_numbers=("NWC", "WIO", "NWC"),
        feature_group_count=c)


def trailing_mean_minus_self(x, window):
    s = x.shape[1]
    xf = x.astype(jnp.float32)
    c = jnp.cumsum(xf, axis=1)
    c_shift = jnp.pad(c, ((0, 0), (window, 0), (0, 0)))[:, :s]
    counts = jnp.minimum(jnp.arange(1, s + 1, dtype=jnp.float32), float(window))[None, :, None]
    return ((c - c_shift) / counts - xf).astype(x.dtype)


def short_gated_conv_mixer(u_a, conv_w):
    b_gate, c_gate, v = jnp.split(u_a, 3, axis=-1)
    return b_gate * causal_depthwise_conv(c_gate * v, conv_w)


def conformer_conv_mixer(u_b, conv_w, conv_b, ln_g, ln_b):
    val, gate = jnp.split(u_b, 2, axis=-1)
    glu = val * jax.nn.sigmoid(gate)
    y = causal_depthwise_conv(glu, conv_w) + conv_b
    y = layer_norm(y, ln_g, ln_b)
    return jax.nn.silu(y)


def multiscale_pool_mixer(u_c, pool_w, pool_scale):
    bsz, s, _ = u_c.shape
    groups = u_c.reshape(bsz, s, N_POOL_GROUPS, G_C)
    pooled = jnp.stack(
        [trailing_mean_minus_self(groups[:, :, gi], w) for gi, w in enumerate(POOL_WINDOWS)],
        axis=2)
    mixed = jnp.einsum("bsgc,gcd->bsgd", pooled, pool_w)
    return mixed.reshape(bsz, s, W_C) * pool_scale


def conv_gated_mlp(h, w_up, conv_w, conv_b, w_down):
    up = h @ w_up
    up = causal_depthwise_conv(up, conv_w) + conv_b
    gate, val = jnp.split(up, 2, axis=-1)
    return (jax.nn.silu(gate) * val) @ w_down


def _fwd_setup_inputs(seed: int = 0) -> dict:
    key = jax.random.key(seed)
    ks = jax.random.split(key, 20)
    f32 = jnp.float32
    nrm = lambda k, shape, scale: jax.random.normal(k, shape, f32) * scale
    gain = lambda k, shape: 1.0 + 0.05 * jax.random.normal(k, shape, f32)
    return {
        "x": jax.random.normal(ks[0], (BATCH, SEQ, D_MODEL), f32),
        "norm_mix_pre": gain(ks[1], (DEPTH, D_MODEL)),
        "norm_mix_post": gain(ks[2], (DEPTH, D_MODEL)),
        "norm_ffn_pre": gain(ks[3], (DEPTH, D_MODEL)),
        "norm_ffn_post": gain(ks[4], (DEPTH, D_MODEL)),
        "w_in": nrm(ks[5], (DEPTH, D_MODEL, D_IN_TOT), D_MODEL ** -0.5),
        "conv_a_w": nrm(ks[6], (DEPTH, K_SHORT, W_A), K_SHORT ** -0.5),
        "conv_b_w": nrm(ks[7], (DEPTH, K_CONFORMER, W_B), K_CONFORMER ** -0.5),
        "conv_b_bias": nrm(ks[8], (DEPTH, W_B), 0.01),
        "ln_b_gain": gain(ks[9], (DEPTH, W_B)),
        "ln_b_bias": nrm(ks[10], (DEPTH, W_B), 0.01),
        "pool_w": nrm(ks[11], (DEPTH, N_POOL_GROUPS, G_C, G_C), G_C ** -0.5),
        "pool_scale": 1.0 + 0.1 * jax.random.normal(ks[12], (DEPTH, W_C), f32),
        "w_out": nrm(ks[13], (DEPTH, D_MIX, D_MODEL), D_MIX ** -0.5),
        "w_up": nrm(ks[14], (DEPTH, D_MODEL, 2 * D_FF), D_MODEL ** -0.5),
        "conv_ffn_w": nrm(ks[15], (DEPTH, K_FFN, 2 * D_FF), K_FFN ** -0.5),
        "conv_ffn_bias": nrm(ks[16], (DEPTH, 2 * D_FF), 0.01),
        "w_down": nrm(ks[17], (DEPTH, D_FF, D_MODEL), D_FF ** -0.5),
    }


def _fwd_reference(x, norm_mix_pre, norm_mix_post, norm_ffn_pre, norm_ffn_post,
              w_in, conv_a_w, conv_b_w, conv_b_bias, ln_b_gain, ln_b_bias,
              pool_w, pool_scale, w_out, w_up, conv_ffn_w, conv_ffn_bias, w_down):
    for l in range(DEPTH):
        h = rms_norm(x, norm_mix_pre[l])
        u = h @ w_in[l]
        u_a = u[..., :3 * W_A]
        u_b = u[..., 3 * W_A:3 * W_A + 2 * W_B]
        u_c = u[..., 3 * W_A + 2 * W_B:]
        y_a = short_gated_conv_mixer(u_a, conv_a_w[l])
        y_b = conformer_conv_mixer(u_b, conv_b_w[l], conv_b_bias[l], ln_b_gain[l], ln_b_bias[l])
        y_c = multiscale_pool_mixer(u_c, pool_w[l], pool_scale[l])
        y = jnp.concatenate([y_a, y_b, y_c], axis=-1) @ w_out[l]
        x = x + rms_norm(y, norm_mix_post[l])
        h = rms_norm(x, norm_ffn_pre[l])
        f = conv_gated_mlp(h, w_up[l], conv_ffn_w[l], conv_ffn_bias[l], w_down[l])
        x = x + rms_norm(f, norm_ffn_post[l])
    return x


import jax as _jax
import jax.numpy as _jnp

TWIN_FORMAT = 'train_step'
FWD_PARAMS = ['x', 'norm_mix_pre', 'norm_mix_post', 'norm_ffn_pre', 'norm_ffn_post', 'w_in', 'conv_a_w', 'conv_b_w', 'conv_b_bias', 'ln_b_gain', 'ln_b_bias', 'pool_w', 'pool_scale', 'w_out', 'w_up', 'conv_ffn_w', 'conv_ffn_bias', 'w_down']
TWIN_WEIGHTS = ['norm_mix_pre', 'norm_mix_post', 'norm_ffn_pre', 'norm_ffn_post', 'w_in', 'conv_a_w', 'conv_b_w', 'conv_b_bias', 'ln_b_gain', 'ln_b_bias', 'pool_w', 'pool_scale', 'w_out', 'w_up', 'conv_ffn_w', 'conv_ffn_bias', 'w_down']
TWIN_DIFF_INPUT = 'x'
TWIN_INPUTS = ['x', 'norm_mix_pre', 'norm_mix_post', 'norm_ffn_pre', 'norm_ffn_post', 'w_in', 'conv_a_w', 'conv_b_w', 'conv_b_bias', 'ln_b_gain', 'ln_b_bias', 'pool_w', 'pool_scale', 'w_out', 'w_up', 'conv_ffn_w', 'conv_ffn_bias', 'w_down', 'loss_target', 'm_norm_mix_pre', 'm_norm_mix_post', 'm_norm_ffn_pre', 'm_norm_ffn_post', 'm_w_in', 'm_conv_a_w', 'm_conv_b_w', 'm_conv_b_bias', 'm_ln_b_gain', 'm_ln_b_bias', 'm_pool_w', 'm_pool_scale', 'm_w_out', 'm_w_up', 'm_conv_ffn_w', 'm_conv_ffn_bias', 'm_w_down', 'v_norm_mix_pre', 'v_norm_mix_post', 'v_norm_ffn_pre', 'v_norm_ffn_post', 'v_w_in', 'v_conv_a_w', 'v_conv_b_w', 'v_conv_b_bias', 'v_ln_b_gain', 'v_ln_b_bias', 'v_pool_w', 'v_pool_scale', 'v_w_out', 'v_w_up', 'v_conv_ffn_w', 'v_conv_ffn_bias', 'v_w_down']
TWIN_OUTPUTS = ['loss', 'grad_x', 'grad_norm_mix_pre', 'grad_norm_mix_post', 'grad_norm_ffn_pre', 'grad_norm_ffn_post', 'grad_w_in', 'grad_conv_a_w', 'grad_conv_b_w', 'grad_conv_b_bias', 'grad_ln_b_gain', 'grad_ln_b_bias', 'grad_pool_w', 'grad_pool_scale', 'grad_w_out', 'grad_w_up', 'grad_conv_ffn_w', 'grad_conv_ffn_bias', 'grad_w_down', 'delta_norm_mix_pre', 'delta_norm_mix_post', 'delta_norm_ffn_pre', 'delta_norm_ffn_post', 'delta_w_in', 'delta_conv_a_w', 'delta_conv_b_w', 'delta_conv_b_bias', 'delta_ln_b_gain', 'delta_ln_b_bias', 'delta_pool_w', 'delta_pool_scale', 'delta_w_out', 'delta_w_up', 'delta_conv_ffn_w', 'delta_conv_ffn_bias', 'delta_w_down', 'new_m_norm_mix_pre', 'new_m_norm_mix_post', 'new_m_norm_ffn_pre', 'new_m_norm_ffn_post', 'new_m_w_in', 'new_m_conv_a_w', 'new_m_conv_b_w', 'new_m_conv_b_bias', 'new_m_ln_b_gain', 'new_m_ln_b_bias', 'new_m_pool_w', 'new_m_pool_scale', 'new_m_w_out', 'new_m_w_up', 'new_m_conv_ffn_w', 'new_m_conv_ffn_bias', 'new_m_w_down', 'new_v_norm_mix_pre', 'new_v_norm_mix_post', 'new_v_norm_ffn_pre', 'new_v_norm_ffn_post', 'new_v_w_in', 'new_v_conv_a_w', 'new_v_conv_b_w', 'new_v_conv_b_bias', 'new_v_ln_b_gain', 'new_v_ln_b_bias', 'new_v_pool_w', 'new_v_pool_scale', 'new_v_w_out', 'new_v_w_up', 'new_v_conv_ffn_w', 'new_v_conv_ffn_bias', 'new_v_w_down']
TWIN_LEAF_KINDS = {'loss': 'loss', 'grad_x': 'grad_x', 'grad_norm_mix_pre': 'grad_w', 'grad_norm_mix_post': 'grad_w', 'grad_norm_ffn_pre': 'grad_w', 'grad_norm_ffn_post': 'grad_w', 'grad_w_in': 'grad_w', 'grad_conv_a_w': 'grad_w', 'grad_conv_b_w': 'grad_w', 'grad_conv_b_bias': 'grad_w', 'grad_ln_b_gain': 'grad_w', 'grad_ln_b_bias': 'grad_w', 'grad_pool_w': 'grad_w', 'grad_pool_scale': 'grad_w', 'grad_w_out': 'grad_w', 'grad_w_up': 'grad_w', 'grad_conv_ffn_w': 'grad_w', 'grad_conv_ffn_bias': 'grad_w', 'grad_w_down': 'grad_w', 'delta_norm_mix_pre': 'delta_w', 'delta_norm_mix_post': 'delta_w', 'delta_norm_ffn_pre': 'delta_w', 'delta_norm_ffn_post': 'delta_w', 'delta_w_in': 'delta_w', 'delta_conv_a_w': 'delta_w', 'delta_conv_b_w': 'delta_w', 'delta_conv_b_bias': 'delta_w', 'delta_ln_b_gain': 'delta_w', 'delta_ln_b_bias': 'delta_w', 'delta_pool_w': 'delta_w', 'delta_pool_scale': 'delta_w', 'delta_w_out': 'delta_w', 'delta_w_up': 'delta_w', 'delta_conv_ffn_w': 'delta_w', 'delta_conv_ffn_bias': 'delta_w', 'delta_w_down': 'delta_w', 'new_m_norm_mix_pre': 'new_m', 'new_m_norm_mix_post': 'new_m', 'new_m_norm_ffn_pre': 'new_m', 'new_m_norm_ffn_post': 'new_m', 'new_m_w_in': 'new_m', 'new_m_conv_a_w': 'new_m', 'new_m_conv_b_w': 'new_m', 'new_m_conv_b_bias': 'new_m', 'new_m_ln_b_gain': 'new_m', 'new_m_ln_b_bias': 'new_m', 'new_m_pool_w': 'new_m', 'new_m_pool_scale': 'new_m', 'new_m_w_out': 'new_m', 'new_m_w_up': 'new_m', 'new_m_conv_ffn_w': 'new_m', 'new_m_conv_ffn_bias': 'new_m', 'new_m_w_down': 'new_m', 'new_v_norm_mix_pre': 'new_v', 'new_v_norm_mix_post': 'new_v', 'new_v_norm_ffn_pre': 'new_v', 'new_v_norm_ffn_post': 'new_v', 'new_v_w_in': 'new_v', 'new_v_conv_a_w': 'new_v', 'new_v_conv_b_w': 'new_v', 'new_v_conv_b_bias': 'new_v', 'new_v_ln_b_gain': 'new_v', 'new_v_ln_b_bias': 'new_v', 'new_v_pool_w': 'new_v', 'new_v_pool_scale': 'new_v', 'new_v_w_out': 'new_v', 'new_v_w_up': 'new_v', 'new_v_conv_ffn_w': 'new_v', 'new_v_conv_ffn_bias': 'new_v', 'new_v_w_down': 'new_v'}


def _forward(args):
    return _fwd_reference(*[args[k] for k in FWD_PARAMS])


def _output_shape():
    def fwd():
        inp = _fwd_setup_inputs(0)
        return _fwd_reference(*[inp[k] for k in FWD_PARAMS])
    out = _jax.eval_shape(fwd)
    return out.shape, out.dtype

N_MICROBATCH = 1
ADAM_LR = 0.001
ADAM_B1 = 0.9
ADAM_B2 = 0.999
ADAM_EPS = 1e-08
ADAM_WD = 0.01
ADAM_STEP = 10
PER_EXAMPLE_BATCH_AXIS = {'x': 0, 'loss_target': 0}
SHARED_INPUTS = []
_WEIGHT_DTYPES = {'norm_mix_pre': _jnp.float32, 'norm_mix_post': _jnp.float32, 'norm_ffn_pre': _jnp.float32, 'norm_ffn_post': _jnp.float32, 'w_in': _jnp.float32, 'conv_a_w': _jnp.float32, 'conv_b_w': _jnp.float32, 'conv_b_bias': _jnp.float32, 'ln_b_gain': _jnp.float32, 'ln_b_bias': _jnp.float32, 'pool_w': _jnp.float32, 'pool_scale': _jnp.float32, 'w_out': _jnp.float32, 'w_up': _jnp.float32, 'conv_ffn_w': _jnp.float32, 'conv_ffn_bias': _jnp.float32, 'w_down': _jnp.float32}
MOMENT_SCALE = {'norm_mix_pre': 1.435867e+00, 'norm_mix_post': 3.175645e+01, 'norm_ffn_pre': 1.086563e+00, 'norm_ffn_post': 3.187218e+01, 'w_in': 9.909425e-01, 'conv_a_w': 1.116246e+00, 'conv_b_w': 8.416775e-01, 'conv_b_bias': 7.882103e+00, 'ln_b_gain': 3.315314e+00, 'ln_b_bias': 5.252801e+00, 'pool_w': 1.388695e+00, 'pool_scale': 1.438627e+00, 'w_out': 1.400709e+00, 'w_up': 4.608224e-01, 'conv_ffn_w': 4.786111e-01, 'conv_ffn_bias': 1.428494e+00, 'w_down': 8.130848e-01}


def _to_microbatches(a, axis):
    t = _jnp.moveaxis(a, axis, 0)
    t = t.reshape((N_MICROBATCH, t.shape[0] // N_MICROBATCH) + t.shape[1:])
    return _jnp.moveaxis(t, 1, axis + 1)


def setup_inputs(seed: int = 0) -> dict:
    inp = _fwd_setup_inputs(seed)
    key = _jax.random.fold_in(_jax.random.key(seed), 7919)
    shape, _ = _output_shape()
    out = dict(inp)
    out["loss_target"] = _jax.random.normal(_jax.random.fold_in(key, 0), shape, _jnp.float32)
    for i, name in enumerate(TWIN_WEIGHTS):
        w = inp[name].astype(_jnp.float32)
        if MOMENT_SCALE is None:
            s = _jnp.sqrt(_jnp.mean(_jnp.square(w)) + 1e-30)
        else:
            s = MOMENT_SCALE[name]
        km, kv = _jax.random.split(_jax.random.fold_in(key, i + 1))
        out[name] = w
        out["m_" + name] = s * _jax.random.normal(km, w.shape, _jnp.float32)
        out["v_" + name] = (s * s) * _jax.random.uniform(kv, w.shape, _jnp.float32, 0.5, 1.5)
    if N_MICROBATCH > 1:
        for name, axis in PER_EXAMPLE_BATCH_AXIS.items():
            out[name] = _to_microbatches(out[name], axis)
    return {'x': out['x'], 'norm_mix_pre': out['norm_mix_pre'], 'norm_mix_post': out['norm_mix_post'], 'norm_ffn_pre': out['norm_ffn_pre'], 'norm_ffn_post': out['norm_ffn_post'], 'w_in': out['w_in'], 'conv_a_w': out['conv_a_w'], 'conv_b_w': out['conv_b_w'], 'conv_b_bias': out['conv_b_bias'], 'ln_b_gain': out['ln_b_gain'], 'ln_b_bias': out['ln_b_bias'], 'pool_w': out['pool_w'], 'pool_scale': out['pool_scale'], 'w_out': out['w_out'], 'w_up': out['w_up'], 'conv_ffn_w': out['conv_ffn_w'], 'conv_ffn_bias': out['conv_ffn_bias'], 'w_down': out['w_down'], 'loss_target': out['loss_target'], 'm_norm_mix_pre': out['m_norm_mix_pre'], 'm_norm_mix_post': out['m_norm_mix_post'], 'm_norm_ffn_pre': out['m_norm_ffn_pre'], 'm_norm_ffn_post': out['m_norm_ffn_post'], 'm_w_in': out['m_w_in'], 'm_conv_a_w': out['m_conv_a_w'], 'm_conv_b_w': out['m_conv_b_w'], 'm_conv_b_bias': out['m_conv_b_bias'], 'm_ln_b_gain': out['m_ln_b_gain'], 'm_ln_b_bias': out['m_ln_b_bias'], 'm_pool_w': out['m_pool_w'], 'm_pool_scale': out['m_pool_scale'], 'm_w_out': out['m_w_out'], 'm_w_up': out['m_w_up'], 'm_conv_ffn_w': out['m_conv_ffn_w'], 'm_conv_ffn_bias': out['m_conv_ffn_bias'], 'm_w_down': out['m_w_down'], 'v_norm_mix_pre': out['v_norm_mix_pre'], 'v_norm_mix_post': out['v_norm_mix_post'], 'v_norm_ffn_pre': out['v_norm_ffn_pre'], 'v_norm_ffn_post': out['v_norm_ffn_post'], 'v_w_in': out['v_w_in'], 'v_conv_a_w': out['v_conv_a_w'], 'v_conv_b_w': out['v_conv_b_w'], 'v_conv_b_bias': out['v_conv_b_bias'], 'v_ln_b_gain': out['v_ln_b_gain'], 'v_ln_b_bias': out['v_ln_b_bias'], 'v_pool_w': out['v_pool_w'], 'v_pool_scale': out['v_pool_scale'], 'v_w_out': out['v_w_out'], 'v_w_up': out['v_w_up'], 'v_conv_ffn_w': out['v_conv_ffn_w'], 'v_conv_ffn_bias': out['v_conv_ffn_bias'], 'v_w_down': out['v_w_down']}


def _loss(weights, diff, rest, loss_target):
    with _jax.named_scope("forward"):
        args = {**rest, TWIN_DIFF_INPUT: diff, **{k: w.astype(_WEIGHT_DTYPES[k]) for k, w in weights.items()}}
        y = _forward(args)
    with _jax.named_scope("loss_head"):
        err = _jnp.square(y.astype(_jnp.float32) - loss_target)
        return 0.5 * _jnp.sum(_jnp.mean(err, axis=-1)) if err.ndim else 0.5 * err


def _adamw(w, g, m, v):
    m = ADAM_B1 * m + (1.0 - ADAM_B1) * g
    v = ADAM_B2 * v + (1.0 - ADAM_B2) * _jnp.square(g)
    m_hat = m / (1.0 - ADAM_B1 ** ADAM_STEP)
    v_hat = v / (1.0 - ADAM_B2 ** ADAM_STEP)
    delta = -ADAM_LR * (m_hat / (_jnp.sqrt(v_hat) + ADAM_EPS) + ADAM_WD * w)
    return delta, m, v


def reference(x, norm_mix_pre, norm_mix_post, norm_ffn_pre, norm_ffn_post, w_in, conv_a_w, conv_b_w, conv_b_bias, ln_b_gain, ln_b_bias, pool_w, pool_scale, w_out, w_up, conv_ffn_w, conv_ffn_bias, w_down, loss_target, m_norm_mix_pre, m_norm_mix_post, m_norm_ffn_pre, m_norm_ffn_post, m_w_in, m_conv_a_w, m_conv_b_w, m_conv_b_bias, m_ln_b_gain, m_ln_b_bias, m_pool_w, m_pool_scale, m_w_out, m_w_up, m_conv_ffn_w, m_conv_ffn_bias, m_w_down, v_norm_mix_pre, v_norm_mix_post, v_norm_ffn_pre, v_norm_ffn_post, v_w_in, v_conv_a_w, v_conv_b_w, v_conv_b_bias, v_ln_b_gain, v_ln_b_bias, v_pool_w, v_pool_scale, v_w_out, v_w_up, v_conv_ffn_w, v_conv_ffn_bias, v_w_down):
    given = dict(x=x, norm_mix_pre=norm_mix_pre, norm_mix_post=norm_mix_post, norm_ffn_pre=norm_ffn_pre, norm_ffn_post=norm_ffn_post, w_in=w_in, conv_a_w=conv_a_w, conv_b_w=conv_b_w, conv_b_bias=conv_b_bias, ln_b_gain=ln_b_gain, ln_b_bias=ln_b_bias, pool_w=pool_w, pool_scale=pool_scale, w_out=w_out, w_up=w_up, conv_ffn_w=conv_ffn_w, conv_ffn_bias=conv_ffn_bias, w_down=w_down, loss_target=loss_target, m_norm_mix_pre=m_norm_mix_pre, m_norm_mix_post=m_norm_mix_post, m_norm_ffn_pre=m_norm_ffn_pre, m_norm_ffn_post=m_norm_ffn_post, m_w_in=m_w_in, m_conv_a_w=m_conv_a_w, m_conv_b_w=m_conv_b_w, m_conv_b_bias=m_conv_b_bias, m_ln_b_gain=m_ln_b_gain, m_ln_b_bias=m_ln_b_bias, m_pool_w=m_pool_w, m_pool_scale=m_pool_scale, m_w_out=m_w_out, m_w_up=m_w_up, m_conv_ffn_w=m_conv_ffn_w, m_conv_ffn_bias=m_conv_ffn_bias, m_w_down=m_w_down, v_norm_mix_pre=v_norm_mix_pre, v_norm_mix_post=v_norm_mix_post, v_norm_ffn_pre=v_norm_ffn_pre, v_norm_ffn_post=v_norm_ffn_post, v_w_in=v_w_in, v_conv_a_w=v_conv_a_w, v_conv_b_w=v_conv_b_w, v_conv_b_bias=v_conv_b_bias, v_ln_b_gain=v_ln_b_gain, v_ln_b_bias=v_ln_b_bias, v_pool_w=v_pool_w, v_pool_scale=v_pool_scale, v_w_out=v_w_out, v_w_up=v_w_up, v_conv_ffn_w=v_conv_ffn_w, v_conv_ffn_bias=v_conv_ffn_bias, v_w_down=v_w_down)
    weights = {n: given[n] for n in TWIN_WEIGHTS}
    shared = {n: given[n] for n in SHARED_INPUTS}
    per_example = {n: given[n] for n in ['x']}
    grad_fn = _jax.value_and_grad(_loss, argnums=(0, 1))

    def one_microbatch(ex, loss_target):
        ex = dict(ex)
        diff = ex.pop(TWIN_DIFF_INPUT)
        return grad_fn(weights, diff, {**shared, **ex}, loss_target)

    if N_MICROBATCH == 1:
        loss, (grad_w, grad_x) = one_microbatch(per_example, given["loss_target"])
    else:
        def body(carry, xs):
            loss_sum, grad_sum = carry
            l_k, (gw_k, gx_k) = one_microbatch(xs[0], xs[1])
            with _jax.named_scope("update"):
                return (loss_sum + l_k, _jax.tree.map(_jnp.add, grad_sum, gw_k)), gx_k

        init = (_jnp.zeros((), _jnp.float32), _jax.tree.map(_jnp.zeros_like, weights))
        (loss, grad_w), grad_x = _jax.lax.scan(body, init, (per_example, given["loss_target"]))
    with _jax.named_scope("update"):
        delta_w, new_m, new_v = {}, {}, {}
        for n in TWIN_WEIGHTS:
            delta_w[n], new_m[n], new_v[n] = _adamw(weights[n], grad_w[n], given["m_" + n], given["v_" + n])
    return (loss, grad_x, *[grad_w[n] for n in TWIN_WEIGHTS], *[delta_w[n] for n in TWIN_WEIGHTS],
            *[new_m[n] for n in TWIN_WEIGHTS], *[new_v[n] for n in TWIN_WEIGHTS])
```

```python
import functools

import jax
import jax.numpy as jnp
from jax import lax
from jax.experimental import pallas as pl
from jax.experimental.pallas import tpu as pltpu

F32 = jnp.float32
BF16 = jnp.bfloat16
MESH = pl.DeviceIdType.MESH

RMS_EPS = 1e-6
LN_EPS = 1e-5
ADAM_LR = 0.001
ADAM_B1 = 0.9
ADAM_B2 = 0.999
ADAM_EPS = 1e-08
ADAM_WD = 0.01
ADAM_STEP = 10
POOL_WINDOWS = (2, 4, 8, 16)
K_A = 3
K_B = 31
K_F = 3

LANES = 128
HALO_MIX = 32
HALO_FFN = 16
ROWS = 64
TM_MIXER = 512
TM_FFN = 256
TM_NORM = 256
ADAM_BLOCK_BYTES = 1 << 20
VMEM_LIMIT = 56 * 1024 * 1024


def _cp(n_axes):
    return pltpu.CompilerParams(dimension_semantics=("arbitrary",) * n_axes, vmem_limit_bytes=VMEM_LIMIT)


def _tile(dim, target, mult=LANES):
    if dim <= target:
        return dim
    t = (target // mult) * mult
    while t >= mult:
        if dim % t == 0:
            return t
        t -= mult
    return dim


def _chunks(n, rc=ROWS):
    out, r0 = [], 0
    while r0 < n:
        s = min(rc, n - r0)
        out.append((r0, s))
        r0 += s
    return out


def _sigmoid(x):
    return 1.0 / (1.0 + jnp.exp(-x))


def matmul(a, b, *, ta=False, tb=False, out_dtype, tm, tn, tk, j_outer=False, groups=1, name):
    if ta:
        K, M = a.shape
    else:
        M, K = a.shape
    if tb:
        N, K2 = b.shape
    else:
        K2, N = b.shape
    assert K == K2, (a.shape, b.shape)
    ng = N // groups
    tm, tn, tk = _tile(M, tm), _tile(ng, tn), _tile(K, tk)
    nm, nn, nk = M // tm, N // tn, K // tk
    per = ng // tn

    def ij(g0, g1):
        return (g1, g0) if j_outer else (g0, g1)

    def a_map(g0, g1, k):
        i, j = ij(g0, g1)
        return (k, i) if ta else (i, k)

    def b_map(g0, g1, k):
        i, j = ij(g0, g1)
        return (j, k) if tb else (k, j)

    def o_map(g0, g1, k):
        i, j = ij(g0, g1)
        return (j // per, i, j % per) if groups > 1 else (i, j)

    dims = (((0 if ta else 1,), (1 if tb else 0,)), ((), ()))
    use_acc = nk > 1 and out_dtype != F32

    def body(a_ref, b_ref, o_ref, *scratch):
        p = lax.dot_general(a_ref[...], b_ref[...], dims, preferred_element_type=F32)
        if nk == 1:
            o_ref[...] = p.astype(o_ref.dtype)
            return
        acc = scratch[0] if use_acc else o_ref
        k = pl.program_id(2)

        @pl.when(k == 0)
        def _():
            acc[...] = p

        @pl.when(k > 0)
        def _():
            acc[...] += p

        if use_acc:
            @pl.when(k == nk - 1)
            def _():
                o_ref[...] = acc[...].astype(o_ref.dtype)

    grid = (nn, nm, nk) if j_outer else (nm, nn, nk)
    if groups > 1:
        out_shape = jax.ShapeDtypeStruct((groups, M, ng), out_dtype)
        out_spec = pl.BlockSpec((None, tm, tn), o_map)
    else:
        out_shape = jax.ShapeDtypeStruct((M, N), out_dtype)
        out_spec = pl.BlockSpec((tm, tn), o_map)
    return pl.pallas_call(
        body, grid=grid,
        in_specs=[pl.BlockSpec((tk, tm) if ta else (tm, tk), a_map),
                  pl.BlockSpec((tn, tk) if tb else (tk, tn), b_map)],
        out_specs=out_spec, out_shape=out_shape,
        scratch_shapes=[pltpu.VMEM((tm, tn), F32)] if use_acc else [],
        compiler_params=_cp(3), name=name,
    )(a, b)


def _rms(v):
    return lax.rsqrt(jnp.mean(v * v, axis=-1, keepdims=True) + RMS_EPS)


def norm_fwd(x, g, *, name):
    S, D = x.shape
    tm = _tile(S, TM_NORM, 16)

    def body(x_ref, g_ref, h_ref):
        v = x_ref[...]
        h_ref[...] = (v * _rms(v) * g_ref[...]).astype(BF16)

    return pl.pallas_call(
        body, grid=(S // tm,),
        in_specs=[pl.BlockSpec((tm, D), lambda i: (i, 0)), pl.BlockSpec((1, D), lambda i: (0, 0))],
        out_specs=pl.BlockSpec((tm, D), lambda i: (i, 0)),
        out_shape=jax.ShapeDtypeStruct((S, D), BF16), compiler_params=_cp(1), name=name,
    )(x, g)


def resid_norm_fwd(x, y, gp, gn, *, emit_h, name):
    S, D = x.shape
    tm = _tile(S, TM_NORM, 16)

    def body(x_ref, y_ref, gp_ref, gn_ref, xn_ref, *rest):
        yv = y_ref[...]
        xn = x_ref[...] + yv * _rms(yv) * gp_ref[...]
        xn_ref[...] = xn
        if emit_h:
            rest[0][...] = (xn * _rms(xn) * gn_ref[...]).astype(BF16)

    row = pl.BlockSpec((tm, D), lambda i: (i, 0))
    vec = pl.BlockSpec((1, D), lambda i: (0, 0))
    outs = pl.pallas_call(
        body, grid=(S // tm,), in_specs=[row, row, vec, vec],
        out_specs=[row, row] if emit_h else [row],
        out_shape=[jax.ShapeDtypeStruct((S, D), F32)] + ([jax.ShapeDtypeStruct((S, D), BF16)] if emit_h else []),
        compiler_params=_cp(1), name=name,
    )(x, y, gp, gn)
    return (outs[0], outs[1]) if emit_h else (outs[0], None)


def _rms_bwd(v, g, dout):
    r = _rms(v)
    gd = g * dout
    dv = r * gd - v * (r * r * r) * jnp.mean(v * gd, axis=-1, keepdims=True)
    return dv, dout * v * r


def norm_bwd(d_direct, dh, xn, gn, y, gp, *, name):
    S, D = d_direct.shape
    tm = _tile(S, TM_NORM, 16)
    has_h, has_y = dh is not None, y is not None

    def body(*refs):
        refs = list(refs)
        dd_ref = refs.pop(0)
        if has_h:
            dh_ref, xn_ref, gn_ref = refs.pop(0), refs.pop(0), refs.pop(0)
        if has_y:
            y_ref, gp_ref = refs.pop(0), refs.pop(0)
        if has_h:
            dt_ref = refs.pop(0)
        if has_y:
            dy_ref = refs.pop(0)
        if has_h:
            dgn_ref = refs.pop(0)
        if has_y:
            dgp_ref = refs.pop(0)
        i = pl.program_id(0)
        dt = dd_ref[...]
        if has_h:
            dv, gterm = _rms_bwd(xn_ref[...], gn_ref[...], dh_ref[...])
            dt = dt + dv
            dt_ref[...] = dt
            part = jnp.sum(gterm, axis=0, keepdims=True)

            @pl.when(i == 0)
            def _():
                dgn_ref[...] = part

            @pl.when(i > 0)
            def _():
                dgn_ref[...] += part
        if has_y:
            dy, gterm = _rms_bwd(y_ref[...], gp_ref[...], dt)
            dy_ref[...] = dy.astype(BF16)
            part2 = jnp.sum(gterm, axis=0, keepdims=True)

            @pl.when(i == 0)
            def _():
                dgp_ref[...] = part2

            @pl.when(i > 0)
            def _():
                dgp_ref[...] += part2

    row = pl.BlockSpec((tm, D), lambda i: (i, 0))
    vec = pl.BlockSpec((1, D), lambda i: (0, 0))
    ins, in_specs = [d_direct], [row]
    if has_h:
        ins += [dh, xn, gn]
        in_specs += [row, row, vec]
    if has_y:
        ins += [y, gp]
        in_specs += [row, vec]
    out_specs, out_shape = [], []
    if has_h:
        out_specs.append(row)
        out_shape.append(jax.ShapeDtypeStruct((S, D), F32))
    if has_y:
        out_specs.append(row)
        out_shape.append(jax.ShapeDtypeStruct((S, D), BF16))
    if has_h:
        out_specs.append(vec)
        out_shape.append(jax.ShapeDtypeStruct((1, D), F32))
    if has_y:
        out_specs.append(vec)
        out_shape.append(jax.ShapeDtypeStruct((1, D), F32))
    outs = list(pl.pallas_call(
        body, grid=(S // tm,), in_specs=in_specs, out_specs=out_specs, out_shape=out_shape,
        compiler_params=_cp(1), name=name,
    )(*ins))
    dt = outs.pop(0) if has_h else d_direct
    dy = outs.pop(0) if has_y else None
    dgn = outs.pop(0) if has_h else None
    dgp = outs.pop(0) if has_y else None
    return dt, dy, dgn, dgp


def loss_head(xl, target, *, name):
    S, D = xl.shape
    tm = _tile(S, TM_NORM, 16)

    def body(x_ref, t_ref, dx_ref, l_ref):
        i = pl.program_id(0)
        e = x_ref[...] - t_ref[...]
        dx_ref[...] = e * (1.0 / D)
        part = jnp.sum(e * e)

        @pl.when(i == 0)
        def _():
            l_ref[...] = jnp.zeros_like(l_ref) + part

        @pl.when(i > 0)
        def _():
            l_ref[...] += part

    row = pl.BlockSpec((tm, D), lambda i: (i, 0))
    return pl.pallas_call(
        body, grid=(S // tm,), in_specs=[row, row],
        out_specs=[row, pl.BlockSpec((8, LANES), lambda i: (0, 0))],
        out_shape=[jax.ShapeDtypeStruct((S, D), F32), jax.ShapeDtypeStruct((8, LANES), F32)],
        compiler_params=_cp(1), name=name,
    )(xl, target)


def _tap_sum(src, base, offs, wrows, r0, rc, c0):
    acc = None
    for k, off in enumerate(offs):
        t = src[base + r0 + off: base + r0 + off + rc, c0:c0 + LANES]
        if wrows is not None:
            t = t * wrows[k]
        acc = t if acc is None else acc + t
    return acc


def _tap_wgrad(out_ref, o0, a, a_base, b, b_base, offs, n, c0, first):
    for k, off in enumerate(offs):
        acc = None
        for r0, rc in _chunks(n):
            t = (a[a_base + r0: a_base + r0 + rc, c0:c0 + LANES]
                 * b[b_base + r0 + off: b_base + r0 + off + rc, c0:c0 + LANES])
            t = jnp.sum(t, axis=0, keepdims=True)
            acc = t if acc is None else acc + t
        _acc_store(out_ref, (slice(k, k + 1), slice(o0 + c0, o0 + c0 + LANES)), acc, first)


def _acc_store(ref, idx, val, first):
    @pl.when(first)
    def _():
        ref[idx] = val

    @pl.when(jnp.logical_not(first))
    def _():
        ref[idx] += val


def _row_counts(t0, rc, w):
    t = t0 + lax.broadcasted_iota(jnp.int32, (rc, LANES), 0)
    return jnp.minimum(t + 1, w).astype(F32)


def mixer_fwd(u, wa, wb, bb, lg, lb, pw, ps, *, name):
    S, DIN = u.shape
    WA, WB, WC = wa.shape[1], wb.shape[1], ps.shape[1]
    DMIX = WA + WB + WC
    tm = _tile(S, TM_MIXER, HALO_MIX)
    HB = HALO_MIX
    r = tm // HB
    oCg, oVa, oVal, oGate, oC = WA, 2 * WA, 3 * WA, 3 * WA + WB, 3 * WA + 2 * WB
    offs_a = [k - (K_A - 1) for k in range(K_A)]
    offs_b = [k - (K_B - 1) for k in range(K_B)]

    def body(u_ref, uh_ref, wa_ref, wb_ref, bb_ref, lg_ref, lb_ref, pw_ref, ps_ref, y_ref,
             pbuf, gbuf, cbuf, xbuf, plbuf):
        i = pl.program_id(0)
        hv = jnp.where(i > 0, 1.0, 0.0).astype(F32)

        def fill(src, dst0, n, scale):
            for r0, rc in _chunks(n):
                rows, drows = slice(r0, r0 + rc), slice(dst0 + r0, dst0 + r0 + rc)
                for c0 in range(0, WA, LANES):
                    v = (src[rows, oCg + c0:oCg + c0 + LANES].astype(F32)
                         * src[rows, oVa + c0:oVa + c0 + LANES].astype(F32))
                    pbuf[drows, c0:c0 + LANES] = v if scale is None else v * scale
                for c0 in range(0, WB, LANES):
                    v = (src[rows, oVal + c0:oVal + c0 + LANES].astype(F32)
                         * _sigmoid(src[rows, oGate + c0:oGate + c0 + LANES].astype(F32)))
                    gbuf[drows, c0:c0 + LANES] = v if scale is None else v * scale
                for c0 in range(0, WC, LANES):
                    v = src[rows, oC + c0:oC + c0 + LANES].astype(F32)
                    xbuf[drows, c0:c0 + LANES] = v if scale is None else v * scale

        fill(uh_ref, 0, HB, hv)
        fill(u_ref, HB, tm, None)

        for c0 in range(0, WA, LANES):
            w = [wa_ref[k:k + 1, c0:c0 + LANES] for k in range(K_A)]
            for r0, rc in _chunks(tm):
                q = _tap_sum(pbuf, HB, offs_a, w, r0, rc, c0)
                bg = u_ref[r0:r0 + rc, c0:c0 + LANES].astype(F32)
                y_ref[r0:r0 + rc, c0:c0 + LANES] = (bg * q).astype(BF16)

        for c0 in range(0, WB, LANES):
            w = [wb_ref[k:k + 1, c0:c0 + LANES] for k in range(K_B)]
            bias = bb_ref[:, c0:c0 + LANES]
            for r0, rc in _chunks(tm):
                cbuf[r0:r0 + rc, c0:c0 + LANES] = _tap_sum(gbuf, HB, offs_b, w, r0, rc, c0) + bias
        for r0, rc in _chunks(tm, 32):
            cb = cbuf[r0:r0 + rc, :]
            mu = jnp.mean(cb, axis=-1, keepdims=True)
            d = cb - mu
            n = d * lax.rsqrt(jnp.mean(d * d, axis=-1, keepdims=True) + LN_EPS)
            z = n * lg_ref[...] + lb_ref[...]
            y_ref[r0:r0 + rc, WA:WA + WB] = (z * _sigmoid(z)).astype(BF16)

        for g, win in enumerate(POOL_WINDOWS):
            c0 = g * LANES
            for r0, rc in _chunks(tm):
                s = _tap_sum(xbuf, HB, [-j for j in range(win)], None, r0, rc, c0)
                pooled = s / _row_counts(i * tm + r0, rc, win) - xbuf[HB + r0:HB + r0 + rc, c0:c0 + LANES]
                plbuf[r0:r0 + rc, c0:c0 + LANES] = pooled.astype(BF16)
            mixed = jnp.dot(plbuf[:, c0:c0 + LANES], pw_ref[g], preferred_element_type=F32)
            y_ref[:, WA + WB + c0:WA + WB + c0 + LANES] = (mixed * ps_ref[:, c0:c0 + LANES]).astype(BF16)

    full = lambda a: pl.BlockSpec(a.shape, lambda i: (0,) * a.ndim)
    return pl.pallas_call(
        body, grid=(S // tm,),
        in_specs=[pl.BlockSpec((tm, DIN), lambda i: (i, 0)),
                  pl.BlockSpec((HB, DIN), lambda i: (jnp.maximum(i * r - 1, 0), 0)),
                  full(wa), full(wb), full(bb), full(lg), full(lb), full(pw), full(ps)],
        out_specs=pl.BlockSpec((tm, DMIX), lambda i: (i, 0)),
        out_shape=jax.ShapeDtypeStruct((S, DMIX), BF16),
        scratch_shapes=[pltpu.VMEM((HB + tm, WA), F32), pltpu.VMEM((HB + tm, WB), F32),
                        pltpu.VMEM((tm, WB), F32), pltpu.VMEM((HB + tm, WC), F32),
                        pltpu.VMEM((tm, WC), BF16)],
        compiler_params=_cp(1), name=name,
    )(u, u, wa, wb, bb, lg, lb, pw, ps)


def mixer_bwd(u, dy, wa, wb, bb, lg, lb, pw, ps, *, name):
    S, DIN = u.shape
    WA, WB, WC = wa.shape[1], wb.shape[1], ps.shape[1]
    NG = WC // LANES
    DMIX = WA + WB + WC
    tm = _tile(S, TM_MIXER, HALO_MIX)
    HB = HALO_MIX
    r = tm // HB
    nI = S // tm
    nH = S // HB
    oCg, oVa, oVal, oGate, oC = WA, 2 * WA, 3 * WA, 3 * WA + WB, 3 * WA + 2 * WB
    offs_a = [k - (K_A - 1) for k in range(K_A)]
    offs_b = [k - (K_B - 1) for k in range(K_B)]
    adj_a = [(K_A - 1) - k for k in range(K_A)]
    adj_b = [(K_B - 1) - k for k in range(K_B)]

    def body(u_ref, ub_ref, ua_ref, dy_ref, dya_ref, wa_ref, wb_ref, bb_ref, lg_ref, lb_ref, pw_ref, ps_ref,
             du_ref, dwa_ref, dwb_ref, dbb_ref, dlg_ref, dlb_ref, dpw_ref, dps_ref,
             pbuf, dqbuf, gbuf, cbuf, dcbuf, xbuf, plbuf, dmbuf, dplbuf, dpcbuf):
        i = pl.program_id(0)
        first = i == 0
        hvb = jnp.where(i > 0, 1.0, 0.0).astype(F32)
        hva = jnp.where(i < nI - 1, 1.0, 0.0).astype(F32)

        def fill(src, dst0, n, scale, want_px):
            for r0, rc in _chunks(n):
                rows, drows = slice(r0, r0 + rc), slice(dst0 + r0, dst0 + r0 + rc)
                if want_px:
                    for c0 in range(0, WA, LANES):
                        v = (src[rows, oCg + c0:oCg + c0 + LANES].astype(F32)
                             * src[rows, oVa + c0:oVa + c0 + LANES].astype(F32))
                        pbuf[drows, c0:c0 + LANES] = v if scale is None else v * scale
                    for c0 in range(0, WC, LANES):
                        v = src[rows, oC + c0:oC + c0 + LANES].astype(F32)
                        xbuf[drows, c0:c0 + LANES] = v if scale is None else v * scale
                for c0 in range(0, WB, LANES):
                    v = (src[rows, oVal + c0:oVal + c0 + LANES].astype(F32)
                         * _sigmoid(src[rows, oGate + c0:oGate + c0 + LANES].astype(F32)))
                    gbuf[drows, c0:c0 + LANES] = v if scale is None else v * scale

        fill(ub_ref, 0, HB, hvb, True)
        fill(u_ref, HB, tm, None, True)
        fill(ua_ref, HB + tm, HB, None, False)

        for r0, rc in _chunks(tm):
            for c0 in range(0, WA, LANES):
                dqbuf[r0:r0 + rc, c0:c0 + LANES] = (dy_ref[r0:r0 + rc, c0:c0 + LANES].astype(F32)
                                                     * u_ref[r0:r0 + rc, c0:c0 + LANES].astype(F32))
        for c0 in range(0, WA, LANES):
            dqbuf[tm:tm + HB, c0:c0 + LANES] = (dya_ref[:, c0:c0 + LANES].astype(F32)
                                                * ua_ref[:, c0:c0 + LANES].astype(F32)) * hva
        for c0 in range(0, WA, LANES):
            w = [wa_ref[k:k + 1, c0:c0 + LANES] for k in range(K_A)]
            for r0, rc in _chunks(tm):
                rows = slice(r0, r0 + rc)
                q = _tap_sum(pbuf, HB, offs_a, w, r0, rc, c0)
                du_ref[rows, c0:c0 + LANES] = (dy_ref[rows, c0:c0 + LANES].astype(F32) * q).astype(BF16)
                dp = _tap_sum(dqbuf, 0, adj_a, w, r0, rc, c0)
                cg = u_ref[rows, oCg + c0:oCg + c0 + LANES].astype(F32)
                va = u_ref[rows, oVa + c0:oVa + c0 + LANES].astype(F32)
                du_ref[rows, oCg + c0:oCg + c0 + LANES] = (dp * va).astype(BF16)
                du_ref[rows, oVa + c0:oVa + c0 + LANES] = (dp * cg).astype(BF16)
            _tap_wgrad(dwa_ref, 0, dqbuf, 0, pbuf, HB, offs_a, tm, c0, first)

        for c0 in range(0, WB, LANES):
            w = [wb_ref[k:k + 1, c0:c0 + LANES] for k in range(K_B)]
            bias = bb_ref[:, c0:c0 + LANES]
            for r0, rc in _chunks(tm + HB):
                cbuf[r0:r0 + rc, c0:c0 + LANES] = _tap_sum(gbuf, HB, offs_b, w, r0, rc, c0) + bias

        def ln_chunk(r0, rc, dyb, scale, main):
            cb = cbuf[r0:r0 + rc, :]
            mu = jnp.mean(cb, axis=-1, keepdims=True)
            d = cb - mu
            rs = lax.rsqrt(jnp.mean(d * d, axis=-1, keepdims=True) + LN_EPS)
            n = d * rs
            z = n * lg_ref[...] + lb_ref[...]
            sg = _sigmoid(z)
            dz = dyb * (sg * (1.0 + z * (1.0 - sg)))
            dn = dz * lg_ref[...]
            dcb = rs * (dn - jnp.mean(dn, axis=-1, keepdims=True) - n * jnp.mean(dn * n, axis=-1, keepdims=True))
            if scale is not None:
                dcb = dcb * scale
            dcbuf[r0:r0 + rc, :] = dcb
            if main:
                return (jnp.sum(dz * n, axis=0, keepdims=True), jnp.sum(dz, axis=0, keepdims=True),
                        jnp.sum(dcb, axis=0, keepdims=True))
            return None

        sums = None
        for r0, rc in _chunks(tm, 32):
            part = ln_chunk(r0, rc, dy_ref[r0:r0 + rc, WA:WA + WB].astype(F32), None, True)
            sums = part if sums is None else tuple(a + b for a, b in zip(sums, part))
        ln_chunk(tm, HB, dya_ref[:, WA:WA + WB].astype(F32), hva, False)
        _acc_store(dlg_ref, (slice(None), slice(None)), sums[0], first)
        _acc_store(dlb_ref, (slice(None), slice(None)), sums[1], first)
        _acc_store(dbb_ref, (slice(None), slice(None)), sums[2], first)

        for c0 in range(0, WB, LANES):
            w = [wb_ref[k:k + 1, c0:c0 + LANES] for k in range(K_B)]
            for r0, rc in _chunks(tm):
                rows = slice(r0, r0 + rc)
                dglu = _tap_sum(dcbuf, 0, adj_b, w, r0, rc, c0)
                val = u_ref[rows, oVal + c0:oVal + c0 + LANES].astype(F32)
                sg = _sigmoid(u_ref[rows, oGate + c0:oGate + c0 + LANES].astype(F32))
                du_ref[rows, oVal + c0:oVal + c0 + LANES] = (dglu * sg).astype(BF16)
                du_ref[rows, oGate + c0:oGate + c0 + LANES] = (dglu * val * sg * (1.0 - sg)).astype(BF16)
            _tap_wgrad(dwb_ref, 0, dcbuf, 0, gbuf, HB, offs_b, tm, c0, first)

        for g, win in enumerate(POOL_WINDOWS):
            c0 = g * LANES
            cols = slice(c0, c0 + LANES)
            ycols = slice(WA + WB + c0, WA + WB + c0 + LANES)
            for r0, rc in _chunks(tm):
                s = _tap_sum(xbuf, HB, [-j for j in range(win)], None, r0, rc, c0)
                pooled = s / _row_counts(i * tm + r0, rc, win) - xbuf[HB + r0:HB + r0 + rc, cols]
                plbuf[r0:r0 + rc, cols] = pooled.astype(BF16)
            mixed = jnp.dot(plbuf[:, cols], pw_ref[g], preferred_element_type=F32)
            dyc = dy_ref[:, ycols].astype(F32)
            _acc_store(dps_ref, (slice(None), cols), jnp.sum(dyc * mixed, axis=0, keepdims=True), first)
            dmbuf[0:tm, :] = (dyc * ps_ref[:, cols]).astype(BF16)
            dmbuf[tm:tm + HB, :] = (dya_ref[:, ycols].astype(F32) * ps_ref[:, cols] * hva).astype(BF16)
            dpw = lax.dot_general(plbuf[:, cols], dmbuf[0:tm, :], (((0,), (0,)), ((), ())),
                                  preferred_element_type=F32)
            _acc_store(dpw_ref, (g, slice(None), slice(None)), dpw, first)
            dplbuf[...] = lax.dot_general(dmbuf[...], pw_ref[g], (((1,), (1,)), ((), ())),
                                          preferred_element_type=F32)
            for r0, rc in _chunks(tm + HB):
                dpcbuf[r0:r0 + rc, :] = dplbuf[r0:r0 + rc, :] / _row_counts(i * tm + r0, rc, win)
            for r0, rc in _chunks(tm):
                duc = _tap_sum(dpcbuf, 0, list(range(win)), None, r0, rc, 0) - dplbuf[r0:r0 + rc, :]
                du_ref[r0:r0 + rc, oC + c0:oC + c0 + LANES] = duc.astype(BF16)

    full = lambda a: pl.BlockSpec(a.shape, lambda i: (0,) * a.ndim)
    acc = lambda shape: pl.BlockSpec(shape, lambda i: (0,) * len(shape))
    small = [(K_A, WA), (K_B, WB), (1, WB), (1, WB), (1, WB), (NG, LANES, LANES), (1, WC)]
    outs = pl.pallas_call(
        body, grid=(nI,),
        in_specs=[pl.BlockSpec((tm, DIN), lambda i: (i, 0)),
                  pl.BlockSpec((HB, DIN), lambda i: (jnp.maximum(i * r - 1, 0), 0)),
                  pl.BlockSpec((HB, DIN), lambda i: (jnp.minimum((i + 1) * r, nH - 1), 0)),
                  pl.BlockSpec((tm, DMIX), lambda i: (i, 0)),
                  pl.BlockSpec((HB, DMIX), lambda i: (jnp.minimum((i + 1) * r, nH - 1), 0)),
                  full(wa), full(wb), full(bb), full(lg), full(lb), full(pw), full(ps)],
        out_specs=[pl.BlockSpec((tm, DIN), lambda i: (i, 0))] + [acc(s) for s in small],
        out_shape=[jax.ShapeDtypeStruct((S, DIN), BF16)] + [jax.ShapeDtypeStruct(s, F32) for s in small],
        scratch_shapes=[pltpu.VMEM((HB + tm, WA), F32), pltpu.VMEM((tm + HB, WA), F32),
                        pltpu.VMEM((HB + tm + HB, WB), F32), pltpu.VMEM((tm + HB, WB), F32),
                        pltpu.VMEM((tm + HB, WB), F32), pltpu.VMEM((HB + tm, WC), F32),
                        pltpu.VMEM((tm, WC), BF16), pltpu.VMEM((tm + HB, LANES), BF16),
                        pltpu.VMEM((tm + HB, LANES), F32), pltpu.VMEM((tm + HB, LANES), F32)],
        compiler_params=_cp(1), name=name,
    )(u, u, u, dy, dy, wa, wb, bb, lg, lb, pw, ps)
    return outs


def ffn_fwd(up, wf, bf, *, name):
    S, F2 = up.shape
    F = F2 // 2
    tm = _tile(S, TM_FFN, HALO_FFN)
    HB = HALO_FFN
    r = tm // HB
    CW = _tile(F, 512)
    offs = [k - (K_F - 1) for k in range(K_F)]

    def body(up_ref, uph_ref, wf_ref, bf_ref, a_ref, ebuf):
        i = pl.program_id(0)
        hv = jnp.where(i > 0, 1.0, 0.0).astype(F32)
        for c0 in range(0, F, CW):
            for h, off in ((0, c0), (1, F + c0)):
                ebuf[h, 0:HB, :] = uph_ref[:, off:off + CW].astype(F32) * hv
                for r0, rc in _chunks(tm):
                    ebuf[h, HB + r0:HB + r0 + rc, :] = up_ref[r0:r0 + rc, off:off + CW].astype(F32)
            for l0 in range(0, CW, LANES):
                wg = [wf_ref[k:k + 1, c0 + l0:c0 + l0 + LANES] for k in range(K_F)]
                wv = [wf_ref[k:k + 1, F + c0 + l0:F + c0 + l0 + LANES] for k in range(K_F)]
                bg = bf_ref[:, c0 + l0:c0 + l0 + LANES]
                bv = bf_ref[:, F + c0 + l0:F + c0 + l0 + LANES]
                for r0, rc in _chunks(tm):
                    gt = _tap_sum(ebuf.at[0], HB, offs, wg, r0, rc, l0) + bg
                    vl = _tap_sum(ebuf.at[1], HB, offs, wv, r0, rc, l0) + bv
                    a_ref[r0:r0 + rc, c0 + l0:c0 + l0 + LANES] = (gt * _sigmoid(gt) * vl).astype(BF16)

    full = lambda a: pl.BlockSpec(a.shape, lambda i: (0,) * a.ndim)
    return pl.pallas_call(
        body, grid=(S // tm,),
        in_specs=[pl.BlockSpec((tm, F2), lambda i: (i, 0)),
                  pl.BlockSpec((HB, F2), lambda i: (jnp.maximum(i * r - 1, 0), 0)),
                  full(wf), full(bf)],
        out_specs=pl.BlockSpec((tm, F), lambda i: (i, 0)),
        out_shape=jax.ShapeDtypeStruct((S, F), BF16),
        scratch_shapes=[pltpu.VMEM((2, HB + tm, CW), F32)],
        compiler_params=_cp(1), name=name,
    )(up, up, wf, bf)


def ffn_bwd(up, da, wf, bf, *, name):
    S, F2 = up.shape
    F = F2 // 2
    tm = _tile(S, TM_FFN, HALO_FFN)
    HB = HALO_FFN
    r = tm // HB
    nI = S // tm
    nH = S // HB
    CW = _tile(F, 512)
    offs = [k - (K_F - 1) for k in range(K_F)]
    adj = [(K_F - 1) - k for k in range(K_F)]

    def body(up_ref, upb_ref, upa_ref, da_ref, daa_ref, wf_ref, bf_ref, dup_ref, dwf_ref, dbf_ref, ebuf, dbuf):
        i = pl.program_id(0)
        first = i == 0
        hvb = jnp.where(i > 0, 1.0, 0.0).astype(F32)
        hva = jnp.where(i < nI - 1, 1.0, 0.0).astype(F32)
        for c0 in range(0, F, CW):
            for h, off in ((0, c0), (1, F + c0)):
                ebuf[h, 0:HB, :] = upb_ref[:, off:off + CW].astype(F32) * hvb
                for r0, rc in _chunks(tm):
                    ebuf[h, HB + r0:HB + r0 + rc, :] = up_ref[r0:r0 + rc, off:off + CW].astype(F32)
                ebuf[h, HB + tm:HB + tm + HB, :] = upa_ref[:, off:off + CW].astype(F32)
            for l0 in range(0, CW, LANES):
                cg, cv = c0 + l0, F + c0 + l0
                wg = [wf_ref[k:k + 1, cg:cg + LANES] for k in range(K_F)]
                wv = [wf_ref[k:k + 1, cv:cv + LANES] for k in range(K_F)]
                bg = bf_ref[:, cg:cg + LANES]
                bv = bf_ref[:, cv:cv + LANES]
                sg_sum, sv_sum = None, None
                for r0, rc in _chunks(tm + HB):
                    gt = _tap_sum(ebuf.at[0], HB, offs, wg, r0, rc, l0) + bg
                    vl = _tap_sum(ebuf.at[1], HB, offs, wv, r0, rc, l0) + bv
                    s = _sigmoid(gt)
                    if r0 < tm:
                        d = da_ref[r0:r0 + rc, cg:cg + LANES].astype(F32)
                    else:
                        d = daa_ref[:, cg:cg + LANES].astype(F32) * hva
                    dg = d * vl * (s * (1.0 + gt * (1.0 - s)))
                    dv = d * (gt * s)
                    dbuf[0, r0:r0 + rc, l0:l0 + LANES] = dg
                    dbuf[1, r0:r0 + rc, l0:l0 + LANES] = dv
                    if r0 < tm:
                        pg, pv = jnp.sum(dg, axis=0, keepdims=True), jnp.sum(dv, axis=0, keepdims=True)
                        sg_sum = pg if sg_sum is None else sg_sum + pg
                        sv_sum = pv if sv_sum is None else sv_sum + pv
                _acc_store(dbf_ref, (slice(None), slice(cg, cg + LANES)), sg_sum, first)
                _acc_store(dbf_ref, (slice(None), slice(cv, cv + LANES)), sv_sum, first)
                for r0, rc in _chunks(tm):
                    dup_ref[r0:r0 + rc, cg:cg + LANES] = _tap_sum(dbuf.at[0], 0, adj, wg, r0, rc, l0).astype(BF16)
                    dup_ref[r0:r0 + rc, cv:cv + LANES] = _tap_sum(dbuf.at[1], 0, adj, wv, r0, rc, l0).astype(BF16)
                _tap_wgrad(dwf_ref, c0, dbuf.at[0], 0, ebuf.at[0], HB, offs, tm, l0, first)
                _tap_wgrad(dwf_ref, F + c0, dbuf.at[1], 0, ebuf.at[1], HB, offs, tm, l0, first)

    full = lambda a: pl.BlockSpec(a.shape, lambda i: (0,) * a.ndim)
    return pl.pallas_call(
        body, grid=(nI,),
        in_specs=[pl.BlockSpec((tm, F2), lambda i: (i, 0)),
                  pl.BlockSpec((HB, F2), lambda i: (jnp.maximum(i * r - 1, 0), 0)),
                  pl.BlockSpec((HB, F2), lambda i: (jnp.minimum((i + 1) * r, nH - 1), 0)),
                  pl.BlockSpec((tm, F), lambda i: (i, 0)),
                  pl.BlockSpec((HB, F), lambda i: (jnp.minimum((i + 1) * r, nH - 1), 0)),
                  full(wf), full(bf)],
        out_specs=[pl.BlockSpec((tm, F2), lambda i: (i, 0)),
                   pl.BlockSpec((K_F, F2), lambda i: (0, 0)), pl.BlockSpec((1, F2), lambda i: (0, 0))],
        out_shape=[jax.ShapeDtypeStruct((S, F2), BF16), jax.ShapeDtypeStruct((K_F, F2), F32),
                   jax.ShapeDtypeStruct((1, F2), F32)],
        scratch_shapes=[pltpu.VMEM((2, HB + tm + HB, CW), F32), pltpu.VMEM((2, tm + HB, CW), F32)],
        compiler_params=_cp(1), name=name,
    )(up, up, up, da, da, wf, bf)


HBM = pl.BlockSpec(memory_space=pltpu.HBM)


def _place():
    return lax.axis_index("x"), lax.axis_index("y"), lax.axis_index("c")


def allgather8(buf, *, name):
    R, C = buf.shape

    def body(x_ref, o_ref, send_sems, recv_sems, local_sem):
        x, y, c = _place()
        me = 4 * x + 2 * y + c
        mine = pltpu.make_async_copy(x_ref, o_ref.at[me], local_sem)
        mine.start()
        sends = []
        for k in range(1, 8):
            fx, fy, fc = (k >> 2) & 1, (k >> 1) & 1, k & 1
            px, py, pc = (x + fx) % 2, (y + fy) % 2, (c + fc) % 2
            cp = pltpu.make_async_remote_copy(
                src_ref=x_ref, dst_ref=o_ref.at[me], send_sem=send_sems.at[k - 1], recv_sem=recv_sems.at[k - 1],
                device_id=(px, py, pc), device_id_type=MESH)
            cp.start()
            sends.append(cp)
        for k in range(1, 8):
            fx, fy, fc = (k >> 2) & 1, (k >> 1) & 1, k & 1
            peer = 4 * ((x + fx) % 2) + 2 * ((y + fy) % 2) + (c + fc) % 2
            pltpu.make_async_remote_copy(
                src_ref=x_ref, dst_ref=o_ref.at[peer], send_sem=send_sems.at[k - 1], recv_sem=recv_sems.at[k - 1],
                device_id=(x, y, c), device_id_type=MESH).wait_recv()
        for cp in sends:
            cp.wait_send()
        mine.wait()

    return pl.pallas_call(
        body, in_specs=[HBM], out_specs=HBM, out_shape=jax.ShapeDtypeStruct((8, R, C), buf.dtype),
        scratch_shapes=[pltpu.SemaphoreType.DMA((7,)), pltpu.SemaphoreType.DMA((7,)), pltpu.SemaphoreType.DMA],
        name=name,
    )(buf)


def gather_chip_shards(shards, *, name):
    n = len(shards)

    def body(*refs):
        ins, outs = refs[:n], refs[n:2 * n]
        send_sems, recv_sems, local_sems = refs[2 * n:]
        x, y, c = _place()
        b = 2 * x + y
        chips = [(1 - x, y), (x, 1 - y), (1 - x, 1 - y)]
        locals_, sends = [], []
        for a in range(n):
            cp = pltpu.make_async_copy(ins[a], outs[a].at[b], local_sems.at[a])
            cp.start()
            locals_.append(cp)

        def half(a, which):
            rh = ins[a].shape[0] // 2
            return pl.ds(pl.multiple_of(which * rh, 16), rh)

        def copy(a, k, src, dst, to):
            return pltpu.make_async_remote_copy(
                src_ref=src, dst_ref=dst, send_sem=send_sems.at[6 * a + k], recv_sem=recv_sems.at[6 * a + k],
                device_id=to, device_id_type=MESH)

        for a in range(n):
            for j, (cx, cy) in enumerate(chips):
                cp = copy(a, j, ins[a].at[half(a, c)], outs[a].at[b, half(a, c)], (cx, cy, c))
                cp.start()
                sends.append(cp)
        for a in range(n):
            for j, (cx, cy) in enumerate(chips):
                got = outs[a].at[2 * cx + cy, half(a, c)]
                copy(a, j, got, got, (x, y, c)).wait_recv()
                cp = copy(a, 3 + j, got, got, (x, y, 1 - c))
                cp.start()
                sends.append(cp)
        for a in range(n):
            for j, (cx, cy) in enumerate(chips):
                got = outs[a].at[2 * cx + cy, half(a, 1 - c)]
                copy(a, 3 + j, got, got, (x, y, c)).wait_recv()
        for cp in sends:
            cp.wait_send()
        for cp in locals_:
            cp.wait()

    return pl.pallas_call(
        body, in_specs=[HBM] * n, out_specs=[HBM] * n,
        out_shape=[jax.ShapeDtypeStruct((4,) + s.shape, s.dtype) for s in shards],
        scratch_shapes=[pltpu.SemaphoreType.DMA((6 * n,)), pltpu.SemaphoreType.DMA((6 * n,)),
                        pltpu.SemaphoreType.DMA((n,))],
        name=name,
    )(*shards)


def sibling_swap_halves(gs, *, name):
    n = len(gs)

    def body(*refs):
        ins, outs = refs[:n], refs[n:2 * n]
        send_sems, recv_sems = refs[2 * n:]
        x, y, c = _place()
        cps = []
        for a in range(n):
            rh = ins[a].shape[1] // 2
            src = ins[a].at[:, pl.ds(pl.multiple_of((1 - c) * rh, 16), rh)]
            cp = pltpu.make_async_remote_copy(
                src_ref=src, dst_ref=outs[a], send_sem=send_sems.at[a], recv_sem=recv_sems.at[a],
                device_id=(x, y, 1 - c), device_id_type=MESH)
            cp.start()
            cps.append(cp)
        for cp in cps:
            cp.wait()

    return pl.pallas_call(
        body, in_specs=[HBM] * n, out_specs=[HBM] * n,
        out_shape=[jax.ShapeDtypeStruct((4, g.shape[1] // 2, g.shape[2]), g.dtype) for g in gs],
        scratch_shapes=[pltpu.SemaphoreType.DMA((n,)), pltpu.SemaphoreType.DMA((n,))],
        name=name,
    )(*gs)


def chip_exchange(ps, *, name):
    n = len(ps)

    def body(*refs):
        ins, outs = refs[:n], refs[n:2 * n]
        send_sems, recv_sems, local_sems = refs[2 * n:]
        x, y, c = _place()
        b = 2 * x + y
        chips = [(1 - x, y), (x, 1 - y), (1 - x, 1 - y)]
        locals_, sends = [], []
        for a in range(n):
            cp = pltpu.make_async_copy(ins[a].at[b], outs[a].at[b], local_sems.at[a])
            cp.start()
            locals_.append(cp)
        for a in range(n):
            for j, (cx, cy) in enumerate(chips):
                cp = pltpu.make_async_remote_copy(
                    src_ref=ins[a].at[2 * cx + cy], dst_ref=outs[a].at[b],
                    send_sem=send_sems.at[3 * a + j], recv_sem=recv_sems.at[3 * a + j],
                    device_id=(cx, cy, c), device_id_type=MESH)
                cp.start()
                sends.append(cp)
        for a in range(n):
            for j, (cx, cy) in enumerate(chips):
                got = outs[a].at[2 * cx + cy]
                pltpu.make_async_remote_copy(
                    src_ref=got, dst_ref=got, send_sem=send_sems.at[3 * a + j], recv_sem=recv_sems.at[3 * a + j],
                    device_id=(x, y, c), device_id_type=MESH).wait_recv()
        for cp in sends:
            cp.wait_send()
        for cp in locals_:
            cp.wait()

    return pl.pallas_call(
        body, in_specs=[HBM] * n, out_specs=[HBM] * n,
        out_shape=[jax.ShapeDtypeStruct(p.shape, p.dtype) for p in ps],
        scratch_shapes=[pltpu.SemaphoreType.DMA((3 * n,)), pltpu.SemaphoreType.DMA((3 * n,)),
                        pltpu.SemaphoreType.DMA((n,))],
        name=name,
    )(*ps)


def sibling_join_halves(fs, *, name):
    n = len(fs)

    def body(*refs):
        ins, outs = refs[:n], refs[n:2 * n]
        send_sems, recv_sems, local_sems = refs[2 * n:]
        x, y, c = _place()
        locals_, sends = [], []
        for a in range(n):
            rh = ins[a].shape[0]
            mine = outs[a].at[pl.ds(pl.multiple_of(c * rh, 8), rh)]
            cp = pltpu.make_async_copy(ins[a], mine, local_sems.at[a])
            cp.start()
            locals_.append(cp)
            cp = pltpu.make_async_remote_copy(
                src_ref=ins[a], dst_ref=mine, send_sem=send_sems.at[a], recv_sem=recv_sems.at[a],
                device_id=(x, y, 1 - c), device_id_type=MESH)
            cp.start()
            sends.append(cp)
        for a in range(n):
            rh = ins[a].shape[0]
            other = outs[a].at[pl.ds(pl.multiple_of((1 - c) * rh, 8), rh)]
            pltpu.make_async_remote_copy(
                src_ref=ins[a], dst_ref=other, send_sem=send_sems.at[a], recv_sem=recv_sems.at[a],
                device_id=(x, y, c), device_id_type=MESH).wait_recv()
        for cp in sends:
            cp.wait_send()
        for cp in locals_:
            cp.wait()

    return pl.pallas_call(
        body, in_specs=[HBM] * n, out_specs=[HBM] * n,
        out_shape=[jax.ShapeDtypeStruct((2 * f.shape[0], f.shape[1]), f.dtype) for f in fs],
        scratch_shapes=[pltpu.SemaphoreType.DMA((n,)), pltpu.SemaphoreType.DMA((n,)),
                        pltpu.SemaphoreType.DMA((n,))],
        name=name,
    )(*fs)


def pair_add(g, t, core, *, name):
    _, R, C = g.shape
    rh = R // 2
    tr = _tile(rh, 256, 16)
    nh = rh // tr

    def body(c_ref, g_ref, t_ref, o_ref):
        o_ref[...] = (g_ref[...].astype(F32) + t_ref[...].astype(F32)).astype(BF16)

    return pl.pallas_call(
        body,
        grid_spec=pltpu.PrefetchScalarGridSpec(
            num_scalar_prefetch=1, grid=(4, nh),
            in_specs=[pl.BlockSpec((None, tr, C), lambda b, i, c_ref: (b, c_ref[0] * nh + i, 0)),
                      pl.BlockSpec((None, tr, C), lambda b, i, c_ref: (b, i, 0))],
            out_specs=pl.BlockSpec((None, tr, C), lambda b, i, c_ref: (b, i, 0))),
        out_shape=jax.ShapeDtypeStruct((4, rh, C), BF16), compiler_params=_cp(2), name=name,
    )(core, g, t)


def sum_slots(q, *, name):
    N, R, C = q.shape
    tr = _tile(R, 256, 16)

    def body(q_ref, o_ref):
        acc = q_ref[0].astype(F32)
        for s in range(1, N):
            acc = acc + q_ref[s].astype(F32)
        o_ref[...] = acc

    return pl.pallas_call(
        body, grid=(R // tr,), in_specs=[pl.BlockSpec((N, tr, C), lambda i: (0, i, 0))],
        out_specs=pl.BlockSpec((tr, C), lambda i: (i, 0)),
        out_shape=jax.ShapeDtypeStruct((R, C), F32), compiler_params=_cp(1), name=name,
    )(q)


def _adam_math(w, g, m, v):
    m = ADAM_B1 * m + (1.0 - ADAM_B1) * g
    v = ADAM_B2 * v + (1.0 - ADAM_B2) * (g * g)
    m_hat = m / (1.0 - ADAM_B1 ** ADAM_STEP)
    v_hat = v / (1.0 - ADAM_B2 ** ADAM_STEP)
    delta = -ADAM_LR * (m_hat / (jnp.sqrt(v_hat) + ADAM_EPS) + ADAM_WD * w)
    return delta, m, v


def adam_stacked(w, m, v, grads, *, name):
    L, R, C = w.shape
    tr = _tile(R, max(8, ADAM_BLOCK_BYTES // (4 * C)), 8)
    nr = R // tr

    def body(w_ref, m_ref, v_ref, *rest):
        g_refs, (go_ref, d_ref, mo_ref, vo_ref) = rest[:L], rest[L:]
        lid = pl.program_id(0)
        for l in range(L):
            @pl.when(lid == l)
            def _(l=l):
                g = g_refs[l][...]
                d, mn, vn = _adam_math(w_ref[...], g, m_ref[...], v_ref[...])
                go_ref[...] = g
                d_ref[...] = d
                mo_ref[...] = mn
                vo_ref[...] = vn

    st = pl.BlockSpec((None, tr, C), lambda l, i: (l, i, 0))
    g_specs = [pl.BlockSpec((tr, C), functools.partial(lambda l, i, ll: (jnp.where(l == ll, i, 0), 0), ll=ll))
               for ll in range(L)]
    return pl.pallas_call(
        body, grid=(L, nr), in_specs=[st, st, st] + g_specs, out_specs=[st] * 4,
        out_shape=[jax.ShapeDtypeStruct((L, R, C), F32)] * 4, compiler_params=_cp(2), name=name,
    )(w, m, v, *grads)


def adam_flat(w, g, m, v, *, name):
    R, C = w.shape
    tr = _tile(R, 512, 8)

    def body(w_ref, g_ref, m_ref, v_ref, d_ref, mo_ref, vo_ref):
        d, mn, vn = _adam_math(w_ref[...], g_ref[...], m_ref[...], v_ref[...])
        d_ref[...] = d
        mo_ref[...] = mn
        vo_ref[...] = vn

    row = pl.BlockSpec((tr, C), lambda i: (i, 0))
    return pl.pallas_call(
        body, grid=(R // tr,), in_specs=[row] * 4, out_specs=[row] * 3,
        out_shape=[jax.ShapeDtypeStruct((R, C), F32)] * 3, compiler_params=_cp(1), name=name,
    )(w, g, m, v)


PACK_ROWS = 64


def _pack(arrays):
    flat = jnp.concatenate([a.reshape(-1).astype(F32) for a in arrays])
    n = flat.shape[0]
    unit = PACK_ROWS * LANES
    pad = (-n) % unit
    return jnp.pad(flat, (0, pad)).reshape(-1, LANES)


def _unpack(buf, shapes):
    flat = buf.reshape(-1)
    out, o = [], 0
    for s in shapes:
        n = 1
        for d in s:
            n *= d
        out.append(flat[o:o + n].reshape(s))
        o += n
    return out


def kernel(x, norm_mix_pre, norm_mix_post, norm_ffn_pre, norm_ffn_post, w_in, conv_a_w, conv_b_w, conv_b_bias, ln_b_gain, ln_b_bias, pool_w, pool_scale, w_out, w_up, conv_ffn_w, conv_ffn_bias, w_down, loss_target, m_norm_mix_pre, m_norm_mix_post, m_norm_ffn_pre, m_norm_ffn_post, m_w_in, m_conv_a_w, m_conv_b_w, m_conv_b_bias, m_ln_b_gain, m_ln_b_bias, m_pool_w, m_pool_scale, m_w_out, m_w_up, m_conv_ffn_w, m_conv_ffn_bias, m_w_down, v_norm_mix_pre, v_norm_mix_post, v_norm_ffn_pre, v_norm_ffn_post, v_w_in, v_conv_a_w, v_conv_b_w, v_conv_b_bias, v_ln_b_gain, v_ln_b_bias, v_pool_w, v_pool_scale, v_w_out, v_w_up, v_conv_ffn_w, v_conv_ffn_bias, v_w_down):
    L = w_in.shape[0]
    S, D = x.shape[1], x.shape[2]
    WA, WB, WC = 4 * conv_a_w.shape[2], 4 * conv_b_w.shape[2], pool_scale.shape[1]
    DIN, DMIX, F2 = 4 * w_in.shape[2], 4 * w_out.shape[1], 4 * w_up.shape[2]
    F = F2 // 2
    NG = WC // LANES
    xi, yi, ci = _place()
    chip = 2 * xi + yi
    core = jnp.reshape(ci, (1,)).astype(jnp.int32)

    conv_shapes = [(L, K_A, WA // 4), (L, K_B, WB // 4), (L, K_F, F2 // 4)]
    conv_all = allgather8(_pack([conv_a_w, conv_b_w, conv_ffn_w]), name="gather_conv_taps")
    per_chip = [_unpack(conv_all[2 * b], conv_shapes) for b in range(4)]
    wa_full, wb_full, wf_full = [jnp.concatenate([per_chip[b][k] for b in range(4)], axis=2) for k in range(3)]
    pw_bf = pool_w.astype(BF16)

    w_full = []
    for l in range(L):
        g_in, g_up, g_out, g_down = gather_chip_shards(
            [w_in[l].astype(BF16), w_up[l].astype(BF16), w_out[l].astype(BF16), w_down[l].astype(BF16)],
            name="gather_layer_weights")
        w_full.append((
            jnp.concatenate([g_in[b] for b in range(4)], axis=1),
            jnp.concatenate([g_up[b] for b in range(4)], axis=1),
            g_out.reshape(DMIX, D),
            g_down.reshape(F, D),
        ))

    def vec(a, l):
        return a[l].reshape(1, -1)

    x0 = x.reshape(S, D)
    h1 = norm_fwd(x0, vec(norm_mix_pre, 0), name="norm_first")
    saved = []
    for l in range(L):
        Win, Wup, Wout, Wdown = w_full[l]
        u = matmul(h1, Win, out_dtype=BF16, tm=512, tn=2176, tk=2048, j_outer=True, name="mm_in")
        ymix = mixer_fwd(u, wa_full[l], wb_full[l], vec(conv_b_bias, l), vec(ln_b_gain, l), vec(ln_b_bias, l),
                         pw_bf[l], vec(pool_scale, l), name="mixer_fwd")
        y = matmul(ymix, Wout, out_dtype=F32, tm=512, tn=2048, tk=2048, name="mm_out")
        x1, h2 = resid_norm_fwd(x0, y, vec(norm_mix_post, l), vec(norm_ffn_pre, l), emit_h=True, name="resid_norm_mid")
        up = matmul(h2, Wup, out_dtype=BF16, tm=512, tn=2816, tk=2048, j_outer=True, name="mm_up")
        a = ffn_fwd(up, wf_full[l], vec(conv_ffn_bias, l), name="ffn_fwd")
        f = matmul(a, Wdown, out_dtype=F32, tm=1024, tn=2048, tk=1408, name="mm_down")
        last = l == L - 1
        x2, h_next = resid_norm_fwd(x1, f, vec(norm_ffn_post, l), vec(norm_mix_pre, 0 if last else l + 1),
                                    emit_h=not last, name="resid_norm_last" if last else "resid_norm_end")
        saved.append((x0, h1, u, ymix, y, x1, h2, up, a, f))
        x0, h1 = x2, h_next

    dx, lsum = loss_head(x0, loss_target.reshape(S, D), name="loss_head")
    loss = lax.psum(lsum[0, 0] * (0.5 / D), ("x", "y", "c"))

    small = [None] * L
    big = [None] * L
    dt = dx
    _, df, _, dg4 = norm_bwd(dt, None, None, None, saved[L - 1][9], vec(norm_ffn_post, L - 1), name="norm_bwd_top")
    for l in reversed(range(L)):
        Win, Wup, Wout, Wdown = w_full[l]
        x0, h1, u, ymix, y, x1, h2, up, a, f = saved[l]
        da = matmul(df, Wdown, tb=True, out_dtype=BF16, tm=512, tn=1408, tk=2048, j_outer=True, name="mm_down_dx")
        g_down = matmul(a, df, ta=True, out_dtype=BF16, tm=1408, tn=2048, tk=512, name="mm_down_dw")
        dup, dwf, dbf = ffn_bwd(up, da, wf_full[l], vec(conv_ffn_bias, l), name="ffn_bwd")
        dh2 = matmul(dup, Wup, tb=True, out_dtype=F32, tm=1024, tn=2048, tk=1408, name="mm_up_dx")
        g_up = matmul(h2, dup, ta=True, out_dtype=BF16, tm=2048, tn=1408, tk=512, groups=4, name="mm_up_dw")
        dt, dy, dg3, dg2 = norm_bwd(dt, dh2, x1, vec(norm_ffn_pre, l), y, vec(norm_mix_post, l), name="norm_bwd_mid")
        dymix = matmul(dy, Wout, tb=True, out_dtype=BF16, tm=512, tn=2048, tk=2048, name="mm_out_dx")
        g_out = matmul(ymix, dy, ta=True, out_dtype=BF16, tm=2048, tn=1024, tk=512, name="mm_out_dw")
        du, dwa, dwb, dbb, dlg, dlb, dpw, dps = mixer_bwd(
            u, dymix, wa_full[l], wb_full[l], vec(conv_b_bias, l), vec(ln_b_gain, l), vec(ln_b_bias, l),
            pw_bf[l], vec(pool_scale, l), name="mixer_bwd")
        dh1 = matmul(du, Win, tb=True, out_dtype=F32, tm=512, tn=2048, tk=2176, name="mm_in_dx")
        g_in = matmul(h1, du, ta=True, out_dtype=BF16, tm=1024, tn=2176, tk=512, name="mm_in_dw")
        dg4_here = dg4
        if l > 0:
            dt, df, dg1, dg4 = norm_bwd(dt, dh1, x0, vec(norm_mix_pre, l), saved[l - 1][9], vec(norm_ffn_post, l - 1),
                                        name="norm_bwd_end")
        else:
            dt, _, dg1, _ = norm_bwd(dt, dh1, x0, vec(norm_mix_pre, 0), None, None, name="norm_bwd_bottom")
        small[l] = dict(norm_mix_pre=dg1, norm_mix_post=dg2, norm_ffn_pre=dg3, norm_ffn_post=dg4_here,
                        conv_a_w=dwa, conv_b_w=dwb, conv_b_bias=dbb, ln_b_gain=dlg, ln_b_bias=dlb,
                        pool_w=dpw, pool_scale=dps, conv_ffn_w=dwf, conv_ffn_bias=dbf)

        gs = [g_in.reshape(D, 4, DIN // 4).transpose(1, 0, 2), g_up,
              g_out.reshape(4, DMIX // 4, D), g_down.reshape(4, F // 4, D)]
        ts = sibling_swap_halves(gs, name="grad_swap_halves")
        ps = [pair_add(g, t, core, name="grad_pair_add_%d" % k) for k, (g, t) in enumerate(zip(gs, ts))]
        qs = chip_exchange(ps, name="grad_chip_exchange")
        fh = [sum_slots(q, name="grad_sum_slots_%d" % k) for k, q in enumerate(qs)]
        big[l] = sibling_join_halves(fh, name="grad_join_halves")
    grad_x = dt.reshape(1, S, D)

    rep_names = ["norm_mix_pre", "norm_mix_post", "norm_ffn_pre", "norm_ffn_post", "conv_b_bias", "ln_b_gain",
                 "ln_b_bias", "pool_w", "pool_scale", "conv_ffn_bias"]
    shd_names = ["conv_a_w", "conv_b_w", "conv_ffn_w"]
    given = dict(
        norm_mix_pre=(norm_mix_pre, m_norm_mix_pre, v_norm_mix_pre), norm_mix_post=(norm_mix_post, m_norm_mix_post, v_norm_mix_post),
        norm_ffn_pre=(norm_ffn_pre, m_norm_ffn_pre, v_norm_ffn_pre), norm_ffn_post=(norm_ffn_post, m_norm_ffn_post, v_norm_ffn_post),
        conv_b_bias=(conv_b_bias, m_conv_b_bias, v_conv_b_bias), ln_b_gain=(ln_b_gain, m_ln_b_gain, v_ln_b_gain),
        ln_b_bias=(ln_b_bias, m_ln_b_bias, v_ln_b_bias), pool_w=(pool_w, m_pool_w, v_pool_w),
        pool_scale=(pool_scale, m_pool_scale, v_pool_scale), conv_ffn_bias=(conv_ffn_bias, m_conv_ffn_bias, v_conv_ffn_bias),
        conv_a_w=(conv_a_w, m_conv_a_w, v_conv_a_w), conv_b_w=(conv_b_w, m_conv_b_w, v_conv_b_w),
        conv_ffn_w=(conv_ffn_w, m_conv_ffn_w, v_conv_ffn_w))
    full_shape = dict(conv_a_w=(L, K_A, WA), conv_b_w=(L, K_B, WB), conv_ffn_w=(L, K_F, F2))
    for nme in rep_names:
        full_shape[nme] = given[nme][0].shape
    names = rep_names + shd_names
    stacked = [jnp.stack([small[l][nme] for l in range(L)]).reshape(full_shape[nme]) for nme in names]
    parts = allgather8(_pack(stacked), name="gather_small_grads")
    totals = _unpack(sum_slots(parts, name="sum_small_grads"), [full_shape[nme] for nme in names])
    total = dict(zip(names, totals))
    for nme in shd_names:
        wd = full_shape[nme][2] // 4
        total[nme] = lax.dynamic_slice_in_dim(total[nme], chip * wd, wd, axis=2)
    shapes = [given[nme][0].shape for nme in names]
    d_s, m_s, v_s = adam_flat(_pack([given[nme][0] for nme in names]), _pack([total[nme] for nme in names]),
                              _pack([given[nme][1] for nme in names]), _pack([given[nme][2] for nme in names]),
                              name="adam_small")
    res = dict(zip(names, zip([total[nme] for nme in names], _unpack(d_s, shapes), _unpack(m_s, shapes),
                              _unpack(v_s, shapes))))

    for k, (nme, trio) in enumerate([("w_in", (w_in, m_w_in, v_w_in)), ("w_up", (w_up, m_w_up, v_w_up)),
                                     ("w_out", (w_out, m_w_out, v_w_out)), ("w_down", (w_down, m_w_down, v_w_down))]):
        res[nme] = adam_stacked(*trio, [big[l][k] for l in range(L)], name="adam_" + nme)

    order = ["norm_mix_pre", "norm_mix_post", "norm_ffn_pre", "norm_ffn_post", "w_in", "conv_a_w", "conv_b_w",
             "conv_b_bias", "ln_b_gain", "ln_b_bias", "pool_w", "pool_scale", "w_out", "w_up", "conv_ffn_w",
             "conv_ffn_bias", "w_down"]
    outs = [loss, grad_x]
    for k in range(4):
        outs += [res[nme][k] for nme in order]
    return tuple(outs)
```

```python
import functools

import jax
import jax.numpy as jnp
from jax import lax
from jax.experimental import pallas as pl
from jax.experimental.pallas import tpu as pltpu

F32 = jnp.float32
BF16 = jnp.bfloat16
MESH = pl.DeviceIdType.MESH

RMS_EPS = 1e-6
LN_EPS = 1e-5
ADAM_LR = 0.001
ADAM_B1 = 0.9
ADAM_B2 = 0.999
ADAM_EPS = 1e-08
ADAM_WD = 0.01
ADAM_STEP = 10
POOL_WINDOWS = (2, 4, 8, 16)
K_A = 3
K_B = 31
K_F = 3

LANES = 128
HALO_MIX = 32
HALO_FFN = 16
ROWS = 64
TM_MIXER = 512
TM_FFN = 256
TM_FFN_BWD = 128
TM_NORM = 256
ADAM_BLOCK_BYTES = 1 << 20
VMEM_LIMIT = 56 * 1024 * 1024


def _cp(n_axes):
    return pltpu.CompilerParams(dimension_semantics=("arbitrary",) * n_axes, vmem_limit_bytes=VMEM_LIMIT)


def _tile(dim, target, mult=LANES):
    if dim <= target:
        return dim
    t = (target // mult) * mult
    while t >= mult:
        if dim % t == 0:
            return t
        t -= mult
    return dim


def _chunks(n, rc=ROWS):
    out, r0 = [], 0
    while r0 < n:
        s = min(rc, n - r0)
        out.append((r0, s))
        r0 += s
    return out


def _sigmoid(x):
    return 1.0 / (1.0 + jnp.exp(-x))


def matmul(a, b, *, ta=False, tb=False, out_dtype, tm, tn, tk, j_outer=False, groups=1, name):
    if ta:
        K, M = a.shape
    else:
        M, K = a.shape
    if tb:
        N, K2 = b.shape
    else:
        K2, N = b.shape
    assert K == K2, (a.shape, b.shape)
    ng = N // groups
    tm, tn, tk = _tile(M, tm), _tile(ng, tn), _tile(K, tk)
    nm, nn, nk = M // tm, N // tn, K // tk
    per = ng // tn

    def ij(g0, g1):
        return (g1, g0) if j_outer else (g0, g1)

    def a_map(g0, g1, k):
        i, j = ij(g0, g1)
        return (k, i) if ta else (i, k)

    def b_map(g0, g1, k):
        i, j = ij(g0, g1)
        return (j, k) if tb else (k, j)

    def o_map(g0, g1, k):
        i, j = ij(g0, g1)
        return (j // per, i, j % per) if groups > 1 else (i, j)

    dims = (((0 if ta else 1,), (1 if tb else 0,)), ((), ()))
    use_acc = nk > 1 and out_dtype != F32

    def body(a_ref, b_ref, o_ref, *scratch):
        p = lax.dot_general(a_ref[...], b_ref[...], dims, preferred_element_type=F32)
        if nk == 1:
            o_ref[...] = p.astype(o_ref.dtype)
            return
        acc = scratch[0] if use_acc else o_ref
        k = pl.program_id(2)

        @pl.when(k == 0)
        def _():
            acc[...] = p

        @pl.when(k > 0)
        def _():
            acc[...] += p

        if use_acc:
            @pl.when(k == nk - 1)
            def _():
                o_ref[...] = acc[...].astype(o_ref.dtype)

    grid = (nn, nm, nk) if j_outer else (nm, nn, nk)
    if groups > 1:
        out_shape = jax.ShapeDtypeStruct((groups, M, ng), out_dtype)
        out_spec = pl.BlockSpec((None, tm, tn), o_map)
    else:
        out_shape = jax.ShapeDtypeStruct((M, N), out_dtype)
        out_spec = pl.BlockSpec((tm, tn), o_map)
    return pl.pallas_call(
        body, grid=grid,
        in_specs=[pl.BlockSpec((tk, tm) if ta else (tm, tk), a_map),
                  pl.BlockSpec((tn, tk) if tb else (tk, tn), b_map)],
        out_specs=out_spec, out_shape=out_shape,
        scratch_shapes=[pltpu.VMEM((tm, tn), F32)] if use_acc else [],
        compiler_params=_cp(3), name=name,
    )(a, b)


def _rms(v):
    return lax.rsqrt(jnp.mean(v * v, axis=-1, keepdims=True) + RMS_EPS)


def norm_fwd(x, g, *, name):
    S, D = x.shape
    tm = _tile(S, TM_NORM, 16)

    def body(x_ref, g_ref, h_ref):
        v = x_ref[...]
        h_ref[...] = (v * _rms(v) * g_ref[...]).astype(BF16)

    return pl.pallas_call(
        body, grid=(S // tm,),
        in_specs=[pl.BlockSpec((tm, D), lambda i: (i, 0)), pl.BlockSpec((1, D), lambda i: (0, 0))],
        out_specs=pl.BlockSpec((tm, D), lambda i: (i, 0)),
        out_shape=jax.ShapeDtypeStruct((S, D), BF16), compiler_params=_cp(1), name=name,
    )(x, g)


def resid_norm_fwd(x, y, gp, gn, *, emit_h, name):
    S, D = x.shape
    tm = _tile(S, TM_NORM, 16)

    def body(x_ref, y_ref, gp_ref, gn_ref, xn_ref, *rest):
        yv = y_ref[...]
        xn = x_ref[...] + yv * _rms(yv) * gp_ref[...]
        xn_ref[...] = xn
        if emit_h:
            rest[0][...] = (xn * _rms(xn) * gn_ref[...]).astype(BF16)

    row = pl.BlockSpec((tm, D), lambda i: (i, 0))
    vec = pl.BlockSpec((1, D), lambda i: (0, 0))
    outs = pl.pallas_call(
        body, grid=(S // tm,), in_specs=[row, row, vec, vec],
        out_specs=[row, row] if emit_h else [row],
        out_shape=[jax.ShapeDtypeStruct((S, D), F32)] + ([jax.ShapeDtypeStruct((S, D), BF16)] if emit_h else []),
        compiler_params=_cp(1), name=name,
    )(x, y, gp, gn)
    return (outs[0], outs[1]) if emit_h else (outs[0], None)


def _rms_bwd(v, g, dout):
    r = _rms(v)
    gd = g * dout
    dv = r * gd - v * (r * r * r) * jnp.mean(v * gd, axis=-1, keepdims=True)
    return dv, dout * v * r


def norm_bwd(d_direct, dh, xn, gn, y, gp, *, name):
    S, D = d_direct.shape
    tm = _tile(S, TM_NORM, 16)
    has_h, has_y = dh is not None, y is not None

    def body(*refs):
        refs = list(refs)
        dd_ref = refs.pop(0)
        if has_h:
            dh_ref, xn_ref, gn_ref = refs.pop(0), refs.pop(0), refs.pop(0)
        if has_y:
            y_ref, gp_ref = refs.pop(0), refs.pop(0)
        if has_h:
            dt_ref = refs.pop(0)
        if has_y:
            dy_ref = refs.pop(0)
        if has_h:
            dgn_ref = refs.pop(0)
        if has_y:
            dgp_ref = refs.pop(0)
        i = pl.program_id(0)
        dt = dd_ref[...]
        if has_h:
            dv, gterm = _rms_bwd(xn_ref[...], gn_ref[...], dh_ref[...])
            dt = dt + dv
            dt_ref[...] = dt
            part = jnp.sum(gterm, axis=0, keepdims=True)

            @pl.when(i == 0)
            def _():
                dgn_ref[...] = part

            @pl.when(i > 0)
            def _():
                dgn_ref[...] += part
        if has_y:
            dy, gterm = _rms_bwd(y_ref[...], gp_ref[...], dt)
            dy_ref[...] = dy.astype(BF16)
            part2 = jnp.sum(gterm, axis=0, keepdims=True)

            @pl.when(i == 0)
            def _():
                dgp_ref[...] = part2

            @pl.when(i > 0)
            def _():
                dgp_ref[...] += part2

    row = pl.BlockSpec((tm, D), lambda i: (i, 0))
    vec = pl.BlockSpec((1, D), lambda i: (0, 0))
    ins, in_specs = [d_direct], [row]
    if has_h:
        ins += [dh, xn, gn]
        in_specs += [row, row, vec]
    if has_y:
        ins += [y, gp]
        in_specs += [row, vec]
    out_specs, out_shape = [], []
    if has_h:
        out_specs.append(row)
        out_shape.append(jax.ShapeDtypeStruct((S, D), F32))
    if has_y:
        out_specs.append(row)
        out_shape.append(jax.ShapeDtypeStruct((S, D), BF16))
    if has_h:
        out_specs.append(vec)
        out_shape.append(jax.ShapeDtypeStruct((1, D), F32))
    if has_y:
        out_specs.append(vec)
        out_shape.append(jax.ShapeDtypeStruct((1, D), F32))
    outs = list(pl.pallas_call(
        body, grid=(S // tm,), in_specs=in_specs, out_specs=out_specs, out_shape=out_shape,
        compiler_params=_cp(1), name=name,
    )(*ins))
    dt = outs.pop(0) if has_h else d_direct
    dy = outs.pop(0) if has_y else None
    dgn = outs.pop(0) if has_h else None
    dgp = outs.pop(0) if has_y else None
    return dt, dy, dgn, dgp


def loss_head(xl, target, *, name):
    S, D = xl.shape
    tm = _tile(S, TM_NORM, 16)

    def body(x_ref, t_ref, dx_ref, l_ref):
        i = pl.program_id(0)
        e = x_ref[...] - t_ref[...]
        dx_ref[...] = e * (1.0 / D)
        part = jnp.sum(e * e)

        @pl.when(i == 0)
        def _():
            l_ref[...] = jnp.zeros_like(l_ref) + part

        @pl.when(i > 0)
        def _():
            l_ref[...] += part

    row = pl.BlockSpec((tm, D), lambda i: (i, 0))
    return pl.pallas_call(
        body, grid=(S // tm,), in_specs=[row, row],
        out_specs=[row, pl.BlockSpec((8, LANES), lambda i: (0, 0))],
        out_shape=[jax.ShapeDtypeStruct((S, D), F32), jax.ShapeDtypeStruct((8, LANES), F32)],
        compiler_params=_cp(1), name=name,
    )(xl, target)


def _tap_sum(src, base, offs, wrows, r0, rc, c0):
    acc = None
    for k, off in enumerate(offs):
        t = src[base + r0 + off: base + r0 + off + rc, c0:c0 + LANES]
        if wrows is not None:
            t = t * wrows[k]
        acc = t if acc is None else acc + t
    return acc


def _tap_wgrad(out_ref, o0, a, a_base, b, b_base, offs, n, c0, first):
    for k, off in enumerate(offs):
        acc = None
        for r0, rc in _chunks(n):
            t = (a[a_base + r0: a_base + r0 + rc, c0:c0 + LANES]
                 * b[b_base + r0 + off: b_base + r0 + off + rc, c0:c0 + LANES])
            t = jnp.sum(t, axis=0, keepdims=True)
            acc = t if acc is None else acc + t
        _acc_store(out_ref, (slice(k, k + 1), slice(o0 + c0, o0 + c0 + LANES)), acc, first)


def _shift_copies(sh, src, n, shifts, c0):
    for b in shifts:
        for r0, rc in _chunks(n):
            sh[b, r0:r0 + rc, :] = src[r0 + b:r0 + b + rc, c0:c0 + LANES]


def _tap(sh, src, o, r0, rc, c0):
    a, b = divmod(o, 8)
    if b == 0:
        return src[r0 + o:r0 + o + rc, c0:c0 + LANES]
    return sh[b, r0 + 8 * a:r0 + 8 * a + rc, :]


def _acc_store(ref, idx, val, first):
    @pl.when(first)
    def _():
        ref[idx] = val

    @pl.when(jnp.logical_not(first))
    def _():
        ref[idx] += val


def _row_counts(t0, rc, w):
    t = t0 + lax.broadcasted_iota(jnp.int32, (rc, LANES), 0)
    return jnp.minimum(t + 1, w).astype(F32)


def mixer_fwd(u, wa, wb, bb, lg, lb, pw, ps, *, name):
    S, DIN = u.shape
    WA, WB, WC = wa.shape[1], wb.shape[1], ps.shape[1]
    DMIX = WA + WB + WC
    tm = _tile(S, TM_MIXER, HALO_MIX)
    HB = HALO_MIX
    r = tm // HB
    oCg, oVa, oVal, oGate, oC = WA, 2 * WA, 3 * WA, 3 * WA + WB, 3 * WA + 2 * WB
    offs_a = [k - (K_A - 1) for k in range(K_A)]
    offs_b = [k - (K_B - 1) for k in range(K_B)]

    def body(u_ref, uh_ref, wa_ref, wb_ref, bb_ref, lg_ref, lb_ref, pw_ref, ps_ref, y_ref, cbuf,
             pbuf, gbuf, shbuf, xbuf, plbuf):
        i = pl.program_id(0)
        hv = jnp.where(i > 0, 1.0, 0.0).astype(F32)

        def fill(src, dst0, n, scale):
            for r0, rc in _chunks(n):
                rows, drows = slice(r0, r0 + rc), slice(dst0 + r0, dst0 + r0 + rc)
                for c0 in range(0, WA, LANES):
                    v = (src[rows, oCg + c0:oCg + c0 + LANES].astype(F32)
                         * src[rows, oVa + c0:oVa + c0 + LANES].astype(F32))
                    pbuf[drows, c0:c0 + LANES] = v if scale is None else v * scale
                for c0 in range(0, WB, LANES):
                    v = (src[rows, oVal + c0:oVal + c0 + LANES].astype(F32)
                         * _sigmoid(src[rows, oGate + c0:oGate + c0 + LANES].astype(F32)))
                    gbuf[drows, c0:c0 + LANES] = v if scale is None else v * scale
                for c0 in range(0, WC, LANES):
                    v = src[rows, oC + c0:oC + c0 + LANES].astype(F32)
                    xbuf[drows, c0:c0 + LANES] = v if scale is None else v * scale

        fill(uh_ref, 0, HB, hv)
        fill(u_ref, HB, tm, None)

        for c0 in range(0, WA, LANES):
            w = [wa_ref[k:k + 1, c0:c0 + LANES] for k in range(K_A)]
            for r0, rc in _chunks(tm):
                q = _tap_sum(pbuf, HB, offs_a, w, r0, rc, c0)
                bg = u_ref[r0:r0 + rc, c0:c0 + LANES].astype(F32)
                y_ref[r0:r0 + rc, c0:c0 + LANES] = (bg * q).astype(BF16)

        for c0 in range(0, WB, LANES):
            w = [wb_ref[k:k + 1, c0:c0 + LANES] for k in range(K_B)]
            bias = bb_ref[:, c0:c0 + LANES]
            _shift_copies(shbuf, gbuf, tm + 24, range(1, 8), c0)
            for r0, rc in _chunks(tm):
                acc = bias
                for k in range(K_B):
                    acc = acc + _tap(shbuf, gbuf, HB - (K_B - 1) + k, r0, rc, c0) * w[k]
                cbuf[r0:r0 + rc, c0:c0 + LANES] = acc
        for r0, rc in _chunks(tm, 32):
            cb = cbuf[r0:r0 + rc, :]
            mu = jnp.mean(cb, axis=-1, keepdims=True)
            d = cb - mu
            n = d * lax.rsqrt(jnp.mean(d * d, axis=-1, keepdims=True) + LN_EPS)
            z = n * lg_ref[...] + lb_ref[...]
            y_ref[r0:r0 + rc, WA:WA + WB] = (z * _sigmoid(z)).astype(BF16)

        for g, win in enumerate(POOL_WINDOWS):
            c0 = g * LANES
            for r0, rc in _chunks(tm):
                s = _tap_sum(xbuf, HB, [-j for j in range(win)], None, r0, rc, c0)
                pooled = s / _row_counts(i * tm + r0, rc, win) - xbuf[HB + r0:HB + r0 + rc, c0:c0 + LANES]
                plbuf[r0:r0 + rc, c0:c0 + LANES] = pooled.astype(BF16)
            mixed = jnp.dot(plbuf[:, c0:c0 + LANES], pw_ref[g], preferred_element_type=F32)
            y_ref[:, WA + WB + c0:WA + WB + c0 + LANES] = (mixed * ps_ref[:, c0:c0 + LANES]).astype(BF16)

    full = lambda a: pl.BlockSpec(a.shape, lambda i: (0,) * a.ndim)
    return pl.pallas_call(
        body, grid=(S // tm,),
        in_specs=[pl.BlockSpec((tm, DIN), lambda i: (i, 0)),
                  pl.BlockSpec((HB, DIN), lambda i: (jnp.maximum(i * r - 1, 0), 0)),
                  full(wa), full(wb), full(bb), full(lg), full(lb), full(pw), full(ps)],
        out_specs=[pl.BlockSpec((tm, DMIX), lambda i: (i, 0)), pl.BlockSpec((tm, WB), lambda i: (i, 0))],
        out_shape=[jax.ShapeDtypeStruct((S, DMIX), BF16), jax.ShapeDtypeStruct((S, WB), F32)],
        scratch_shapes=[pltpu.VMEM((HB + tm, WA), F32), pltpu.VMEM((HB + tm, WB), F32),
                        pltpu.VMEM((8, tm + 24, LANES), F32), pltpu.VMEM((HB + tm, WC), F32),
                        pltpu.VMEM((tm, WC), BF16)],
        compiler_params=_cp(1), name=name,
    )(u, u, wa, wb, bb, lg, lb, pw, ps)


def mixer_bwd(u, cb, dy, wa, wb, bb, lg, lb, pw, ps, *, name):
    S, DIN = u.shape
    WA, WB, WC = wa.shape[1], wb.shape[1], ps.shape[1]
    NG = WC // LANES
    DMIX = WA + WB + WC
    tm = _tile(S, TM_MIXER, HALO_MIX)
    HB = HALO_MIX
    r = tm // HB
    nI = S // tm
    nH = S // HB
    oCg, oVa, oVal, oGate, oC = WA, 2 * WA, 3 * WA, 3 * WA + WB, 3 * WA + 2 * WB
    offs_a = [k - (K_A - 1) for k in range(K_A)]
    offs_b = [k - (K_B - 1) for k in range(K_B)]
    adj_a = [(K_A - 1) - k for k in range(K_A)]
    adj_b = [(K_B - 1) - k for k in range(K_B)]

    def body(u_ref, ub_ref, ua_ref, cb_ref, cba_ref, dy_ref, dya_ref,
             wa_ref, wb_ref, bb_ref, lg_ref, lb_ref, pw_ref, ps_ref,
             du_ref, dwa_ref, dwb_ref, dbb_ref, dlg_ref, dlb_ref, dpw_ref, dps_ref,
             pbuf, dqbuf, gbuf, dcbuf, shbuf, xbuf, plbuf, dmbuf, dplbuf, dpcbuf):
        i = pl.program_id(0)
        first = i == 0
        hvb = jnp.where(i > 0, 1.0, 0.0).astype(F32)
        hva = jnp.where(i < nI - 1, 1.0, 0.0).astype(F32)

        def fill(src, dst0, n, scale, main):
            for r0, rc in _chunks(n):
                rows, drows = slice(r0, r0 + rc), slice(dst0 + r0, dst0 + r0 + rc)
                for c0 in range(0, WA, LANES):
                    v = (src[rows, oCg + c0:oCg + c0 + LANES].astype(F32)
                         * src[rows, oVa + c0:oVa + c0 + LANES].astype(F32))
                    pbuf[drows, c0:c0 + LANES] = v if scale is None else v * scale
                for c0 in range(0, WC, LANES):
                    v = src[rows, oC + c0:oC + c0 + LANES].astype(F32)
                    xbuf[drows, c0:c0 + LANES] = v if scale is None else v * scale
                if main:
                    for c0 in range(0, WB, LANES):
                        gbuf[rows, c0:c0 + LANES] = (
                            src[rows, oVal + c0:oVal + c0 + LANES].astype(F32)
                            * _sigmoid(src[rows, oGate + c0:oGate + c0 + LANES].astype(F32)))

        fill(ub_ref, 0, HB, hvb, False)
        fill(u_ref, HB, tm, None, True)

        for r0, rc in _chunks(tm):
            for c0 in range(0, WA, LANES):
                dqbuf[r0:r0 + rc, c0:c0 + LANES] = (dy_ref[r0:r0 + rc, c0:c0 + LANES].astype(F32)
                                                     * u_ref[r0:r0 + rc, c0:c0 + LANES].astype(F32))
        for c0 in range(0, WA, LANES):
            dqbuf[tm:tm + HB, c0:c0 + LANES] = (dya_ref[:, c0:c0 + LANES].astype(F32)
                                                * ua_ref[:, c0:c0 + LANES].astype(F32)) * hva
        for c0 in range(0, WA, LANES):
            w = [wa_ref[k:k + 1, c0:c0 + LANES] for k in range(K_A)]
            for r0, rc in _chunks(tm):
                rows = slice(r0, r0 + rc)
                q = _tap_sum(pbuf, HB, offs_a, w, r0, rc, c0)
                du_ref[rows, c0:c0 + LANES] = (dy_ref[rows, c0:c0 + LANES].astype(F32) * q).astype(BF16)
                dp = _tap_sum(dqbuf, 0, adj_a, w, r0, rc, c0)
                cg = u_ref[rows, oCg + c0:oCg + c0 + LANES].astype(F32)
                va = u_ref[rows, oVa + c0:oVa + c0 + LANES].astype(F32)
                du_ref[rows, oCg + c0:oCg + c0 + LANES] = (dp * va).astype(BF16)
                du_ref[rows, oVa + c0:oVa + c0 + LANES] = (dp * cg).astype(BF16)
            _tap_wgrad(dwa_ref, 0, dqbuf, 0, pbuf, HB, offs_a, tm, c0, first)

        def ln_chunk(r0, rc, cb, dyb, scale, main):
            mu = jnp.mean(cb, axis=-1, keepdims=True)
            d = cb - mu
            rs = lax.rsqrt(jnp.mean(d * d, axis=-1, keepdims=True) + LN_EPS)
            n = d * rs
            z = n * lg_ref[...] + lb_ref[...]
            sg = _sigmoid(z)
            dz = dyb * (sg * (1.0 + z * (1.0 - sg)))
            dn = dz * lg_ref[...]
            dcb = rs * (dn - jnp.mean(dn, axis=-1, keepdims=True) - n * jnp.mean(dn * n, axis=-1, keepdims=True))
            if scale is not None:
                dcb = dcb * scale
            dcbuf[r0:r0 + rc, :] = dcb
            if main:
                return (jnp.sum(dz * n, axis=0, keepdims=True), jnp.sum(dz, axis=0, keepdims=True),
                        jnp.sum(dcb, axis=0, keepdims=True))
            return None

        sums = None
        for r0, rc in _chunks(tm, 32):
            part = ln_chunk(r0, rc, cb_ref[r0:r0 + rc, :], dy_ref[r0:r0 + rc, WA:WA + WB].astype(F32), None, True)
            sums = part if sums is None else tuple(a + b for a, b in zip(sums, part))
        ln_chunk(tm, HB, cba_ref[...], dya_ref[:, WA:WA + WB].astype(F32), hva, False)
        _acc_store(dlg_ref, (slice(None), slice(None)), sums[0], first)
        _acc_store(dlb_ref, (slice(None), slice(None)), sums[1], first)
        _acc_store(dbb_ref, (slice(None), slice(None)), sums[2], first)

        for c0 in range(0, WB, LANES):
            w = [wb_ref[k:k + 1, c0:c0 + LANES] for k in range(K_B)]
            _shift_copies(shbuf, dcbuf, tm + 24, range(1, 8), c0)
            for r0, rc in _chunks(tm):
                rows = slice(r0, r0 + rc)
                dglu = None
                for k in range(K_B):
                    t = _tap(shbuf, dcbuf, adj_b[k], r0, rc, c0) * w[k]
                    dglu = t if dglu is None else dglu + t
                val = u_ref[rows, oVal + c0:oVal + c0 + LANES].astype(F32)
                sg = _sigmoid(u_ref[rows, oGate + c0:oGate + c0 + LANES].astype(F32))
                du_ref[rows, oVal + c0:oVal + c0 + LANES] = (dglu * sg).astype(BF16)
                du_ref[rows, oGate + c0:oGate + c0 + LANES] = (dglu * val * sg * (1.0 - sg)).astype(BF16)
            for k in range(K_B):
                acc = None
                for r0, rc in _chunks(tm):
                    t = _tap(shbuf, dcbuf, adj_b[k], r0, rc, c0) * gbuf[r0:r0 + rc, c0:c0 + LANES]
                    acc = t if acc is None else acc + t
                _acc_store(dwb_ref, (slice(k, k + 1), slice(c0, c0 + LANES)),
                           jnp.sum(acc, axis=0, keepdims=True), first)

        for g, win in enumerate(POOL_WINDOWS):
            c0 = g * LANES
            cols = slice(c0, c0 + LANES)
            ycols = slice(WA + WB + c0, WA + WB + c0 + LANES)
            for r0, rc in _chunks(tm):
                s = _tap_sum(xbuf, HB, [-j for j in range(win)], None, r0, rc, c0)
                pooled = s / _row_counts(i * tm + r0, rc, win) - xbuf[HB + r0:HB + r0 + rc, cols]
                plbuf[r0:r0 + rc, cols] = pooled.astype(BF16)
            mixed = jnp.dot(plbuf[:, cols], pw_ref[g], preferred_element_type=F32)
            dyc = dy_ref[:, ycols].astype(F32)
            _acc_store(dps_ref, (slice(None), cols), jnp.sum(dyc * mixed, axis=0, keepdims=True), first)
            dmbuf[0:tm, :] = (dyc * ps_ref[:, cols]).astype(BF16)
            dmbuf[tm:tm + HB, :] = (dya_ref[:, ycols].astype(F32) * ps_ref[:, cols] * hva).astype(BF16)
            dpw = lax.dot_general(plbuf[:, cols], dmbuf[0:tm, :], (((0,), (0,)), ((), ())),
                                  preferred_element_type=F32)
            _acc_store(dpw_ref, (g, slice(None), slice(None)), dpw, first)
            dplbuf[...] = lax.dot_general(dmbuf[...], pw_ref[g], (((1,), (1,)), ((), ())),
                                          preferred_element_type=F32)
            for r0, rc in _chunks(tm + HB):
                dpcbuf[r0:r0 + rc, :] = dplbuf[r0:r0 + rc, :] / _row_counts(i * tm + r0, rc, win)
            for r0, rc in _chunks(tm):
                duc = _tap_sum(dpcbuf, 0, list(range(win)), None, r0, rc, 0) - dplbuf[r0:r0 + rc, :]
                du_ref[r0:r0 + rc, oC + c0:oC + c0 + LANES] = duc.astype(BF16)

    full = lambda a: pl.BlockSpec(a.shape, lambda i: (0,) * a.ndim)
    acc = lambda shape: pl.BlockSpec(shape, lambda i: (0,) * len(shape))
    small = [(K_A, WA), (K_B, WB), (1, WB), (1, WB), (1, WB), (NG, LANES, LANES), (1, WC)]
    outs = pl.pallas_call(
        body, grid=(nI,),
        in_specs=[pl.BlockSpec((tm, DIN), lambda i: (i, 0)),
                  pl.BlockSpec((HB, DIN), lambda i: (jnp.maximum(i * r - 1, 0), 0)),
                  pl.BlockSpec((HB, DIN), lambda i: (jnp.minimum((i + 1) * r, nH - 1), 0)),
                  pl.BlockSpec((tm, WB), lambda i: (i, 0)),
                  pl.BlockSpec((HB, WB), lambda i: (jnp.minimum((i + 1) * r, nH - 1), 0)),
                  pl.BlockSpec((tm, DMIX), lambda i: (i, 0)),
                  pl.BlockSpec((HB, DMIX), lambda i: (jnp.minimum((i + 1) * r, nH - 1), 0)),
                  full(wa), full(wb), full(bb), full(lg), full(lb), full(pw), full(ps)],
        out_specs=[pl.BlockSpec((tm, DIN), lambda i: (i, 0))] + [acc(s) for s in small],
        out_shape=[jax.ShapeDtypeStruct((S, DIN), BF16)] + [jax.ShapeDtypeStruct(s, F32) for s in small],
        scratch_shapes=[pltpu.VMEM((HB + tm, WA), F32), pltpu.VMEM((tm + HB, WA), F32),
                        pltpu.VMEM((tm, WB), F32), pltpu.VMEM((tm + HB, WB), F32),
                        pltpu.VMEM((8, tm + 24, LANES), F32), pltpu.VMEM((HB + tm, WC), F32),
                        pltpu.VMEM((tm, WC), BF16), pltpu.VMEM((tm + HB, LANES), BF16),
                        pltpu.VMEM((tm + HB, LANES), F32), pltpu.VMEM((tm + HB, LANES), F32)],
        compiler_params=_cp(1), name=name,
    )(u, u, u, cb, cb, dy, dy, wa, wb, bb, lg, lb, pw, ps)
    return outs


def ffn_fwd(up, wf, bf, *, name):
    S, F2 = up.shape
    F = F2 // 2
    tm = _tile(S, TM_FFN, HALO_FFN)
    HB = HALO_FFN
    r = tm // HB
    CW = _tile(F, 512)
    offs = [k - (K_F - 1) for k in range(K_F)]

    def body(up_ref, uph_ref, wf_ref, bf_ref, a_ref, upc_ref, ebuf):
        i = pl.program_id(0)
        hv = jnp.where(i > 0, 1.0, 0.0).astype(F32)
        for c0 in range(0, F, CW):
            for h, off in ((0, c0), (1, F + c0)):
                ebuf[h, 0:HB, :] = uph_ref[:, off:off + CW].astype(F32) * hv
                for r0, rc in _chunks(tm):
                    ebuf[h, HB + r0:HB + r0 + rc, :] = up_ref[r0:r0 + rc, off:off + CW].astype(F32)
            for l0 in range(0, CW, LANES):
                cg, cv = c0 + l0, F + c0 + l0
                wg = [wf_ref[k:k + 1, cg:cg + LANES] for k in range(K_F)]
                wv = [wf_ref[k:k + 1, cv:cv + LANES] for k in range(K_F)]
                bg = bf_ref[:, cg:cg + LANES]
                bv = bf_ref[:, cv:cv + LANES]
                for r0, rc in _chunks(tm):
                    gt = _tap_sum(ebuf.at[0], HB, offs, wg, r0, rc, l0) + bg
                    vl = _tap_sum(ebuf.at[1], HB, offs, wv, r0, rc, l0) + bv
                    a_ref[r0:r0 + rc, cg:cg + LANES] = (gt * _sigmoid(gt) * vl).astype(BF16)
                    upc_ref[r0:r0 + rc, cg:cg + LANES] = gt.astype(BF16)
                    upc_ref[r0:r0 + rc, cv:cv + LANES] = vl.astype(BF16)

    full = lambda a: pl.BlockSpec(a.shape, lambda i: (0,) * a.ndim)
    return pl.pallas_call(
        body, grid=(S // tm,),
        in_specs=[pl.BlockSpec((tm, F2), lambda i: (i, 0)),
                  pl.BlockSpec((HB, F2), lambda i: (jnp.maximum(i * r - 1, 0), 0)),
                  full(wf), full(bf)],
        out_specs=[pl.BlockSpec((tm, F), lambda i: (i, 0)), pl.BlockSpec((tm, F2), lambda i: (i, 0))],
        out_shape=[jax.ShapeDtypeStruct((S, F), BF16), jax.ShapeDtypeStruct((S, F2), BF16)],
        scratch_shapes=[pltpu.VMEM((2, HB + tm, CW), F32)],
        compiler_params=_cp(1), name=name,
    )(up, up, wf, bf)


def ffn_bwd(up, upc, da, wf, *, name):
    S, F2 = up.shape
    F = F2 // 2
    tm = _tile(S, TM_FFN_BWD, HALO_FFN)
    HB = HALO_FFN
    r = tm // HB
    nI = S // tm
    nH = S // HB
    CW = _tile(F, 512)
    adj = [(K_F - 1) - k for k in range(K_F)]

    def body(up_ref, upc_ref, upca_ref, da_ref, daa_ref, wf_ref, dup_ref, dwf_ref, dbf_ref, dbuf, shbuf):
        i = pl.program_id(0)
        first = i == 0
        hva = jnp.where(i < nI - 1, 1.0, 0.0).astype(F32)
        for c0 in range(0, F, CW):
            for l0 in range(0, CW, LANES):
                cg, cv = c0 + l0, F + c0 + l0
                wg = [wf_ref[k:k + 1, cg:cg + LANES] for k in range(K_F)]
                wv = [wf_ref[k:k + 1, cv:cv + LANES] for k in range(K_F)]
                sg_sum, sv_sum = None, None
                for r0, rc in _chunks(tm + HB):
                    if r0 < tm:
                        gt = upc_ref[r0:r0 + rc, cg:cg + LANES].astype(F32)
                        vl = upc_ref[r0:r0 + rc, cv:cv + LANES].astype(F32)
                        d = da_ref[r0:r0 + rc, cg:cg + LANES].astype(F32)
                    else:
                        gt = upca_ref[:, cg:cg + LANES].astype(F32)
                        vl = upca_ref[:, cv:cv + LANES].astype(F32)
                        d = daa_ref[:, cg:cg + LANES].astype(F32) * hva
                    s = _sigmoid(gt)
                    dg = d * vl * (s * (1.0 + gt * (1.0 - s)))
                    dv = d * (gt * s)
                    dbuf[0, r0:r0 + rc, :] = dg
                    dbuf[1, r0:r0 + rc, :] = dv
                    if r0 < tm:
                        pg, pv = jnp.sum(dg, axis=0, keepdims=True), jnp.sum(dv, axis=0, keepdims=True)
                        sg_sum = pg if sg_sum is None else sg_sum + pg
                        sv_sum = pv if sv_sum is None else sv_sum + pv
                _acc_store(dbf_ref, (slice(None), slice(cg, cg + LANES)), sg_sum, first)
                _acc_store(dbf_ref, (slice(None), slice(cv, cv + LANES)), sv_sum, first)
                for h, w, col in ((0, wg, cg), (1, wv, cv)):
                    d_h, sh_h = dbuf.at[h], shbuf.at[h]
                    _shift_copies(sh_h, d_h, tm, (1, 2), 0)
                    accs = [None] * K_F
                    for r0, rc in _chunks(tm):
                        taps = [_tap(sh_h, d_h, adj[k], r0, rc, 0) for k in range(K_F)]
                        dup_ref[r0:r0 + rc, col:col + LANES] = (
                            taps[0] * w[0] + taps[1] * w[1] + taps[2] * w[2]).astype(BF16)
                        uv = up_ref[r0:r0 + rc, col:col + LANES].astype(F32)
                        for k in range(K_F):
                            t = taps[k] * uv
                            accs[k] = t if accs[k] is None else accs[k] + t
                    for k in range(K_F):
                        _acc_store(dwf_ref, (slice(k, k + 1), slice(col, col + LANES)),
                                   jnp.sum(accs[k], axis=0, keepdims=True), first)

    full = lambda a: pl.BlockSpec(a.shape, lambda i: (0,) * a.ndim)
    return pl.pallas_call(
        body, grid=(nI,),
        in_specs=[pl.BlockSpec((tm, F2), lambda i: (i, 0)),
                  pl.BlockSpec((tm, F2), lambda i: (i, 0)),
                  pl.BlockSpec((HB, F2), lambda i: (jnp.minimum((i + 1) * r, nH - 1), 0)),
                  pl.BlockSpec((tm, F), lambda i: (i, 0)),
                  pl.BlockSpec((HB, F), lambda i: (jnp.minimum((i + 1) * r, nH - 1), 0)),
                  full(wf)],
        out_specs=[pl.BlockSpec((tm, F2), lambda i: (i, 0)),
                   pl.BlockSpec((K_F, F2), lambda i: (0, 0)), pl.BlockSpec((1, F2), lambda i: (0, 0))],
        out_shape=[jax.ShapeDtypeStruct((S, F2), BF16), jax.ShapeDtypeStruct((K_F, F2), F32),
                   jax.ShapeDtypeStruct((1, F2), F32)],
        scratch_shapes=[pltpu.VMEM((2, tm + HB, LANES), F32), pltpu.VMEM((2, 3, tm, LANES), F32)],
        compiler_params=_cp(1), name=name,
    )(up, upc, upc, da, da, wf)


HBM = pl.BlockSpec(memory_space=pltpu.HBM)


def _place():
    return lax.axis_index("x"), lax.axis_index("y"), lax.axis_index("c")


def allgather8(buf, *, name):
    R, C = buf.shape

    def body(x_ref, o_ref, send_sems, recv_sems, local_sem):
        x, y, c = _place()
        me = 4 * x + 2 * y + c
        mine = pltpu.make_async_copy(x_ref, o_ref.at[me], local_sem)
        mine.start()
        sends = []
        for k in range(1, 8):
            fx, fy, fc = (k >> 2) & 1, (k >> 1) & 1, k & 1
            px, py, pc = (x + fx) % 2, (y + fy) % 2, (c + fc) % 2
            cp = pltpu.make_async_remote_copy(
                src_ref=x_ref, dst_ref=o_ref.at[me], send_sem=send_sems.at[k - 1], recv_sem=recv_sems.at[k - 1],
                device_id=(px, py, pc), device_id_type=MESH)
            cp.start()
            sends.append(cp)
        for k in range(1, 8):
            fx, fy, fc = (k >> 2) & 1, (k >> 1) & 1, k & 1
            peer = 4 * ((x + fx) % 2) + 2 * ((y + fy) % 2) + (c + fc) % 2
            pltpu.make_async_remote_copy(
                src_ref=x_ref, dst_ref=o_ref.at[peer], send_sem=send_sems.at[k - 1], recv_sem=recv_sems.at[k - 1],
                device_id=(x, y, c), device_id_type=MESH).wait_recv()
        for cp in sends:
            cp.wait_send()
        mine.wait()

    return pl.pallas_call(
        body, in_specs=[HBM], out_specs=HBM, out_shape=jax.ShapeDtypeStruct((8, R, C), buf.dtype),
        scratch_shapes=[pltpu.SemaphoreType.DMA((7,)), pltpu.SemaphoreType.DMA((7,)), pltpu.SemaphoreType.DMA],
        name=name,
    )(buf)


def gather_chip_shards(shards, *, name):
    n = len(shards)

    def body(*refs):
        ins, outs = refs[:n], refs[n:2 * n]
        send_sems, recv_sems = refs[2 * n:]
        x, y, c = _place()
        b = 2 * x + y
        chips = [(1 - x, y), (x, 1 - y), (1 - x, 1 - y)]
        sends = []

        def half(a, which):
            rh = ins[a].shape[0] // 2
            return pl.ds(pl.multiple_of(which * rh, 16), rh)

        def copy(a, k, src, dst, to):
            return pltpu.make_async_remote_copy(
                src_ref=src, dst_ref=dst, send_sem=send_sems.at[7 * a + k], recv_sem=recv_sems.at[7 * a + k],
                device_id=to, device_id_type=MESH)

        for a in range(n):
            for j, (cx, cy) in enumerate(chips):
                cp = copy(a, j, ins[a].at[half(a, c)], outs[a].at[b, half(a, c)], (cx, cy, c))
                cp.start()
                sends.append(cp)
        for a in range(n):
            cp = copy(a, 6, ins[a], outs[a].at[b], (x, y, 1 - c))
            cp.start()
            sends.append(cp)
        for a in range(n):
            for j, (cx, cy) in enumerate(chips):
                got = outs[a].at[2 * cx + cy, half(a, c)]
                copy(a, j, got, got, (x, y, c)).wait_recv()
                cp = copy(a, 3 + j, got, got, (x, y, 1 - c))
                cp.start()
                sends.append(cp)
        for a in range(n):
            for j, (cx, cy) in enumerate(chips):
                got = outs[a].at[2 * cx + cy, half(a, 1 - c)]
                copy(a, 3 + j, got, got, (x, y, c)).wait_recv()
            copy(a, 6, ins[a], outs[a].at[b], (x, y, c)).wait_recv()
        for cp in sends:
            cp.wait_send()

    return pl.pallas_call(
        body, in_specs=[HBM] * n, out_specs=[HBM] * n,
        out_shape=[jax.ShapeDtypeStruct((4,) + s.shape, s.dtype) for s in shards],
        scratch_shapes=[pltpu.SemaphoreType.DMA((7 * n,)), pltpu.SemaphoreType.DMA((7 * n,))],
        name=name,
    )(*shards)


def sibling_swap_halves(gs, *, name):
    n = len(gs)

    def body(*refs):
        ins, outs = refs[:n], refs[n:2 * n]
        send_sems, recv_sems = refs[2 * n:]
        x, y, c = _place()
        cps = []
        for a in range(n):
            rh = ins[a].shape[1] // 2
            src = ins[a].at[:, pl.ds(pl.multiple_of((1 - c) * rh, 16), rh)]
            cp = pltpu.make_async_remote_copy(
                src_ref=src, dst_ref=outs[a], send_sem=send_sems.at[a], recv_sem=recv_sems.at[a],
                device_id=(x, y, 1 - c), device_id_type=MESH)
            cp.start()
            cps.append(cp)
        for cp in cps:
            cp.wait()

    return pl.pallas_call(
        body, in_specs=[HBM] * n, out_specs=[HBM] * n,
        out_shape=[jax.ShapeDtypeStruct((4, g.shape[1] // 2, g.shape[2]), g.dtype) for g in gs],
        scratch_shapes=[pltpu.SemaphoreType.DMA((n,)), pltpu.SemaphoreType.DMA((n,))],
        name=name,
    )(*gs)


def chip_exchange(ps, *, name):
    n = len(ps)

    def body(*refs):
        ins, outs = refs[:n], refs[n:2 * n]
        send_sems, recv_sems, local_sems = refs[2 * n:]
        x, y, c = _place()
        b = 2 * x + y
        chips = [(1 - x, y), (x, 1 - y), (1 - x, 1 - y)]
        locals_, sends = [], []
        for a in range(n):
            cp = pltpu.make_async_copy(ins[a].at[b], outs[a].at[b], local_sems.at[a])
            cp.start()
            locals_.append(cp)
        for a in range(n):
            for j, (cx, cy) in enumerate(chips):
                cp = pltpu.make_async_remote_copy(
                    src_ref=ins[a].at[2 * cx + cy], dst_ref=outs[a].at[b],
                    send_sem=send_sems.at[3 * a + j], recv_sem=recv_sems.at[3 * a + j],
                    device_id=(cx, cy, c), device_id_type=MESH)
                cp.start()
                sends.append(cp)
        for a in range(n):
            for j, (cx, cy) in enumerate(chips):
                got = outs[a].at[2 * cx + cy]
                pltpu.make_async_remote_copy(
                    src_ref=got, dst_ref=got, send_sem=send_sems.at[3 * a + j], recv_sem=recv_sems.at[3 * a + j],
                    device_id=(x, y, c), device_id_type=MESH).wait_recv()
        for cp in sends:
            cp.wait_send()
        for cp in locals_:
            cp.wait()

    return pl.pallas_call(
        body, in_specs=[HBM] * n, out_specs=[HBM] * n,
        out_shape=[jax.ShapeDtypeStruct(p.shape, p.dtype) for p in ps],
        scratch_shapes=[pltpu.SemaphoreType.DMA((3 * n,)), pltpu.SemaphoreType.DMA((3 * n,)),
                        pltpu.SemaphoreType.DMA((n,))],
        name=name,
    )(*ps)


def sibling_join_halves(fs, *, name):
    n = len(fs)

    def body(*refs):
        ins, outs = refs[:n], refs[n:2 * n]
        send_sems, recv_sems = refs[2 * n:]
        x, y, c = _place()
        sends = []
        for a in range(n):
            rh = ins[a].shape[0] // 2
            mine = pl.ds(pl.multiple_of(c * rh, 8), rh)
            cp = pltpu.make_async_remote_copy(
                src_ref=ins[a].at[mine], dst_ref=outs[a].at[mine], send_sem=send_sems.at[a],
                recv_sem=recv_sems.at[a], device_id=(x, y, 1 - c), device_id_type=MESH)
            cp.start()
            sends.append(cp)
        for a in range(n):
            rh = ins[a].shape[0] // 2
            other = pl.ds(pl.multiple_of((1 - c) * rh, 8), rh)
            pltpu.make_async_remote_copy(
                src_ref=ins[a].at[other], dst_ref=outs[a].at[other], send_sem=send_sems.at[a],
                recv_sem=recv_sems.at[a], device_id=(x, y, c), device_id_type=MESH).wait_recv()
        for cp in sends:
            cp.wait_send()

    return pl.pallas_call(
        body, in_specs=[HBM] * n, out_specs=[HBM] * n,
        out_shape=[jax.ShapeDtypeStruct(f.shape, f.dtype) for f in fs],
        input_output_aliases={a: a for a in range(n)},
        scratch_shapes=[pltpu.SemaphoreType.DMA((n,)), pltpu.SemaphoreType.DMA((n,))],
        name=name,
    )(*fs)


def pair_add(g, t, core, *, name):
    _, R, C = g.shape
    rh = R // 2
    tr = _tile(rh, 256, 16)
    nh = rh // tr

    def body(c_ref, g_ref, t_ref, o_ref):
        o_ref[...] = (g_ref[...].astype(F32) + t_ref[...].astype(F32)).astype(BF16)

    return pl.pallas_call(
        body,
        grid_spec=pltpu.PrefetchScalarGridSpec(
            num_scalar_prefetch=1, grid=(4, nh),
            in_specs=[pl.BlockSpec((None, tr, C), lambda b, i, c_ref: (b, c_ref[0] * nh + i, 0)),
                      pl.BlockSpec((None, tr, C), lambda b, i, c_ref: (b, i, 0))],
            out_specs=pl.BlockSpec((None, tr, C), lambda b, i, c_ref: (b, i, 0))),
        out_shape=jax.ShapeDtypeStruct((4, rh, C), BF16), compiler_params=_cp(2), name=name,
    )(core, g, t)


def sum_slots(q, *, name):
    N, R, C = q.shape
    tr = _tile(R, 256, 16)

    def body(q_ref, o_ref):
        acc = q_ref[0].astype(F32)
        for s in range(1, N):
            acc = acc + q_ref[s].astype(F32)
        o_ref[...] = acc

    return pl.pallas_call(
        body, grid=(R // tr,), in_specs=[pl.BlockSpec((N, tr, C), lambda i: (0, i, 0))],
        out_specs=pl.BlockSpec((tr, C), lambda i: (i, 0)),
        out_shape=jax.ShapeDtypeStruct((R, C), F32), compiler_params=_cp(1), name=name,
    )(q)


def sum_slots_half(q, core, *, name):
    N, rh, C = q.shape
    tr = _tile(rh, 256, 16)
    nh = rh // tr

    def body(c_ref, q_ref, o_ref):
        acc = q_ref[0].astype(F32)
        for s in range(1, N):
            acc = acc + q_ref[s].astype(F32)
        o_ref[...] = acc

    return pl.pallas_call(
        body,
        grid_spec=pltpu.PrefetchScalarGridSpec(
            num_scalar_prefetch=1, grid=(nh,),
            in_specs=[pl.BlockSpec((N, tr, C), lambda i, c_ref: (0, i, 0))],
            out_specs=pl.BlockSpec((tr, C), lambda i, c_ref: (c_ref[0] * nh + i, 0))),
        out_shape=jax.ShapeDtypeStruct((2 * rh, C), F32), compiler_params=_cp(1), name=name,
    )(core, q)


def _adam_math(w, g, m, v):
    m = ADAM_B1 * m + (1.0 - ADAM_B1) * g
    v = ADAM_B2 * v + (1.0 - ADAM_B2) * (g * g)
    m_hat = m / (1.0 - ADAM_B1 ** ADAM_STEP)
    v_hat = v / (1.0 - ADAM_B2 ** ADAM_STEP)
    delta = -ADAM_LR * (m_hat / (jnp.sqrt(v_hat) + ADAM_EPS) + ADAM_WD * w)
    return delta, m, v


def adam_stacked(w, m, v, grads, *, name):
    L, R, C = w.shape
    tr = _tile(R, max(8, ADAM_BLOCK_BYTES // (4 * C)), 8)
    nr = R // tr

    def body(w_ref, m_ref, v_ref, *rest):
        g_refs, (go_ref, d_ref, mo_ref, vo_ref) = rest[:L], rest[L:]
        lid = pl.program_id(0)
        for l in range(L):
            @pl.when(lid == l)
            def _(l=l):
                g = g_refs[l][...]
                d, mn, vn = _adam_math(w_ref[...], g, m_ref[...], v_ref[...])
                go_ref[...] = g
                d_ref[...] = d
                mo_ref[...] = mn
                vo_ref[...] = vn

    st = pl.BlockSpec((None, tr, C), lambda l, i: (l, i, 0))
    g_specs = [pl.BlockSpec((tr, C), functools.partial(lambda l, i, ll: (jnp.where(l == ll, i, 0), 0), ll=ll))
               for ll in range(L)]
    return pl.pallas_call(
        body, grid=(L, nr), in_specs=[st, st, st] + g_specs, out_specs=[st] * 4,
        out_shape=[jax.ShapeDtypeStruct((L, R, C), F32)] * 4, compiler_params=_cp(2), name=name,
    )(w, m, v, *grads)


def adam_flat(w, g, m, v, *, name):
    R, C = w.shape
    tr = _tile(R, 512, 8)

    def body(w_ref, g_ref, m_ref, v_ref, d_ref, mo_ref, vo_ref):
        d, mn, vn = _adam_math(w_ref[...], g_ref[...], m_ref[...], v_ref[...])
        d_ref[...] = d
        mo_ref[...] = mn
        vo_ref[...] = vn

    row = pl.BlockSpec((tr, C), lambda i: (i, 0))
    return pl.pallas_call(
        body, grid=(R // tr,), in_specs=[row] * 4, out_specs=[row] * 3,
        out_shape=[jax.ShapeDtypeStruct((R, C), F32)] * 3, compiler_params=_cp(1), name=name,
    )(w, g, m, v)


PACK_ROWS = 64


def _pack(arrays):
    flat = jnp.concatenate([a.reshape(-1).astype(F32) for a in arrays])
    n = flat.shape[0]
    unit = PACK_ROWS * LANES
    pad = (-n) % unit
    return jnp.pad(flat, (0, pad)).reshape(-1, LANES)


def _unpack(buf, shapes):
    flat = buf.reshape(-1)
    out, o = [], 0
    for s in shapes:
        n = 1
        for d in s:
            n *= d
        out.append(flat[o:o + n].reshape(s))
        o += n
    return out


def kernel(x, norm_mix_pre, norm_mix_post, norm_ffn_pre, norm_ffn_post, w_in, conv_a_w, conv_b_w, conv_b_bias, ln_b_gain, ln_b_bias, pool_w, pool_scale, w_out, w_up, conv_ffn_w, conv_ffn_bias, w_down, loss_target, m_norm_mix_pre, m_norm_mix_post, m_norm_ffn_pre, m_norm_ffn_post, m_w_in, m_conv_a_w, m_conv_b_w, m_conv_b_bias, m_ln_b_gain, m_ln_b_bias, m_pool_w, m_pool_scale, m_w_out, m_w_up, m_conv_ffn_w, m_conv_ffn_bias, m_w_down, v_norm_mix_pre, v_norm_mix_post, v_norm_ffn_pre, v_norm_ffn_post, v_w_in, v_conv_a_w, v_conv_b_w, v_conv_b_bias, v_ln_b_gain, v_ln_b_bias, v_pool_w, v_pool_scale, v_w_out, v_w_up, v_conv_ffn_w, v_conv_ffn_bias, v_w_down):
    L = w_in.shape[0]
    S, D = x.shape[1], x.shape[2]
    WA, WB, WC = 4 * conv_a_w.shape[2], 4 * conv_b_w.shape[2], pool_scale.shape[1]
    DIN, DMIX, F2 = 4 * w_in.shape[2], 4 * w_out.shape[1], 4 * w_up.shape[2]
    F = F2 // 2
    NG = WC // LANES
    xi, yi, ci = _place()
    chip = 2 * xi + yi
    core = jnp.reshape(ci, (1,)).astype(jnp.int32)

    conv_shapes = [(L, K_A, WA // 4), (L, K_B, WB // 4), (L, K_F, F2 // 4)]
    conv_all = allgather8(_pack([conv_a_w, conv_b_w, conv_ffn_w]), name="gather_conv_taps")
    per_chip = [_unpack(conv_all[2 * b], conv_shapes) for b in range(4)]
    wa_full, wb_full, wf_full = [jnp.concatenate([per_chip[b][k] for b in range(4)], axis=2) for k in range(3)]
    pw_bf = pool_w.astype(BF16)

    w_full = []
    for l in range(L):
        g_in, g_up, g_out, g_down = gather_chip_shards(
            [w_in[l].astype(BF16), w_up[l].astype(BF16), w_out[l].astype(BF16), w_down[l].astype(BF16)],
            name="gather_layer_weights")
        w_full.append((
            jnp.concatenate([g_in[b] for b in range(4)], axis=1),
            jnp.concatenate([g_up[b] for b in range(4)], axis=1),
            g_out.reshape(DMIX, D),
            g_down.reshape(F, D),
        ))

    def vec(a, l):
        return a[l].reshape(1, -1)

    x0 = x.reshape(S, D)
    h1 = norm_fwd(x0, vec(norm_mix_pre, 0), name="norm_first")
    saved = []
    for l in range(L):
        Win, Wup, Wout, Wdown = w_full[l]
        u = matmul(h1, Win, out_dtype=BF16, tm=512, tn=2176, tk=2048, j_outer=True, name="mm_in")
        ymix, cb = mixer_fwd(u, wa_full[l], wb_full[l], vec(conv_b_bias, l), vec(ln_b_gain, l), vec(ln_b_bias, l),
                             pw_bf[l], vec(pool_scale, l), name="mixer_fwd")
        y = matmul(ymix, Wout, out_dtype=F32, tm=512, tn=2048, tk=2048, name="mm_out")
        x1, h2 = resid_norm_fwd(x0, y, vec(norm_mix_post, l), vec(norm_ffn_pre, l), emit_h=True, name="resid_norm_mid")
        up = matmul(h2, Wup, out_dtype=BF16, tm=512, tn=2816, tk=2048, j_outer=True, name="mm_up")
        a, upc = ffn_fwd(up, wf_full[l], vec(conv_ffn_bias, l), name="ffn_fwd")
        f = matmul(a, Wdown, out_dtype=F32, tm=1024, tn=2048, tk=1408, name="mm_down")
        last = l == L - 1
        x2, h_next = resid_norm_fwd(x1, f, vec(norm_ffn_post, l), vec(norm_mix_pre, 0 if last else l + 1),
                                    emit_h=not last, name="resid_norm_last" if last else "resid_norm_end")
        saved.append((x0, h1, u, ymix, y, x1, h2, up, a, f, cb, upc))
        x0, h1 = x2, h_next

    dx, lsum = loss_head(x0, loss_target.reshape(S, D), name="loss_head")
    loss = lax.psum(lsum[0, 0] * (0.5 / D), ("x", "y", "c"))

    small = [None] * L
    big = [None] * L
    dt = dx
    _, df, _, dg4 = norm_bwd(dt, None, None, None, saved[L - 1][9], vec(norm_ffn_post, L - 1), name="norm_bwd_top")
    for l in reversed(range(L)):
        Win, Wup, Wout, Wdown = w_full[l]
        x0, h1, u, ymix, y, x1, h2, up, a, f, cb, upc = saved[l]
        da = matmul(df, Wdown, tb=True, out_dtype=BF16, tm=512, tn=1408, tk=2048, j_outer=True, name="mm_down_dx")
        g_down = matmul(a, df, ta=True, out_dtype=BF16, tm=1408, tn=2048, tk=512, name="mm_down_dw")
        dup, dwf, dbf = ffn_bwd(up, upc, da, wf_full[l], name="ffn_bwd")
        dh2 = matmul(dup, Wup, tb=True, out_dtype=F32, tm=1024, tn=2048, tk=1408, name="mm_up_dx")
        g_up = matmul(h2, dup, ta=True, out_dtype=BF16, tm=2048, tn=1408, tk=512, groups=4, name="mm_up_dw")
        dt, dy, dg3, dg2 = norm_bwd(dt, dh2, x1, vec(norm_ffn_pre, l), y, vec(norm_mix_post, l), name="norm_bwd_mid")
        dymix = matmul(dy, Wout, tb=True, out_dtype=BF16, tm=512, tn=2048, tk=2048, name="mm_out_dx")
        g_out = matmul(ymix, dy, ta=True, out_dtype=BF16, tm=2048, tn=1024, tk=512, name="mm_out_dw")
        du, dwa, dwb, dbb, dlg, dlb, dpw, dps = mixer_bwd(
            u, cb, dymix, wa_full[l], wb_full[l], vec(conv_b_bias, l), vec(ln_b_gain, l), vec(ln_b_bias, l),
            pw_bf[l], vec(pool_scale, l), name="mixer_bwd")
        dh1 = matmul(du, Win, tb=True, out_dtype=F32, tm=512, tn=2048, tk=2176, name="mm_in_dx")
        g_in = matmul(h1, du, ta=True, out_dtype=BF16, tm=1024, tn=2176, tk=512, name="mm_in_dw")
        dg4_here = dg4
        if l > 0:
            dt, df, dg1, dg4 = norm_bwd(dt, dh1, x0, vec(norm_mix_pre, l), saved[l - 1][9], vec(norm_ffn_post, l - 1),
                                        name="norm_bwd_end")
        else:
            dt, _, dg1, _ = norm_bwd(dt, dh1, x0, vec(norm_mix_pre, 0), None, None, name="norm_bwd_bottom")
        small[l] = dict(norm_mix_pre=dg1, norm_mix_post=dg2, norm_ffn_pre=dg3, norm_ffn_post=dg4_here,
                        conv_a_w=dwa, conv_b_w=dwb, conv_b_bias=dbb, ln_b_gain=dlg, ln_b_bias=dlb,
                        pool_w=dpw, pool_scale=dps, conv_ffn_w=dwf, conv_ffn_bias=dbf)

        gs = [g_in.reshape(D, 4, DIN // 4).transpose(1, 0, 2), g_up,
              g_out.reshape(4, DMIX // 4, D), g_down.reshape(4, F // 4, D)]
        ts = sibling_swap_halves(gs, name="grad_swap_halves")
        ps = [pair_add(g, t, core, name="grad_pair_add_%d" % k) for k, (g, t) in enumerate(zip(gs, ts))]
        qs = chip_exchange(ps, name="grad_chip_exchange")
        fh = [sum_slots_half(q, core, name="grad_sum_slots_%d" % k) for k, q in enumerate(qs)]
        big[l] = sibling_join_halves(fh, name="grad_join_halves")
    grad_x = dt.reshape(1, S, D)

    rep_names = ["norm_mix_pre", "norm_mix_post", "norm_ffn_pre", "norm_ffn_post", "conv_b_bias", "ln_b_gain",
                 "ln_b_bias", "pool_w", "pool_scale", "conv_ffn_bias"]
    shd_names = ["conv_a_w", "conv_b_w", "conv_ffn_w"]
    given = dict(
        norm_mix_pre=(norm_mix_pre, m_norm_mix_pre, v_norm_mix_pre), norm_mix_post=(norm_mix_post, m_norm_mix_post, v_norm_mix_post),
        norm_ffn_pre=(norm_ffn_pre, m_norm_ffn_pre, v_norm_ffn_pre), norm_ffn_post=(norm_ffn_post, m_norm_ffn_post, v_norm_ffn_post),
        conv_b_bias=(conv_b_bias, m_conv_b_bias, v_conv_b_bias), ln_b_gain=(ln_b_gain, m_ln_b_gain, v_ln_b_gain),
        ln_b_bias=(ln_b_bias, m_ln_b_bias, v_ln_b_bias), pool_w=(pool_w, m_pool_w, v_pool_w),
        pool_scale=(pool_scale, m_pool_scale, v_pool_scale), conv_ffn_bias=(conv_ffn_bias, m_conv_ffn_bias, v_conv_ffn_bias),
        conv_a_w=(conv_a_w, m_conv_a_w, v_conv_a_w), conv_b_w=(conv_b_w, m_conv_b_w, v_conv_b_w),
        conv_ffn_w=(conv_ffn_w, m_conv_ffn_w, v_conv_ffn_w))
    full_shape = dict(conv_a_w=(L, K_A, WA), conv_b_w=(L, K_B, WB), conv_ffn_w=(L, K_F, F2))
    for nme in rep_names:
        full_shape[nme] = given[nme][0].shape
    names = rep_names + shd_names
    stacked = [jnp.stack([small[l][nme] for l in range(L)]).reshape(full_shape[nme]) for nme in names]
    parts = allgather8(_pack(stacked), name="gather_small_grads")
    totals = _unpack(sum_slots(parts, name="sum_small_grads"), [full_shape[nme] for nme in names])
    total = dict(zip(names, totals))
    for nme in shd_names:
        wd = full_shape[nme][2] // 4
        total[nme] = lax.dynamic_slice_in_dim(total[nme], chip * wd, wd, axis=2)
    shapes = [given[nme][0].shape for nme in names]
    d_s, m_s, v_s = adam_flat(_pack([given[nme][0] for nme in names]), _pack([total[nme] for nme in names]),
                              _pack([given[nme][1] for nme in names]), _pack([given[nme][2] for nme in names]),
                              name="adam_small")
    res = dict(zip(names, zip([total[nme] for nme in names], _unpack(d_s, shapes), _unpack(m_s, shapes),
                              _unpack(v_s, shapes))))

    for k, (nme, trio) in enumerate([("w_in", (w_in, m_w_in, v_w_in)), ("w_up", (w_up, m_w_up, v_w_up)),
                                     ("w_out", (w_out, m_w_out, v_w_out)), ("w_down", (w_down, m_w_down, v_w_down))]):
        res[nme] = adam_stacked(*trio, [big[l][k] for l in range(L)], name="adam_" + nme)

    order = ["norm_mix_pre", "norm_mix_post", "norm_ffn_pre", "norm_ffn_post", "w_in", "conv_a_w", "conv_b_w",
             "conv_b_bias", "ln_b_gain", "ln_b_bias", "pool_w", "pool_scale", "w_out", "w_up", "conv_ffn_w",
             "conv_ffn_bias", "w_down"]
    outs = [loss, grad_x]
    for k in range(4):
        outs += [res[nme][k] for nme in order]
    return tuple(outs)
```

```python
import functools

import jax
import jax.numpy as jnp
from jax import lax
from jax.experimental import pallas as pl
from jax.experimental.pallas import tpu as pltpu

F32 = jnp.float32
BF16 = jnp.bfloat16
MESH = pl.DeviceIdType.MESH

RMS_EPS = 1e-6
LN_EPS = 1e-5
ADAM_LR = 0.001
ADAM_B1 = 0.9
ADAM_B2 = 0.999
ADAM_EPS = 1e-08
ADAM_WD = 0.01
ADAM_STEP = 10
POOL_WINDOWS = (2, 4, 8, 16)
K_A = 3
K_B = 31
K_F = 3

LANES = 128
HALO_MIX = 32
HALO_FFN = 16
ROWS = 64
TM_MIXER = 512
TM_FFN = 256
TM_FFN_BWD = 128
TM_NORM = 256
ADAM_BLOCK_BYTES = 1 << 20
VMEM_LIMIT = 56 * 1024 * 1024


def _cp(n_axes):
    return pltpu.CompilerParams(dimension_semantics=("arbitrary",) * n_axes, vmem_limit_bytes=VMEM_LIMIT)


def _tile(dim, target, mult=LANES):
    if dim <= target:
        return dim
    t = (target // mult) * mult
    while t >= mult:
        if dim % t == 0:
            return t
        t -= mult
    return dim


def _chunks(n, rc=ROWS):
    out, r0 = [], 0
    while r0 < n:
        s = min(rc, n - r0)
        out.append((r0, s))
        r0 += s
    return out


def _sigmoid(x):
    return 1.0 / (1.0 + jnp.exp(-x))


def matmul(a, b, *, ta=False, tb=False, out_dtype, tm, tn, tk, j_outer=False, groups=1, after=None, name):
    if ta:
        K, M = a.shape
    else:
        M, K = a.shape
    if tb:
        N, K2 = b.shape
    else:
        K2, N = b.shape
    assert K == K2, (a.shape, b.shape)
    ng = N // groups
    tm, tn, tk = _tile(M, tm), _tile(ng, tn), _tile(K, tk)
    nm, nn, nk = M // tm, N // tn, K // tk
    per = ng // tn

    def ij(g0, g1):
        return (g1, g0) if j_outer else (g0, g1)

    def a_map(g0, g1, k):
        i, j = ij(g0, g1)
        return (k, i) if ta else (i, k)

    def b_map(g0, g1, k):
        i, j = ij(g0, g1)
        return (j, k) if tb else (k, j)

    def o_map(g0, g1, k):
        i, j = ij(g0, g1)
        return (j // per, i, j % per) if groups > 1 else (i, j)

    dims = (((0 if ta else 1,), (1 if tb else 0,)), ((), ()))
    use_acc = nk > 1 and out_dtype != F32

    def body(a_ref, b_ref, o_ref, *scratch):
        p = lax.dot_general(a_ref[...], b_ref[...], dims, preferred_element_type=F32)
        if nk == 1:
            o_ref[...] = p.astype(o_ref.dtype)
            return
        acc = scratch[0] if use_acc else o_ref
        k = pl.program_id(2)

        @pl.when(k == 0)
        def _():
            acc[...] = p

        @pl.when(k > 0)
        def _():
            acc[...] += p

        if use_acc:
            @pl.when(k == nk - 1)
            def _():
                o_ref[...] = acc[...].astype(o_ref.dtype)

    grid = (nn, nm, nk) if j_outer else (nm, nn, nk)
    if groups > 1:
        out_shape = jax.ShapeDtypeStruct((groups, M, ng), out_dtype)
        out_spec = pl.BlockSpec((None, tm, tn), o_map)
    else:
        out_shape = jax.ShapeDtypeStruct((M, N), out_dtype)
        out_spec = pl.BlockSpec((tm, tn), o_map)
    has_after = after is not None

    def body_after(a_ref, b_ref, after_ref, o_ref, *scratch):
        body(a_ref, b_ref, o_ref, *scratch)

    return pl.pallas_call(
        body_after if has_after else body, grid=grid,
        in_specs=[pl.BlockSpec((tk, tm) if ta else (tm, tk), a_map),
                  pl.BlockSpec((tn, tk) if tb else (tk, tn), b_map)]
                 + ([pl.BlockSpec(memory_space=pl.ANY)] if has_after else []),
        out_specs=out_spec, out_shape=out_shape,
        scratch_shapes=[pltpu.VMEM((tm, tn), F32)] if use_acc else [],
        compiler_params=_cp(3), name=name,
    )(*((a, b, after) if has_after else (a, b)))


def _rms(v):
    return lax.rsqrt(jnp.mean(v * v, axis=-1, keepdims=True) + RMS_EPS)


def norm_fwd(x, g, *, name):
    S, D = x.shape
    tm = _tile(S, TM_NORM, 16)

    def body(x_ref, g_ref, h_ref):
        v = x_ref[...]
        h_ref[...] = (v * _rms(v) * g_ref[...]).astype(BF16)

    return pl.pallas_call(
        body, grid=(S // tm,),
        in_specs=[pl.BlockSpec((tm, D), lambda i: (i, 0)), pl.BlockSpec((1, D), lambda i: (0, 0))],
        out_specs=pl.BlockSpec((tm, D), lambda i: (i, 0)),
        out_shape=jax.ShapeDtypeStruct((S, D), BF16), compiler_params=_cp(1), name=name,
    )(x, g)


def resid_norm_fwd(x, y, gp, gn, *, emit_h, name):
    S, D = x.shape
    tm = _tile(S, TM_NORM, 16)

    def body(x_ref, y_ref, gp_ref, gn_ref, xn_ref, *rest):
        yv = y_ref[...]
        xn = x_ref[...] + yv * _rms(yv) * gp_ref[...]
        xn_ref[...] = xn
        if emit_h:
            rest[0][...] = (xn * _rms(xn) * gn_ref[...]).astype(BF16)

    row = pl.BlockSpec((tm, D), lambda i: (i, 0))
    vec = pl.BlockSpec((1, D), lambda i: (0, 0))
    outs = pl.pallas_call(
        body, grid=(S // tm,), in_specs=[row, row, vec, vec],
        out_specs=[row, row] if emit_h else [row],
        out_shape=[jax.ShapeDtypeStruct((S, D), F32)] + ([jax.ShapeDtypeStruct((S, D), BF16)] if emit_h else []),
        compiler_params=_cp(1), name=name,
    )(x, y, gp, gn)
    return (outs[0], outs[1]) if emit_h else (outs[0], None)


def _rms_bwd(v, g, dout):
    r = _rms(v)
    gd = g * dout
    dv = r * gd - v * (r * r * r) * jnp.mean(v * gd, axis=-1, keepdims=True)
    return dv, dout * v * r


def norm_bwd(d_direct, dh, xn, gn, y, gp, *, name):
    S, D = d_direct.shape
    tm = _tile(S, TM_NORM, 16)
    has_h, has_y = dh is not None, y is not None

    def body(*refs):
        refs = list(refs)
        dd_ref = refs.pop(0)
        if has_h:
            dh_ref, xn_ref, gn_ref = refs.pop(0), refs.pop(0), refs.pop(0)
        if has_y:
            y_ref, gp_ref = refs.pop(0), refs.pop(0)
        if has_h:
            dt_ref = refs.pop(0)
        if has_y:
            dy_ref = refs.pop(0)
        if has_h:
            dgn_ref = refs.pop(0)
        if has_y:
            dgp_ref = refs.pop(0)
        i = pl.program_id(0)
        dt = dd_ref[...]
        if has_h:
            dv, gterm = _rms_bwd(xn_ref[...], gn_ref[...], dh_ref[...])
            dt = dt + dv
            dt_ref[...] = dt
            part = jnp.sum(gterm, axis=0, keepdims=True)

            @pl.when(i == 0)
            def _():
                dgn_ref[...] = part

            @pl.when(i > 0)
            def _():
                dgn_ref[...] += part
        if has_y:
            dy, gterm = _rms_bwd(y_ref[...], gp_ref[...], dt)
            dy_ref[...] = dy.astype(BF16)
            part2 = jnp.sum(gterm, axis=0, keepdims=True)

            @pl.when(i == 0)
            def _():
                dgp_ref[...] = part2

            @pl.when(i > 0)
            def _():
                dgp_ref[...] += part2

    row = pl.BlockSpec((tm, D), lambda i: (i, 0))
    vec = pl.BlockSpec((1, D), lambda i: (0, 0))
    ins, in_specs = [d_direct], [row]
    if has_h:
        ins += [dh, xn, gn]
        in_specs += [row, row, vec]
    if has_y:
        ins += [y, gp]
        in_specs += [row, vec]
    out_specs, out_shape = [], []
    if has_h:
        out_specs.append(row)
        out_shape.append(jax.ShapeDtypeStruct((S, D), F32))
    if has_y:
        out_specs.append(row)
        out_shape.append(jax.ShapeDtypeStruct((S, D), BF16))
    if has_h:
        out_specs.append(vec)
        out_shape.append(jax.ShapeDtypeStruct((1, D), F32))
    if has_y:
        out_specs.append(vec)
        out_shape.append(jax.ShapeDtypeStruct((1, D), F32))
    outs = list(pl.pallas_call(
        body, grid=(S // tm,), in_specs=in_specs, out_specs=out_specs, out_shape=out_shape,
        compiler_params=_cp(1), name=name,
    )(*ins))
    dt = outs.pop(0) if has_h else d_direct
    dy = outs.pop(0) if has_y else None
    dgn = outs.pop(0) if has_h else None
    dgp = outs.pop(0) if has_y else None
    return dt, dy, dgn, dgp


def loss_head(xl, target, *, name):
    S, D = xl.shape
    tm = _tile(S, TM_NORM, 16)

    def body(x_ref, t_ref, dx_ref, l_ref):
        i = pl.program_id(0)
        e = x_ref[...] - t_ref[...]
        dx_ref[...] = e * (1.0 / D)
        part = jnp.sum(e * e)

        @pl.when(i == 0)
        def _():
            l_ref[...] = jnp.zeros_like(l_ref) + part

        @pl.when(i > 0)
        def _():
            l_ref[...] += part

    row = pl.BlockSpec((tm, D), lambda i: (i, 0))
    return pl.pallas_call(
        body, grid=(S // tm,), in_specs=[row, row],
        out_specs=[row, pl.BlockSpec((8, LANES), lambda i: (0, 0))],
        out_shape=[jax.ShapeDtypeStruct((S, D), F32), jax.ShapeDtypeStruct((8, LANES), F32)],
        compiler_params=_cp(1), name=name,
    )(xl, target)


def _tap_sum(src, base, offs, wrows, r0, rc, c0):
    acc = None
    for k, off in enumerate(offs):
        t = src[base + r0 + off: base + r0 + off + rc, c0:c0 + LANES]
        if wrows is not None:
            t = t * wrows[k]
        acc = t if acc is None else acc + t
    return acc


def _tap_wgrad(out_ref, o0, a, a_base, b, b_base, offs, n, c0, first):
    for k, off in enumerate(offs):
        acc = None
        for r0, rc in _chunks(n):
            t = (a[a_base + r0: a_base + r0 + rc, c0:c0 + LANES]
                 * b[b_base + r0 + off: b_base + r0 + off + rc, c0:c0 + LANES])
            t = jnp.sum(t, axis=0, keepdims=True)
            acc = t if acc is None else acc + t
        _acc_store(out_ref, (slice(k, k + 1), slice(o0 + c0, o0 + c0 + LANES)), acc, first)


def _shift_copies(sh, src, n, shifts, c0):
    for b in shifts:
        for r0, rc in _chunks(n):
            sh[b, r0:r0 + rc, :] = src[r0 + b:r0 + b + rc, c0:c0 + LANES]


def _tap(sh, src, o, r0, rc, c0):
    a, b = divmod(o, 8)
    if b == 0:
        return src[r0 + o:r0 + o + rc, c0:c0 + LANES]
    return sh[b, r0 + 8 * a:r0 + 8 * a + rc, :]


def _acc_store(ref, idx, val, first):
    @pl.when(first)
    def _():
        ref[idx] = val

    @pl.when(jnp.logical_not(first))
    def _():
        ref[idx] += val


def _row_counts(t0, rc, w):
    t = t0 + lax.broadcasted_iota(jnp.int32, (rc, LANES), 0)
    return jnp.minimum(t + 1, w).astype(F32)


def mixer_fwd(u, wa, wb, bb, lg, lb, pw, ps, *, name):
    S, DIN = u.shape
    WA, WB, WC = wa.shape[1], wb.shape[1], ps.shape[1]
    DMIX = WA + WB + WC
    tm = _tile(S, TM_MIXER, HALO_MIX)
    HB = HALO_MIX
    r = tm // HB
    oCg, oVa, oVal, oGate, oC = WA, 2 * WA, 3 * WA, 3 * WA + WB, 3 * WA + 2 * WB
    offs_a = [k - (K_A - 1) for k in range(K_A)]
    offs_b = [k - (K_B - 1) for k in range(K_B)]

    def body(u_ref, uh_ref, wa_ref, wb_ref, bb_ref, lg_ref, lb_ref, pw_ref, ps_ref, y_ref, cbuf,
             pbuf, gbuf, shbuf, xbuf, plbuf):
        i = pl.program_id(0)
        hv = jnp.where(i > 0, 1.0, 0.0).astype(F32)

        def fill(src, dst0, n, scale):
            for r0, rc in _chunks(n):
                rows, drows = slice(r0, r0 + rc), slice(dst0 + r0, dst0 + r0 + rc)
                for c0 in range(0, WA, LANES):
                    v = (src[rows, oCg + c0:oCg + c0 + LANES].astype(F32)
                         * src[rows, oVa + c0:oVa + c0 + LANES].astype(F32))
                    pbuf[drows, c0:c0 + LANES] = v if scale is None else v * scale
                for c0 in range(0, WB, LANES):
                    v = (src[rows, oVal + c0:oVal + c0 + LANES].astype(F32)
                         * _sigmoid(src[rows, oGate + c0:oGate + c0 + LANES].astype(F32)))
                    gbuf[drows, c0:c0 + LANES] = v if scale is None else v * scale
                for c0 in range(0, WC, LANES):
                    v = src[rows, oC + c0:oC + c0 + LANES].astype(F32)
                    xbuf[drows, c0:c0 + LANES] = v if scale is None else v * scale

        fill(uh_ref, 0, HB, hv)
        fill(u_ref, HB, tm, None)

        for c0 in range(0, WA, LANES):
            w = [wa_ref[k:k + 1, c0:c0 + LANES] for k in range(K_A)]
            for r0, rc in _chunks(tm):
                q = _tap_sum(pbuf, HB, offs_a, w, r0, rc, c0)
                bg = u_ref[r0:r0 + rc, c0:c0 + LANES].astype(F32)
                y_ref[r0:r0 + rc, c0:c0 + LANES] = (bg * q).astype(BF16)

        for c0 in range(0, WB, LANES):
            w = [wb_ref[k:k + 1, c0:c0 + LANES] for k in range(K_B)]
            bias = bb_ref[:, c0:c0 + LANES]
            _shift_copies(shbuf, gbuf, tm + 24, range(1, 8), c0)
            for r0, rc in _chunks(tm):
                acc = bias
                for k in range(K_B):
                    acc = acc + _tap(shbuf, gbuf, HB - (K_B - 1) + k, r0, rc, c0) * w[k]
                cbuf[r0:r0 + rc, c0:c0 + LANES] = acc
        for r0, rc in _chunks(tm, 32):
            cb = cbuf[r0:r0 + rc, :]
            mu = jnp.mean(cb, axis=-1, keepdims=True)
            d = cb - mu
            n = d * lax.rsqrt(jnp.mean(d * d, axis=-1, keepdims=True) + LN_EPS)
            z = n * lg_ref[...] + lb_ref[...]
            y_ref[r0:r0 + rc, WA:WA + WB] = (z * _sigmoid(z)).astype(BF16)

        for g, win in enumerate(POOL_WINDOWS):
            c0 = g * LANES
            for r0, rc in _chunks(tm):
                s = _tap_sum(xbuf, HB, [-j for j in range(win)], None, r0, rc, c0)
                pooled = s / _row_counts(i * tm + r0, rc, win) - xbuf[HB + r0:HB + r0 + rc, c0:c0 + LANES]
                plbuf[r0:r0 + rc, c0:c0 + LANES] = pooled.astype(BF16)
            mixed = jnp.dot(plbuf[:, c0:c0 + LANES], pw_ref[g], preferred_element_type=F32)
            y_ref[:, WA + WB + c0:WA + WB + c0 + LANES] = (mixed * ps_ref[:, c0:c0 + LANES]).astype(BF16)

    full = lambda a: pl.BlockSpec(a.shape, lambda i: (0,) * a.ndim)
    return pl.pallas_call(
        body, grid=(S // tm,),
        in_specs=[pl.BlockSpec((tm, DIN), lambda i: (i, 0)),
                  pl.BlockSpec((HB, DIN), lambda i: (jnp.maximum(i * r - 1, 0), 0)),
                  full(wa), full(wb), full(bb), full(lg), full(lb), full(pw), full(ps)],
        out_specs=[pl.BlockSpec((tm, DMIX), lambda i: (i, 0)), pl.BlockSpec((tm, WB), lambda i: (i, 0))],
        out_shape=[jax.ShapeDtypeStruct((S, DMIX), BF16), jax.ShapeDtypeStruct((S, WB), F32)],
        scratch_shapes=[pltpu.VMEM((HB + tm, WA), F32), pltpu.VMEM((HB + tm, WB), F32),
                        pltpu.VMEM((8, tm + 24, LANES), F32), pltpu.VMEM((HB + tm, WC), F32),
                        pltpu.VMEM((tm, WC), BF16)],
        compiler_params=_cp(1), name=name,
    )(u, u, wa, wb, bb, lg, lb, pw, ps)


def mixer_bwd(u, cb, dy, wa, wb, bb, lg, lb, pw, ps, *, name):
    S, DIN = u.shape
    WA, WB, WC = wa.shape[1], wb.shape[1], ps.shape[1]
    NG = WC // LANES
    DMIX = WA + WB + WC
    tm = _tile(S, TM_MIXER, HALO_MIX)
    HB = HALO_MIX
    r = tm // HB
    nI = S // tm
    nH = S // HB
    oCg, oVa, oVal, oGate, oC = WA, 2 * WA, 3 * WA, 3 * WA + WB, 3 * WA + 2 * WB
    offs_a = [k - (K_A - 1) for k in range(K_A)]
    offs_b = [k - (K_B - 1) for k in range(K_B)]
    adj_a = [(K_A - 1) - k for k in range(K_A)]
    adj_b = [(K_B - 1) - k for k in range(K_B)]

    def body(u_ref, ub_ref, ua_ref, cb_ref, cba_ref, dy_ref, dya_ref,
             wa_ref, wb_ref, bb_ref, lg_ref, lb_ref, pw_ref, ps_ref,
             du_ref, dwa_ref, dwb_ref, dbb_ref, dlg_ref, dlb_ref, dpw_ref, dps_ref,
             pbuf, dqbuf, gbuf, dcbuf, shbuf, xbuf, plbuf, dmbuf, dplbuf, dpcbuf):
        i = pl.program_id(0)
        first = i == 0
        hvb = jnp.where(i > 0, 1.0, 0.0).astype(F32)
        hva = jnp.where(i < nI - 1, 1.0, 0.0).astype(F32)

        def fill(src, dst0, n, scale, main):
            for r0, rc in _chunks(n):
                rows, drows = slice(r0, r0 + rc), slice(dst0 + r0, dst0 + r0 + rc)
                for c0 in range(0, WA, LANES):
                    v = (src[rows, oCg + c0:oCg + c0 + LANES].astype(F32)
                         * src[rows, oVa + c0:oVa + c0 + LANES].astype(F32))
                    pbuf[drows, c0:c0 + LANES] = v if scale is None else v * scale
                for c0 in range(0, WC, LANES):
                    v = src[rows, oC + c0:oC + c0 + LANES].astype(F32)
                    xbuf[drows, c0:c0 + LANES] = v if scale is None else v * scale
                if main:
                    for c0 in range(0, WB, LANES):
                        gbuf[rows, c0:c0 + LANES] = (
                            src[rows, oVal + c0:oVal + c0 + LANES].astype(F32)
                            * _sigmoid(src[rows, oGate + c0:oGate + c0 + LANES].astype(F32)))

        fill(ub_ref, 0, HB, hvb, False)
        fill(u_ref, HB, tm, None, True)

        for r0, rc in _chunks(tm):
            for c0 in range(0, WA, LANES):
                dqbuf[r0:r0 + rc, c0:c0 + LANES] = (dy_ref[r0:r0 + rc, c0:c0 + LANES].astype(F32)
                                                     * u_ref[r0:r0 + rc, c0:c0 + LANES].astype(F32))
        for c0 in range(0, WA, LANES):
            dqbuf[tm:tm + HB, c0:c0 + LANES] = (dya_ref[:, c0:c0 + LANES].astype(F32)
                                                * ua_ref[:, c0:c0 + LANES].astype(F32)) * hva
        for c0 in range(0, WA, LANES):
            w = [wa_ref[k:k + 1, c0:c0 + LANES] for k in range(K_A)]
            for r0, rc in _chunks(tm):
                rows = slice(r0, r0 + rc)
                q = _tap_sum(pbuf, HB, offs_a, w, r0, rc, c0)
                du_ref[rows, c0:c0 + LANES] = (dy_ref[rows, c0:c0 + LANES].astype(F32) * q).astype(BF16)
                dp = _tap_sum(dqbuf, 0, adj_a, w, r0, rc, c0)
                cg = u_ref[rows, oCg + c0:oCg + c0 + LANES].astype(F32)
                va = u_ref[rows, oVa + c0:oVa + c0 + LANES].astype(F32)
                du_ref[rows, oCg + c0:oCg + c0 + LANES] = (dp * va).astype(BF16)
                du_ref[rows, oVa + c0:oVa + c0 + LANES] = (dp * cg).astype(BF16)
            _tap_wgrad(dwa_ref, 0, dqbuf, 0, pbuf, HB, offs_a, tm, c0, first)

        def ln_chunk(r0, rc, cb, dyb, scale, main):
            mu = jnp.mean(cb, axis=-1, keepdims=True)
            d = cb - mu
            rs = lax.rsqrt(jnp.mean(d * d, axis=-1, keepdims=True) + LN_EPS)
            n = d * rs
            z = n * lg_ref[...] + lb_ref[...]
            sg = _sigmoid(z)
            dz = dyb * (sg * (1.0 + z * (1.0 - sg)))
            dn = dz * lg_ref[...]
            dcb = rs * (dn - jnp.mean(dn, axis=-1, keepdims=True) - n * jnp.mean(dn * n, axis=-1, keepdims=True))
            if scale is not None:
                dcb = dcb * scale
            dcbuf[r0:r0 + rc, :] = dcb
            if main:
                return (jnp.sum(dz * n, axis=0, keepdims=True), jnp.sum(dz, axis=0, keepdims=True),
                        jnp.sum(dcb, axis=0, keepdims=True))
            return None

        sums = None
        for r0, rc in _chunks(tm, 32):
            part = ln_chunk(r0, rc, cb_ref[r0:r0 + rc, :], dy_ref[r0:r0 + rc, WA:WA + WB].astype(F32), None, True)
            sums = part if sums is None else tuple(a + b for a, b in zip(sums, part))
        ln_chunk(tm, HB, cba_ref[...], dya_ref[:, WA:WA + WB].astype(F32), hva, False)
        _acc_store(dlg_ref, (slice(None), slice(None)), sums[0], first)
        _acc_store(dlb_ref, (slice(None), slice(None)), sums[1], first)
        _acc_store(dbb_ref, (slice(None), slice(None)), sums[2], first)

        for c0 in range(0, WB, LANES):
            w = [wb_ref[k:k + 1, c0:c0 + LANES] for k in range(K_B)]
            _shift_copies(shbuf, dcbuf, tm + 24, range(1, 8), c0)
            for r0, rc in _chunks(tm):
                rows = slice(r0, r0 + rc)
                dglu = None
                for k in range(K_B):
                    t = _tap(shbuf, dcbuf, adj_b[k], r0, rc, c0) * w[k]
                    dglu = t if dglu is None else dglu + t
                val = u_ref[rows, oVal + c0:oVal + c0 + LANES].astype(F32)
                sg = _sigmoid(u_ref[rows, oGate + c0:oGate + c0 + LANES].astype(F32))
                du_ref[rows, oVal + c0:oVal + c0 + LANES] = (dglu * sg).astype(BF16)
                du_ref[rows, oGate + c0:oGate + c0 + LANES] = (dglu * val * sg * (1.0 - sg)).astype(BF16)
            for k in range(K_B):
                acc = None
                for r0, rc in _chunks(tm):
                    t = _tap(shbuf, dcbuf, adj_b[k], r0, rc, c0) * gbuf[r0:r0 + rc, c0:c0 + LANES]
                    acc = t if acc is None else acc + t
                _acc_store(dwb_ref, (slice(k, k + 1), slice(c0, c0 + LANES)),
                           jnp.sum(acc, axis=0, keepdims=True), first)

        for g, win in enumerate(POOL_WINDOWS):
            c0 = g * LANES
            cols = slice(c0, c0 + LANES)
            ycols = slice(WA + WB + c0, WA + WB + c0 + LANES)
            for r0, rc in _chunks(tm):
                s = _tap_sum(xbuf, HB, [-j for j in range(win)], None, r0, rc, c0)
                pooled = s / _row_counts(i * tm + r0, rc, win) - xbuf[HB + r0:HB + r0 + rc, cols]
                plbuf[r0:r0 + rc, cols] = pooled.astype(BF16)
            mixed = jnp.dot(plbuf[:, cols], pw_ref[g], preferred_element_type=F32)
            dyc = dy_ref[:, ycols].astype(F32)
            _acc_store(dps_ref, (slice(None), cols), jnp.sum(dyc * mixed, axis=0, keepdims=True), first)
            dmbuf[0:tm, :] = (dyc * ps_ref[:, cols]).astype(BF16)
            dmbuf[tm:tm + HB, :] = (dya_ref[:, ycols].astype(F32) * ps_ref[:, cols] * hva).astype(BF16)
            dpw = lax.dot_general(plbuf[:, cols], dmbuf[0:tm, :], (((0,), (0,)), ((), ())),
                                  preferred_element_type=F32)
            _acc_store(dpw_ref, (g, slice(None), slice(None)), dpw, first)
            dplbuf[...] = lax.dot_general(dmbuf[...], pw_ref[g], (((1,), (1,)), ((), ())),
                                          preferred_element_type=F32)
            for r0, rc in _chunks(tm + HB):
                dpcbuf[r0:r0 + rc, :] = dplbuf[r0:r0 + rc, :] / _row_counts(i * tm + r0, rc, win)
            for r0, rc in _chunks(tm):
                duc = _tap_sum(dpcbuf, 0, list(range(win)), None, r0, rc, 0) - dplbuf[r0:r0 + rc, :]
                du_ref[r0:r0 + rc, oC + c0:oC + c0 + LANES] = duc.astype(BF16)

    full = lambda a: pl.BlockSpec(a.shape, lambda i: (0,) * a.ndim)
    acc = lambda shape: pl.BlockSpec(shape, lambda i: (0,) * len(shape))
    small = [(K_A, WA), (K_B, WB), (1, WB), (1, WB), (1, WB), (NG, LANES, LANES), (1, WC)]
    outs = pl.pallas_call(
        body, grid=(nI,),
        in_specs=[pl.BlockSpec((tm, DIN), lambda i: (i, 0)),
                  pl.BlockSpec((HB, DIN), lambda i: (jnp.maximum(i * r - 1, 0), 0)),
                  pl.BlockSpec((HB, DIN), lambda i: (jnp.minimum((i + 1) * r, nH - 1), 0)),
                  pl.BlockSpec((tm, WB), lambda i: (i, 0)),
                  pl.BlockSpec((HB, WB), lambda i: (jnp.minimum((i + 1) * r, nH - 1), 0)),
                  pl.BlockSpec((tm, DMIX), lambda i: (i, 0)),
                  pl.BlockSpec((HB, DMIX), lambda i: (jnp.minimum((i + 1) * r, nH - 1), 0)),
                  full(wa), full(wb), full(bb), full(lg), full(lb), full(pw), full(ps)],
        out_specs=[pl.BlockSpec((tm, DIN), lambda i: (i, 0))] + [acc(s) for s in small],
        out_shape=[jax.ShapeDtypeStruct((S, DIN), BF16)] + [jax.ShapeDtypeStruct(s, F32) for s in small],
        scratch_shapes=[pltpu.VMEM((HB + tm, WA), F32), pltpu.VMEM((tm + HB, WA), F32),
                        pltpu.VMEM((tm, WB), F32), pltpu.VMEM((tm + HB, WB), F32),
                        pltpu.VMEM((8, tm + 24, LANES), F32), pltpu.VMEM((HB + tm, WC), F32),
                        pltpu.VMEM((tm, WC), BF16), pltpu.VMEM((tm + HB, LANES), BF16),
                        pltpu.VMEM((tm + HB, LANES), F32), pltpu.VMEM((tm + HB, LANES), F32)],
        compiler_params=_cp(1), name=name,
    )(u, u, u, cb, cb, dy, dy, wa, wb, bb, lg, lb, pw, ps)
    return outs


def ffn_fwd(up, wf, bf, *, name):
    S, F2 = up.shape
    F = F2 // 2
    tm = _tile(S, TM_FFN, HALO_FFN)
    HB = HALO_FFN
    r = tm // HB
    CW = _tile(F, 512)
    offs = [k - (K_F - 1) for k in range(K_F)]

    def body(up_ref, uph_ref, wf_ref, bf_ref, a_ref, upc_ref, ebuf):
        i = pl.program_id(0)
        hv = jnp.where(i > 0, 1.0, 0.0).astype(F32)
        for c0 in range(0, F, CW):
            for h, off in ((0, c0), (1, F + c0)):
                ebuf[h, 0:HB, :] = uph_ref[:, off:off + CW].astype(F32) * hv
                for r0, rc in _chunks(tm):
                    ebuf[h, HB + r0:HB + r0 + rc, :] = up_ref[r0:r0 + rc, off:off + CW].astype(F32)
            for l0 in range(0, CW, LANES):
                cg, cv = c0 + l0, F + c0 + l0
                wg = [wf_ref[k:k + 1, cg:cg + LANES] for k in range(K_F)]
                wv = [wf_ref[k:k + 1, cv:cv + LANES] for k in range(K_F)]
                bg = bf_ref[:, cg:cg + LANES]
                bv = bf_ref[:, cv:cv + LANES]
                for r0, rc in _chunks(tm):
                    gt = _tap_sum(ebuf.at[0], HB, offs, wg, r0, rc, l0) + bg
                    vl = _tap_sum(ebuf.at[1], HB, offs, wv, r0, rc, l0) + bv
                    a_ref[r0:r0 + rc, cg:cg + LANES] = (gt * _sigmoid(gt) * vl).astype(BF16)
                    upc_ref[r0:r0 + rc, cg:cg + LANES] = gt.astype(BF16)
                    upc_ref[r0:r0 + rc, cv:cv + LANES] = vl.astype(BF16)

    full = lambda a: pl.BlockSpec(a.shape, lambda i: (0,) * a.ndim)
    return pl.pallas_call(
        body, grid=(S // tm,),
        in_specs=[pl.BlockSpec((tm, F2), lambda i: (i, 0)),
                  pl.BlockSpec((HB, F2), lambda i: (jnp.maximum(i * r - 1, 0), 0)),
                  full(wf), full(bf)],
        out_specs=[pl.BlockSpec((tm, F), lambda i: (i, 0)), pl.BlockSpec((tm, F2), lambda i: (i, 0))],
        out_shape=[jax.ShapeDtypeStruct((S, F), BF16), jax.ShapeDtypeStruct((S, F2), BF16)],
        scratch_shapes=[pltpu.VMEM((2, HB + tm, CW), F32)],
        compiler_params=_cp(1), name=name,
    )(up, up, wf, bf)


def ffn_bwd(up, upc, da, wf, *, name):
    S, F2 = up.shape
    F = F2 // 2
    tm = _tile(S, TM_FFN_BWD, HALO_FFN)
    HB = HALO_FFN
    r = tm // HB
    nI = S // tm
    nH = S // HB
    CW = _tile(F, 512)
    adj = [(K_F - 1) - k for k in range(K_F)]

    def body(up_ref, upc_ref, upca_ref, da_ref, daa_ref, wf_ref, dup_ref, dwf_ref, dbf_ref, dbuf, shbuf):
        i = pl.program_id(0)
        first = i == 0
        hva = jnp.where(i < nI - 1, 1.0, 0.0).astype(F32)
        for c0 in range(0, F, CW):
            for l0 in range(0, CW, LANES):
                cg, cv = c0 + l0, F + c0 + l0
                wg = [wf_ref[k:k + 1, cg:cg + LANES] for k in range(K_F)]
                wv = [wf_ref[k:k + 1, cv:cv + LANES] for k in range(K_F)]
                sg_sum, sv_sum = None, None
                for r0, rc in _chunks(tm + HB):
                    if r0 < tm:
                        gt = upc_ref[r0:r0 + rc, cg:cg + LANES].astype(F32)
                        vl = upc_ref[r0:r0 + rc, cv:cv + LANES].astype(F32)
                        d = da_ref[r0:r0 + rc, cg:cg + LANES].astype(F32)
                    else:
                        gt = upca_ref[:, cg:cg + LANES].astype(F32)
                        vl = upca_ref[:, cv:cv + LANES].astype(F32)
                        d = daa_ref[:, cg:cg + LANES].astype(F32) * hva
                    s = _sigmoid(gt)
                    dg = d * vl * (s * (1.0 + gt * (1.0 - s)))
                    dv = d * (gt * s)
                    dbuf[0, r0:r0 + rc, :] = dg
                    dbuf[1, r0:r0 + rc, :] = dv
                    if r0 < tm:
                        pg, pv = jnp.sum(dg, axis=0, keepdims=True), jnp.sum(dv, axis=0, keepdims=True)
                        sg_sum = pg if sg_sum is None else sg_sum + pg
                        sv_sum = pv if sv_sum is None else sv_sum + pv
                _acc_store(dbf_ref, (slice(None), slice(cg, cg + LANES)), sg_sum, first)
                _acc_store(dbf_ref, (slice(None), slice(cv, cv + LANES)), sv_sum, first)
                for h, w, col in ((0, wg, cg), (1, wv, cv)):
                    d_h, sh_h = dbuf.at[h], shbuf.at[h]
                    _shift_copies(sh_h, d_h, tm, (1, 2), 0)
                    accs = [None] * K_F
                    for r0, rc in _chunks(tm):
                        taps = [_tap(sh_h, d_h, adj[k], r0, rc, 0) for k in range(K_F)]
                        dup_ref[r0:r0 + rc, col:col + LANES] = (
                            taps[0] * w[0] + taps[1] * w[1] + taps[2] * w[2]).astype(BF16)
                        uv = up_ref[r0:r0 + rc, col:col + LANES].astype(F32)
                        for k in range(K_F):
                            t = taps[k] * uv
                            accs[k] = t if accs[k] is None else accs[k] + t
                    for k in range(K_F):
                        _acc_store(dwf_ref, (slice(k, k + 1), slice(col, col + LANES)),
                                   jnp.sum(accs[k], axis=0, keepdims=True), first)

    full = lambda a: pl.BlockSpec(a.shape, lambda i: (0,) * a.ndim)
    return pl.pallas_call(
        body, grid=(nI,),
        in_specs=[pl.BlockSpec((tm, F2), lambda i: (i, 0)),
                  pl.BlockSpec((tm, F2), lambda i: (i, 0)),
                  pl.BlockSpec((HB, F2), lambda i: (jnp.minimum((i + 1) * r, nH - 1), 0)),
                  pl.BlockSpec((tm, F), lambda i: (i, 0)),
                  pl.BlockSpec((HB, F), lambda i: (jnp.minimum((i + 1) * r, nH - 1), 0)),
                  full(wf)],
        out_specs=[pl.BlockSpec((tm, F2), lambda i: (i, 0)),
                   pl.BlockSpec((K_F, F2), lambda i: (0, 0)), pl.BlockSpec((1, F2), lambda i: (0, 0))],
        out_shape=[jax.ShapeDtypeStruct((S, F2), BF16), jax.ShapeDtypeStruct((K_F, F2), F32),
                   jax.ShapeDtypeStruct((1, F2), F32)],
        scratch_shapes=[pltpu.VMEM((2, tm + HB, LANES), F32), pltpu.VMEM((2, 3, tm, LANES), F32)],
        compiler_params=_cp(1), name=name,
    )(up, upc, upc, da, da, wf)


HBM = pl.BlockSpec(memory_space=pltpu.HBM)


def _place():
    return lax.axis_index("x"), lax.axis_index("y"), lax.axis_index("c")


def allgather8(buf, *, name):
    R, C = buf.shape

    def body(x_ref, o_ref, send_sems, recv_sems, local_sem):
        x, y, c = _place()
        me = 4 * x + 2 * y + c
        mine = pltpu.make_async_copy(x_ref, o_ref.at[me], local_sem)
        mine.start()
        sends = []
        for k in range(1, 8):
            fx, fy, fc = (k >> 2) & 1, (k >> 1) & 1, k & 1
            px, py, pc = (x + fx) % 2, (y + fy) % 2, (c + fc) % 2
            cp = pltpu.make_async_remote_copy(
                src_ref=x_ref, dst_ref=o_ref.at[me], send_sem=send_sems.at[k - 1], recv_sem=recv_sems.at[k - 1],
                device_id=(px, py, pc), device_id_type=MESH)
            cp.start()
            sends.append(cp)
        for k in range(1, 8):
            fx, fy, fc = (k >> 2) & 1, (k >> 1) & 1, k & 1
            peer = 4 * ((x + fx) % 2) + 2 * ((y + fy) % 2) + (c + fc) % 2
            pltpu.make_async_remote_copy(
                src_ref=x_ref, dst_ref=o_ref.at[peer], send_sem=send_sems.at[k - 1], recv_sem=recv_sems.at[k - 1],
                device_id=(x, y, c), device_id_type=MESH).wait_recv()
        for cp in sends:
            cp.wait_send()
        mine.wait()

    return pl.pallas_call(
        body, in_specs=[HBM], out_specs=HBM, out_shape=jax.ShapeDtypeStruct((8, R, C), buf.dtype),
        scratch_shapes=[pltpu.SemaphoreType.DMA((7,)), pltpu.SemaphoreType.DMA((7,)), pltpu.SemaphoreType.DMA],
        name=name,
    )(buf)


def gather_chip_shards(shards, *, name):
    n = len(shards)

    def body(*refs):
        ins, outs = refs[:n], refs[n:2 * n]
        send_sems, recv_sems = refs[2 * n:]
        x, y, c = _place()
        b = 2 * x + y
        chips = [(1 - x, y), (x, 1 - y), (1 - x, 1 - y)]
        sends = []

        def half(a, which):
            rh = ins[a].shape[0] // 2
            return pl.ds(pl.multiple_of(which * rh, 16), rh)

        def copy(a, k, src, dst, to):
            return pltpu.make_async_remote_copy(
                src_ref=src, dst_ref=dst, send_sem=send_sems.at[7 * a + k], recv_sem=recv_sems.at[7 * a + k],
                device_id=to, device_id_type=MESH)

        for a in range(n):
            for j, (cx, cy) in enumerate(chips):
                cp = copy(a, j, ins[a].at[half(a, c)], outs[a].at[b, half(a, c)], (cx, cy, c))
                cp.start()
                sends.append(cp)
        for a in range(n):
            cp = copy(a, 6, ins[a], outs[a].at[b], (x, y, 1 - c))
            cp.start()
            sends.append(cp)
        for a in range(n):
            for j, (cx, cy) in enumerate(chips):
                got = outs[a].at[2 * cx + cy, half(a, c)]
                copy(a, j, got, got, (x, y, c)).wait_recv()
                cp = copy(a, 3 + j, got, got, (x, y, 1 - c))
                cp.start()
                sends.append(cp)
        for a in range(n):
            for j, (cx, cy) in enumerate(chips):
                got = outs[a].at[2 * cx + cy, half(a, 1 - c)]
                copy(a, 3 + j, got, got, (x, y, c)).wait_recv()
            copy(a, 6, ins[a], outs[a].at[b], (x, y, c)).wait_recv()
        for cp in sends:
            cp.wait_send()

    return pl.pallas_call(
        body, in_specs=[HBM] * n, out_specs=[HBM] * n,
        out_shape=[jax.ShapeDtypeStruct((4,) + s.shape, s.dtype) for s in shards],
        scratch_shapes=[pltpu.SemaphoreType.DMA((7 * n,)), pltpu.SemaphoreType.DMA((7 * n,))],
        name=name,
    )(*shards)


def sibling_swap_halves(gs, *, name):
    n = len(gs)

    def body(*refs):
        ins, outs = refs[:n], refs[n:2 * n]
        send_sems, recv_sems = refs[2 * n:]
        x, y, c = _place()
        cps = []
        for a in range(n):
            rh = ins[a].shape[1] // 2
            src = ins[a].at[:, pl.ds(pl.multiple_of((1 - c) * rh, 16), rh)]
            cp = pltpu.make_async_remote_copy(
                src_ref=src, dst_ref=outs[a], send_sem=send_sems.at[a], recv_sem=recv_sems.at[a],
                device_id=(x, y, 1 - c), device_id_type=MESH)
            cp.start()
            cps.append(cp)
        for cp in cps:
            cp.wait()

    return pl.pallas_call(
        body, in_specs=[HBM] * n, out_specs=[HBM] * n,
        out_shape=[jax.ShapeDtypeStruct((4, g.shape[1] // 2, g.shape[2]), g.dtype) for g in gs],
        scratch_shapes=[pltpu.SemaphoreType.DMA((n,)), pltpu.SemaphoreType.DMA((n,))],
        name=name,
    )(*gs)


SEM =pl.BlockSpec(memory_space=pltpu.SEMAPHORE)
SPLIT_COPY = pltpu.CompilerParams(has_side_effects=pltpu.SideEffectType.DATAFLOW_SIDE_EFFECTING)


def _split_start(srcs, lands, n_copies, issue, *, name):
    n, m = len(srcs), len(lands)

    def body(*refs):
        ins, lnd = refs[:n], refs[n:n + m]
        send_sems, recv_sems = refs[n + m], refs[n + m + 1]
        token = refs[-1]

        def copy(k, src, dst, to):
            return pltpu.make_async_remote_copy(src_ref=src, dst_ref=dst, send_sem=send_sems.at[k],
                                                recv_sem=recv_sems.at[k], device_id=to, device_id_type=MESH)

        for cp in issue(ins, lnd, copy):
            cp.start()
        token[...] = jnp.zeros_like(token)

    outs = pl.pallas_call(
        body, name=name,
        out_shape=(pltpu.SemaphoreType.DMA((n_copies,)), pltpu.SemaphoreType.DMA((n_copies,)),
                   *[pltpu.HBM(a.shape, a.dtype) for a in list(srcs) + list(lands)],
                   jax.ShapeDtypeStruct((8, LANES), F32)),
        in_specs=[HBM] * (n + m),
        out_specs=(SEM, SEM, *([HBM] * (n + m)), pl.BlockSpec(memory_space=pltpu.VMEM)),
        input_output_aliases={i: 2 + i for i in range(n + m)},
        compiler_params=SPLIT_COPY,
    )(*[pltpu.with_memory_space_constraint(a, pltpu.HBM) for a in list(srcs) + list(lands)])
    return outs[0], outs[1], list(outs[2:2 + n]), list(outs[2 + n:2 + n + m]), outs[-1]


def _split_wait(send_sems, recv_sems, srcs, lands, after, issue, *, name):
    n, m = len(srcs), len(lands)

    def body(*refs):
        ins, lnd = refs[:n], refs[n:n + m]
        s_sems, r_sems = refs[n + m], refs[n + m + 1]

        def copy(k, src, dst, to):
            return pltpu.make_async_remote_copy(src_ref=src, dst_ref=dst, send_sem=s_sems.at[k],
                                                recv_sem=r_sems.at[k], device_id=to, device_id_type=MESH)

        for cp in issue(ins, lnd, copy):
            cp.wait_send()
            cp.wait_recv()

    outs = pl.pallas_call(
        body, name=name,
        out_shape=tuple(pltpu.HBM(a.shape, a.dtype) for a in list(srcs) + list(lands)),
        in_specs=[HBM] * (n + m) + [SEM, SEM, pl.BlockSpec(memory_space=pl.ANY)],
        out_specs=tuple([HBM] * (n + m)),
        input_output_aliases={i: i for i in range(n + m)},
        compiler_params=SPLIT_COPY,
    )(*srcs, *lands, send_sems, recv_sems, after)
    return list(outs[:n]), list(outs[n:n + m])


def _gather_direct_copies(received):
    def issue(ins, lnd, copy):
        x, y, c = _place()
        b = 2 * x + y
        chips = [(1 - x, y), (x, 1 - y), (1 - x, 1 - y)]
        cps = []
        for a in range(len(ins)):
            for j, (cx, cy) in enumerate(chips):
                slot = 2 * cx + cy if received else b
                cps.append(copy(4 * a + j, ins[a], lnd[a].at[slot], (cx, cy, c)))
            cps.append(copy(4 * a + 3, ins[a], lnd[a].at[b], (x, y, 1 - c)))
        return cps
    return issue


def _exchange_copies(received):
    def issue(ins, lnd, copy):
        x, y, c = _place()
        b = 2 * x + y
        chips = [(1 - x, y), (x, 1 - y), (1 - x, 1 - y)]
        cps = []
        for a in range(len(ins)):
            for j, (cx, cy) in enumerate(chips):
                slot = 2 * cx + cy if received else b
                cps.append(copy(3 * a + j, ins[a].at[2 * cx + cy], lnd[a].at[slot], (cx, cy, c)))
        return cps
    return issue


def sibling_join_halves(fs, *, name):
    n = len(fs)

    def body(*refs):
        ins, outs = refs[:n], refs[n:2 * n]
        send_sems, recv_sems = refs[2 * n:]
        x, y, c = _place()
        sends = []
        for a in range(n):
            rh = ins[a].shape[0] // 2
            mine = pl.ds(pl.multiple_of(c * rh, 8), rh)
            cp = pltpu.make_async_remote_copy(
                src_ref=ins[a].at[mine], dst_ref=outs[a].at[mine], send_sem=send_sems.at[a],
                recv_sem=recv_sems.at[a], device_id=(x, y, 1 - c), device_id_type=MESH)
            cp.start()
            sends.append(cp)
        for a in range(n):
            rh = ins[a].shape[0] // 2
            other = pl.ds(pl.multiple_of((1 - c) * rh, 8), rh)
            pltpu.make_async_remote_copy(
                src_ref=ins[a].at[other], dst_ref=outs[a].at[other], send_sem=send_sems.at[a],
                recv_sem=recv_sems.at[a], device_id=(x, y, c), device_id_type=MESH).wait_recv()
        for cp in sends:
            cp.wait_send()

    return pl.pallas_call(
        body, in_specs=[HBM] * n, out_specs=[HBM] * n,
        out_shape=[jax.ShapeDtypeStruct(f.shape, f.dtype) for f in fs],
        input_output_aliases={a: a for a in range(n)},
        scratch_shapes=[pltpu.SemaphoreType.DMA((n,)), pltpu.SemaphoreType.DMA((n,))],
        name=name,
    )(*fs)


def pair_add(g, t, core, *, name):
    _, R, C = g.shape
    rh = R // 2
    tr = _tile(rh, 256, 16)
    nh = rh // tr

    def body(c_ref, g_ref, t_ref, o_ref):
        o_ref[...] = (g_ref[...].astype(F32) + t_ref[...].astype(F32)).astype(BF16)

    return pl.pallas_call(
        body,
        grid_spec=pltpu.PrefetchScalarGridSpec(
            num_scalar_prefetch=1, grid=(4, nh),
            in_specs=[pl.BlockSpec((None, tr, C), lambda b, i, c_ref: (b, c_ref[0] * nh + i, 0)),
                      pl.BlockSpec((None, tr, C), lambda b, i, c_ref: (b, i, 0))],
            out_specs=pl.BlockSpec((None, tr, C), lambda b, i, c_ref: (b, i, 0))),
        out_shape=jax.ShapeDtypeStruct((4, rh, C), BF16), compiler_params=_cp(2), name=name,
    )(core, g, t)


def sum_slots(q, *, name):
    N, R, C = q.shape
    tr = _tile(R, 256, 16)

    def body(q_ref, o_ref):
        acc = q_ref[0].astype(F32)
        for s in range(1, N):
            acc = acc + q_ref[s].astype(F32)
        o_ref[...] = acc

    return pl.pallas_call(
        body, grid=(R // tr,), in_specs=[pl.BlockSpec((N, tr, C), lambda i: (0, i, 0))],
        out_specs=pl.BlockSpec((tr, C), lambda i: (i, 0)),
        out_shape=jax.ShapeDtypeStruct((R, C), F32), compiler_params=_cp(1), name=name,
    )(q)


def sum_own_and_received(p, q, place, *, name):
    _, rh, C = p.shape
    tr = _tile(rh, 256, 16)
    nh = rh // tr

    def body(s0, s1, s2, s3, cr, p_ref, q1_ref, q2_ref, q3_ref, o_ref):
        o_ref[...] = (p_ref[...].astype(F32) + q1_ref[...].astype(F32)
                      + q2_ref[...].astype(F32) + q3_ref[...].astype(F32))

    def slot(d):
        return pl.BlockSpec((None, tr, C), lambda i, *pc: (pc[d][0], i, 0))

    return pl.pallas_call(
        body,
        grid_spec=pltpu.PrefetchScalarGridSpec(
            num_scalar_prefetch=5, grid=(nh,),
            in_specs=[slot(0), slot(1), slot(2), slot(3)],
            out_specs=pl.BlockSpec((tr, C), lambda i, *pc: (pc[4][0] * nh + i, 0))),
        out_shape=jax.ShapeDtypeStruct((2 * rh, C), F32), compiler_params=_cp(1), name=name,
    )(*place, p, q, q, q)


def _adam_math(w, g, m, v):
    m = ADAM_B1 * m + (1.0 - ADAM_B1) * g
    v = ADAM_B2 * v + (1.0 - ADAM_B2) * (g * g)
    m_hat = m / (1.0 - ADAM_B1 ** ADAM_STEP)
    v_hat = v / (1.0 - ADAM_B2 ** ADAM_STEP)
    delta = -ADAM_LR * (m_hat / (jnp.sqrt(v_hat) + ADAM_EPS) + ADAM_WD * w)
    return delta, m, v


def adam_stacked(w, m, v, grads, *, name):
    L, R, C = w.shape
    tr = _tile(R, max(8, ADAM_BLOCK_BYTES // (4 * C)), 8)
    nr = R // tr

    def body(w_ref, m_ref, v_ref, *rest):
        g_refs, (go_ref, d_ref, mo_ref, vo_ref) = rest[:L], rest[L:]
        lid = pl.program_id(0)
        for l in range(L):
            @pl.when(lid == l)
            def _(l=l):
                g = g_refs[l][...]
                d, mn, vn = _adam_math(w_ref[...], g, m_ref[...], v_ref[...])
                go_ref[...] = g
                d_ref[...] = d
                mo_ref[...] = mn
                vo_ref[...] = vn

    st = pl.BlockSpec((None, tr, C), lambda l, i: (l, i, 0))
    g_specs = [pl.BlockSpec((tr, C), functools.partial(lambda l, i, ll: (jnp.where(l == ll, i, 0), 0), ll=ll))
               for ll in range(L)]
    return pl.pallas_call(
        body, grid=(L, nr), in_specs=[st, st, st] + g_specs, out_specs=[st] * 4,
        out_shape=[jax.ShapeDtypeStruct((L, R, C), F32)] * 4, compiler_params=_cp(2), name=name,
    )(w, m, v, *grads)


def adam_flat(w, g, m, v, *, name):
    R, C = w.shape
    tr = _tile(R, 512, 8)

    def body(w_ref, g_ref, m_ref, v_ref, d_ref, mo_ref, vo_ref):
        d, mn, vn = _adam_math(w_ref[...], g_ref[...], m_ref[...], v_ref[...])
        d_ref[...] = d
        mo_ref[...] = mn
        vo_ref[...] = vn

    row = pl.BlockSpec((tr, C), lambda i: (i, 0))
    return pl.pallas_call(
        body, grid=(R // tr,), in_specs=[row] * 4, out_specs=[row] * 3,
        out_shape=[jax.ShapeDtypeStruct((R, C), F32)] * 3, compiler_params=_cp(1), name=name,
    )(w, g, m, v)


PACK_ROWS = 64


def _pack(arrays):
    flat = jnp.concatenate([a.reshape(-1).astype(F32) for a in arrays])
    n = flat.shape[0]
    unit = PACK_ROWS * LANES
    pad = (-n) % unit
    return jnp.pad(flat, (0, pad)).reshape(-1, LANES)


def _unpack(buf, shapes):
    flat = buf.reshape(-1)
    out, o = [], 0
    for s in shapes:
        n = 1
        for d in s:
            n *= d
        out.append(flat[o:o + n].reshape(s))
        o += n
    return out


def kernel(x, norm_mix_pre, norm_mix_post, norm_ffn_pre, norm_ffn_post, w_in, conv_a_w, conv_b_w, conv_b_bias, ln_b_gain, ln_b_bias, pool_w, pool_scale, w_out, w_up, conv_ffn_w, conv_ffn_bias, w_down, loss_target, m_norm_mix_pre, m_norm_mix_post, m_norm_ffn_pre, m_norm_ffn_post, m_w_in, m_conv_a_w, m_conv_b_w, m_conv_b_bias, m_ln_b_gain, m_ln_b_bias, m_pool_w, m_pool_scale, m_w_out, m_w_up, m_conv_ffn_w, m_conv_ffn_bias, m_w_down, v_norm_mix_pre, v_norm_mix_post, v_norm_ffn_pre, v_norm_ffn_post, v_w_in, v_conv_a_w, v_conv_b_w, v_conv_b_bias, v_ln_b_gain, v_ln_b_bias, v_pool_w, v_pool_scale, v_w_out, v_w_up, v_conv_ffn_w, v_conv_ffn_bias, v_w_down):
    L = w_in.shape[0]
    S, D = x.shape[1], x.shape[2]
    WA, WB, WC = 4 * conv_a_w.shape[2], 4 * conv_b_w.shape[2], pool_scale.shape[1]
    DIN, DMIX, F2 = 4 * w_in.shape[2], 4 * w_out.shape[1], 4 * w_up.shape[2]
    F = F2 // 2
    NG = WC // LANES
    xi, yi, ci = _place()
    chip = 2 * xi + yi
    core = jnp.reshape(ci, (1,)).astype(jnp.int32)

    conv_shapes = [(L, K_A, WA // 4), (L, K_B, WB // 4), (L, K_F, F2 // 4)]
    conv_all = allgather8(_pack([conv_a_w, conv_b_w, conv_ffn_w]), name="gather_conv_taps")
    per_chip = [_unpack(conv_all[2 * b], conv_shapes) for b in range(4)]
    wa_full, wb_full, wf_full = [jnp.concatenate([per_chip[b][k] for b in range(4)], axis=2) for k in range(3)]
    pw_bf = pool_w.astype(BF16)

    def shards_of(l):
        return [w_in[l].astype(BF16), w_up[l].astype(BF16), w_out[l].astype(BF16), w_down[l].astype(BF16)]

    def assemble(g_in, g_up, g_out, g_down):
        return (jnp.concatenate([g_in[b] for b in range(4)], axis=1),
                jnp.concatenate([g_up[b] for b in range(4)], axis=1),
                g_out.reshape(DMIX, D), g_down.reshape(F, D))

    w_full = [assemble(*gather_chip_shards(shards_of(0), name="gather_layer_weights"))]

    def vec(a, l):
        return a[l].reshape(1, -1)

    x0 = x.reshape(S, D)
    h1 = norm_fwd(x0, vec(norm_mix_pre, 0), name="norm_first")
    saved = []
    for l in range(L):
        Win, Wup, Wout, Wdown = w_full[l]
        token = None
        if l + 1 < L:
            srcs = shards_of(l + 1)
            lands = [lax.empty((4,) + s.shape, s.dtype) for s in srcs]
            s_sems, r_sems, srcs, lands, token = _split_start(
                srcs, lands, 4 * len(srcs), _gather_direct_copies(False), name="gather_start_layer%d" % (l + 1))
        u = matmul(h1, Win, out_dtype=BF16, tm=512, tn=2176, tk=2048, j_outer=True, after=token, name="mm_in")
        ymix, cb = mixer_fwd(u, wa_full[l], wb_full[l], vec(conv_b_bias, l), vec(ln_b_gain, l), vec(ln_b_bias, l),
                             pw_bf[l], vec(pool_scale, l), name="mixer_fwd")
        y = matmul(ymix, Wout, out_dtype=F32, tm=512, tn=2048, tk=2048, name="mm_out")
        x1, h2 = resid_norm_fwd(x0, y, vec(norm_mix_post, l), vec(norm_ffn_pre, l), emit_h=True, name="resid_norm_mid")
        up = matmul(h2, Wup, out_dtype=BF16, tm=512, tn=2816, tk=2048, j_outer=True, name="mm_up")
        a, upc = ffn_fwd(up, wf_full[l], vec(conv_ffn_bias, l), name="ffn_fwd")
        f = matmul(a, Wdown, out_dtype=F32, tm=1024, tn=2048, tk=1408, name="mm_down")
        last = l == L - 1
        x2, h_next = resid_norm_fwd(x1, f, vec(norm_ffn_post, l), vec(norm_mix_pre, 0 if last else l + 1),
                                    emit_h=not last, name="resid_norm_last" if last else "resid_norm_end")
        saved.append((x0, h1, u, ymix, y, x1, h2, up, a, f, cb, upc))
        if l + 1 < L:
            w_full.append(assemble(*_split_wait(s_sems, r_sems, srcs, lands, x2, _gather_direct_copies(True),
                                                name="gather_wait_layer%d" % (l + 1))[1]))
        x0, h1 = x2, h_next

    dx, lsum = loss_head(x0, loss_target.reshape(S, D), name="loss_head")
    loss = lax.psum(lsum[0, 0] * (0.5 / D), ("x", "y", "c"))

    small = [None] * L
    big = [None] * L
    dt = dx
    _, df, _, dg4 = norm_bwd(dt, None, None, None, saved[L - 1][9], vec(norm_ffn_post, L - 1), name="norm_bwd_top")
    place = [jnp.reshape(v, (1,)).astype(jnp.int32) for v in
             (2 * xi + yi, 2 * (1 - xi) + yi, 2 * xi + (1 - yi), 2 * (1 - xi) + (1 - yi), ci)]

    def finish_exchange(pend, after):
        lp, s_sems, r_sems, ps, qs = pend
        ps, qs = _split_wait(s_sems, r_sems, ps, qs, after, _exchange_copies(True),
                             name="grad_exchange_wait_layer%d" % lp)
        fh = [sum_own_and_received(p, q, place, name="grad_sum_chips_%d" % k) for k, (p, q) in enumerate(zip(ps, qs))]
        big[lp] = sibling_join_halves(fh, name="grad_join_halves")

    pending, token = None, None
    for l in reversed(range(L)):
        Win, Wup, Wout, Wdown = w_full[l]
        x0, h1, u, ymix, y, x1, h2, up, a, f, cb, upc = saved[l]
        da = matmul(df, Wdown, tb=True, out_dtype=BF16, tm=512, tn=1408, tk=2048, j_outer=True, after=token,
                    name="mm_down_dx")
        g_down = matmul(a, df, ta=True, out_dtype=BF16, tm=1408, tn=2048, tk=512, name="mm_down_dw")
        dup, dwf, dbf = ffn_bwd(up, upc, da, wf_full[l], name="ffn_bwd")
        dh2 = matmul(dup, Wup, tb=True, out_dtype=F32, tm=1024, tn=2048, tk=1408, name="mm_up_dx")
        g_up = matmul(h2, dup, ta=True, out_dtype=BF16, tm=2048, tn=1408, tk=512, groups=4, name="mm_up_dw")
        dt, dy, dg3, dg2 = norm_bwd(dt, dh2, x1, vec(norm_ffn_pre, l), y, vec(norm_mix_post, l), name="norm_bwd_mid")
        dymix = matmul(dy, Wout, tb=True, out_dtype=BF16, tm=512, tn=2048, tk=2048, name="mm_out_dx")
        g_out = matmul(ymix, dy, ta=True, out_dtype=BF16, tm=2048, tn=1024, tk=512, name="mm_out_dw")
        du, dwa, dwb, dbb, dlg, dlb, dpw, dps = mixer_bwd(
            u, cb, dymix, wa_full[l], wb_full[l], vec(conv_b_bias, l), vec(ln_b_gain, l), vec(ln_b_bias, l),
            pw_bf[l], vec(pool_scale, l), name="mixer_bwd")
        dh1 = matmul(du, Win, tb=True, out_dtype=F32, tm=512, tn=2048, tk=2176, name="mm_in_dx")
        g_in = matmul(h1, du, ta=True, out_dtype=BF16, tm=1024, tn=2176, tk=512, name="mm_in_dw")
        dg4_here = dg4
        if l > 0:
            dt, df, dg1, dg4 = norm_bwd(dt, dh1, x0, vec(norm_mix_pre, l), saved[l - 1][9], vec(norm_ffn_post, l - 1),
                                        name="norm_bwd_end")
        else:
            dt, _, dg1, _ = norm_bwd(dt, dh1, x0, vec(norm_mix_pre, 0), None, None, name="norm_bwd_bottom")
        small[l] = dict(norm_mix_pre=dg1, norm_mix_post=dg2, norm_ffn_pre=dg3, norm_ffn_post=dg4_here,
                        conv_a_w=dwa, conv_b_w=dwb, conv_b_bias=dbb, ln_b_gain=dlg, ln_b_bias=dlb,
                        pool_w=dpw, pool_scale=dps, conv_ffn_w=dwf, conv_ffn_bias=dbf)

        gs = [g_in.reshape(D, 4, DIN // 4).transpose(1, 0, 2), g_up,
              g_out.reshape(4, DMIX // 4, D), g_down.reshape(4, F // 4, D)]
        ts = sibling_swap_halves(gs, name="grad_swap_halves")
        ps = [pair_add(g, t, core, name="grad_pair_add_%d" % k) for k, (g, t) in enumerate(zip(gs, ts))]
        if pending is not None:
            finish_exchange(pending, dt)
        qs = [lax.empty(p.shape, p.dtype) for p in ps]
        s_sems, r_sems, ps, qs, token = _split_start(ps, qs, 3 * len(ps), _exchange_copies(False),
                                                     name="grad_exchange_start_layer%d" % l)
        pending = (l, s_sems, r_sems, ps, qs)
    finish_exchange(pending, dt)
    grad_x = dt.reshape(1, S, D)

    rep_names = ["norm_mix_pre", "norm_mix_post", "norm_ffn_pre", "norm_ffn_post", "conv_b_bias", "ln_b_gain",
                 "ln_b_bias", "pool_w", "pool_scale", "conv_ffn_bias"]
    shd_names = ["conv_a_w", "conv_b_w", "conv_ffn_w"]
    given = dict(
        norm_mix_pre=(norm_mix_pre, m_norm_mix_pre, v_norm_mix_pre), norm_mix_post=(norm_mix_post, m_norm_mix_post, v_norm_mix_post),
        norm_ffn_pre=(norm_ffn_pre, m_norm_ffn_pre, v_norm_ffn_pre), norm_ffn_post=(norm_ffn_post, m_norm_ffn_post, v_norm_ffn_post),
        conv_b_bias=(conv_b_bias, m_conv_b_bias, v_conv_b_bias), ln_b_gain=(ln_b_gain, m_ln_b_gain, v_ln_b_gain),
        ln_b_bias=(ln_b_bias, m_ln_b_bias, v_ln_b_bias), pool_w=(pool_w, m_pool_w, v_pool_w),
        pool_scale=(pool_scale, m_pool_scale, v_pool_scale), conv_ffn_bias=(conv_ffn_bias, m_conv_ffn_bias, v_conv_ffn_bias),
        conv_a_w=(conv_a_w, m_conv_a_w, v_conv_a_w), conv_b_w=(conv_b_w, m_conv_b_w, v_conv_b_w),
        conv_ffn_w=(conv_ffn_w, m_conv_ffn_w, v_conv_ffn_w))
    full_shape = dict(conv_a_w=(L, K_A, WA), conv_b_w=(L, K_B, WB), conv_ffn_w=(L, K_F, F2))
    for nme in rep_names:
        full_shape[nme] = given[nme][0].shape
    names = rep_names + shd_names
    stacked = [jnp.stack([small[l][nme] for l in range(L)]).reshape(full_shape[nme]) for nme in names]
    parts = allgather8(_pack(stacked), name="gather_small_grads")
    totals = _unpack(sum_slots(parts, name="sum_small_grads"), [full_shape[nme] for nme in names])
    total = dict(zip(names, totals))
    for nme in shd_names:
        wd = full_shape[nme][2] // 4
        total[nme] = lax.dynamic_slice_in_dim(total[nme], chip * wd, wd, axis=2)
    shapes = [given[nme][0].shape for nme in names]
    d_s, m_s, v_s = adam_flat(_pack([given[nme][0] for nme in names]), _pack([total[nme] for nme in names]),
                              _pack([given[nme][1] for nme in names]), _pack([given[nme][2] for nme in names]),
                              name="adam_small")
    res = dict(zip(names, zip([total[nme] for nme in names], _unpack(d_s, shapes), _unpack(m_s, shapes),
                              _unpack(v_s, shapes))))

    for k, (nme, trio) in enumerate([("w_in", (w_in, m_w_in, v_w_in)), ("w_up", (w_up, m_w_up, v_w_up)),
                                     ("w_out", (w_out, m_w_out, v_w_out)), ("w_down", (w_down, m_w_down, v_w_down))]):
        res[nme] = adam_stacked(*trio, [big[l][k] for l in range(L)], name="adam_" + nme)

    order = ["norm_mix_pre", "norm_mix_post", "norm_ffn_pre", "norm_ffn_post", "w_in", "conv_a_w", "conv_b_w",
             "conv_b_bias", "ln_b_gain", "ln_b_bias", "pool_w", "pool_scale", "w_out", "w_up", "conv_ffn_w",
             "conv_ffn_bias", "w_down"]
    outs = [loss, grad_x]
    for k in range(4):
        outs += [res[nme][k] for nme in order]
    return tuple(outs)
```

```python
import functools

import jax
import jax.numpy as jnp
from jax import lax
from jax.experimental import pallas as pl
from jax.experimental.pallas import tpu as pltpu

F32 = jnp.float32
BF16 = jnp.bfloat16
MESH = pl.DeviceIdType.MESH

RMS_EPS = 1e-6
LN_EPS = 1e-5
ADAM_LR = 0.001
ADAM_B1 = 0.9
ADAM_B2 = 0.999
ADAM_EPS = 1e-08
ADAM_WD = 0.01
ADAM_STEP = 10
POOL_WINDOWS = (2, 4, 8, 16)
K_A = 3
K_B = 31
K_F = 3

LANES = 128
HALO_MIX = 32
HALO_FFN = 16
ROWS = 64
TM_MIXER = 512
TM_FFN = 256
TM_FFN_BWD = 128
TM_NORM = 256
ADAM_BLOCK_BYTES = 1 << 20
VMEM_LIMIT = 56 * 1024 * 1024


def _cp(n_axes):
    return pltpu.CompilerParams(dimension_semantics=("arbitrary",) * n_axes, vmem_limit_bytes=VMEM_LIMIT)


def _tile(dim, target, mult=LANES):
    if dim <= target:
        return dim
    t = (target // mult) * mult
    while t >= mult:
        if dim % t == 0:
            return t
        t -= mult
    return dim


def _chunks(n, rc=ROWS):
    out, r0 = [], 0
    while r0 < n:
        s = min(rc, n - r0)
        out.append((r0, s))
        r0 += s
    return out


def _sigmoid(x):
    return 1.0 / (1.0 + jnp.exp(-x))


def matmul(a, b, *, ta=False, tb=False, out_dtype, tm, tn, tk, j_outer=False, groups=1, after=None, name):
    if ta:
        K, M = a.shape
    else:
        M, K = a.shape
    if tb:
        N, K2 = b.shape
    else:
        K2, N = b.shape
    assert K == K2, (a.shape, b.shape)
    ng = N // groups
    tm, tn, tk = _tile(M, tm), _tile(ng, tn), _tile(K, tk)
    nm, nn, nk = M // tm, N // tn, K // tk
    per = ng // tn

    def ij(g0, g1):
        return (g1, g0) if j_outer else (g0, g1)

    def a_map(g0, g1, k):
        i, j = ij(g0, g1)
        return (k, i) if ta else (i, k)

    def b_map(g0, g1, k):
        i, j = ij(g0, g1)
        return (j, k) if tb else (k, j)

    def o_map(g0, g1, k):
        i, j = ij(g0, g1)
        return (j // per, i, j % per) if groups > 1 else (i, j)

    dims = (((0 if ta else 1,), (1 if tb else 0,)), ((), ()))
    use_acc = nk > 1 and out_dtype != F32

    def body(a_ref, b_ref, o_ref, *scratch):
        p = lax.dot_general(a_ref[...], b_ref[...], dims, preferred_element_type=F32)
        if nk == 1:
            o_ref[...] = p.astype(o_ref.dtype)
            return
        acc = scratch[0] if use_acc else o_ref
        k = pl.program_id(2)

        @pl.when(k == 0)
        def _():
            acc[...] = p

        @pl.when(k > 0)
        def _():
            acc[...] += p

        if use_acc:
            @pl.when(k == nk - 1)
            def _():
                o_ref[...] = acc[...].astype(o_ref.dtype)

    grid = (nn, nm, nk) if j_outer else (nm, nn, nk)
    if groups > 1:
        out_shape = jax.ShapeDtypeStruct((groups, M, ng), out_dtype)
        out_spec = pl.BlockSpec((None, tm, tn), o_map)
    else:
        out_shape = jax.ShapeDtypeStruct((M, N), out_dtype)
        out_spec = pl.BlockSpec((tm, tn), o_map)
    has_after = after is not None

    def body_after(a_ref, b_ref, after_ref, o_ref, *scratch):
        body(a_ref, b_ref, o_ref, *scratch)

    return pl.pallas_call(
        body_after if has_after else body, grid=grid,
        in_specs=[pl.BlockSpec((tk, tm) if ta else (tm, tk), a_map),
                  pl.BlockSpec((tn, tk) if tb else (tk, tn), b_map)]
                 + ([pl.BlockSpec(memory_space=pl.ANY)] if has_after else []),
        out_specs=out_spec, out_shape=out_shape,
        scratch_shapes=[pltpu.VMEM((tm, tn), F32)] if use_acc else [],
        compiler_params=_cp(3), name=name,
    )(*((a, b, after) if has_after else (a, b)))


def _rms(v):
    return lax.rsqrt(jnp.mean(v * v, axis=-1, keepdims=True) + RMS_EPS)


def norm_fwd(x, g, *, name):
    S, D = x.shape
    tm = _tile(S, TM_NORM, 16)

    def body(x_ref, g_ref, h_ref):
        v = x_ref[...]
        h_ref[...] = (v * _rms(v) * g_ref[...]).astype(BF16)

    return pl.pallas_call(
        body, grid=(S // tm,),
        in_specs=[pl.BlockSpec((tm, D), lambda i: (i, 0)), pl.BlockSpec((1, D), lambda i: (0, 0))],
        out_specs=pl.BlockSpec((tm, D), lambda i: (i, 0)),
        out_shape=jax.ShapeDtypeStruct((S, D), BF16), compiler_params=_cp(1), name=name,
    )(x, g)


def resid_norm_fwd(x, y, gp, gn, *, emit_h, name):
    S, D = x.shape
    tm = _tile(S, TM_NORM, 16)

    def body(x_ref, y_ref, gp_ref, gn_ref, xn_ref, *rest):
        yv = y_ref[...]
        xn = x_ref[...] + yv * _rms(yv) * gp_ref[...]
        xn_ref[...] = xn
        if emit_h:
            rest[0][...] = (xn * _rms(xn) * gn_ref[...]).astype(BF16)

    row = pl.BlockSpec((tm, D), lambda i: (i, 0))
    vec = pl.BlockSpec((1, D), lambda i: (0, 0))
    outs = pl.pallas_call(
        body, grid=(S // tm,), in_specs=[row, row, vec, vec],
        out_specs=[row, row] if emit_h else [row],
        out_shape=[jax.ShapeDtypeStruct((S, D), F32)] + ([jax.ShapeDtypeStruct((S, D), BF16)] if emit_h else []),
        compiler_params=_cp(1), name=name,
    )(x, y, gp, gn)
    return (outs[0], outs[1]) if emit_h else (outs[0], None)


def _rms_bwd(v, g, dout):
    r = _rms(v)
    gd = g * dout
    dv = r * gd - v * (r * r * r) * jnp.mean(v * gd, axis=-1, keepdims=True)
    return dv, dout * v * r


def norm_bwd(d_direct, dh, xn, gn, y, gp, *, name):
    S, D = d_direct.shape
    tm = _tile(S, TM_NORM, 16)
    has_h, has_y = dh is not None, y is not None

    def body(*refs):
        refs = list(refs)
        dd_ref = refs.pop(0)
        if has_h:
            dh_ref, xn_ref, gn_ref = refs.pop(0), refs.pop(0), refs.pop(0)
        if has_y:
            y_ref, gp_ref = refs.pop(0), refs.pop(0)
        if has_h:
            dt_ref = refs.pop(0)
        if has_y:
            dy_ref = refs.pop(0)
        if has_h:
            dgn_ref = refs.pop(0)
        if has_y:
            dgp_ref = refs.pop(0)
        i = pl.program_id(0)
        dt = dd_ref[...]
        if has_h:
            dv, gterm = _rms_bwd(xn_ref[...], gn_ref[...], dh_ref[...])
            dt = dt + dv
            dt_ref[...] = dt
            part = jnp.sum(gterm, axis=0, keepdims=True)

            @pl.when(i == 0)
            def _():
                dgn_ref[...] = part

            @pl.when(i > 0)
            def _():
                dgn_ref[...] += part
        if has_y:
            dy, gterm = _rms_bwd(y_ref[...], gp_ref[...], dt)
            dy_ref[...] = dy.astype(BF16)
            part2 = jnp.sum(gterm, axis=0, keepdims=True)

            @pl.when(i == 0)
            def _():
                dgp_ref[...] = part2

            @pl.when(i > 0)
            def _():
                dgp_ref[...] += part2

    row = pl.BlockSpec((tm, D), lambda i: (i, 0))
    vec = pl.BlockSpec((1, D), lambda i: (0, 0))
    ins, in_specs = [d_direct], [row]
    if has_h:
        ins += [dh, xn, gn]
        in_specs += [row, row, vec]
    if has_y:
        ins += [y, gp]
        in_specs += [row, vec]
    out_specs, out_shape = [], []
    if has_h:
        out_specs.append(row)
        out_shape.append(jax.ShapeDtypeStruct((S, D), F32))
    if has_y:
        out_specs.append(row)
        out_shape.append(jax.ShapeDtypeStruct((S, D), BF16))
    if has_h:
        out_specs.append(vec)
        out_shape.append(jax.ShapeDtypeStruct((1, D), F32))
    if has_y:
        out_specs.append(vec)
        out_shape.append(jax.ShapeDtypeStruct((1, D), F32))
    outs = list(pl.pallas_call(
        body, grid=(S // tm,), in_specs=in_specs, out_specs=out_specs, out_shape=out_shape,
        compiler_params=_cp(1), name=name,
    )(*ins))
    dt = outs.pop(0) if has_h else d_direct
    dy = outs.pop(0) if has_y else None
    dgn = outs.pop(0) if has_h else None
    dgp = outs.pop(0) if has_y else None
    return dt, dy, dgn, dgp


def loss_head(xl, target, *, name):
    S, D = xl.shape
    tm = _tile(S, TM_NORM, 16)

    def body(x_ref, t_ref, dx_ref, l_ref):
        i = pl.program_id(0)
        e = x_ref[...] - t_ref[...]
        dx_ref[...] = e * (1.0 / D)
        part = jnp.sum(e * e)

        @pl.when(i == 0)
        def _():
            l_ref[...] = jnp.zeros_like(l_ref) + part

        @pl.when(i > 0)
        def _():
            l_ref[...] += part

    row = pl.BlockSpec((tm, D), lambda i: (i, 0))
    return pl.pallas_call(
        body, grid=(S // tm,), in_specs=[row, row],
        out_specs=[row, pl.BlockSpec((8, LANES), lambda i: (0, 0))],
        out_shape=[jax.ShapeDtypeStruct((S, D), F32), jax.ShapeDtypeStruct((8, LANES), F32)],
        compiler_params=_cp(1), name=name,
    )(xl, target)


def _tap_sum(src, base, offs, wrows, r0, rc, c0):
    acc = None
    for k, off in enumerate(offs):
        t = src[base + r0 + off: base + r0 + off + rc, c0:c0 + LANES]
        if wrows is not None:
            t = t * wrows[k]
        acc = t if acc is None else acc + t
    return acc


def _tap_wgrad(out_ref, o0, a, a_base, b, b_base, offs, n, c0, first):
    for k, off in enumerate(offs):
        acc = None
        for r0, rc in _chunks(n):
            t = (a[a_base + r0: a_base + r0 + rc, c0:c0 + LANES]
                 * b[b_base + r0 + off: b_base + r0 + off + rc, c0:c0 + LANES])
            t = jnp.sum(t, axis=0, keepdims=True)
            acc = t if acc is None else acc + t
        _acc_store(out_ref, (slice(k, k + 1), slice(o0 + c0, o0 + c0 + LANES)), acc, first)


def _shift_copies(sh, src, n, shifts, c0):
    for b in shifts:
        for r0, rc in _chunks(n):
            sh[b, r0:r0 + rc, :] = src[r0 + b:r0 + b + rc, c0:c0 + LANES]


def _tap(sh, src, o, r0, rc, c0):
    a, b = divmod(o, 8)
    if b == 0:
        return src[r0 + o:r0 + o + rc, c0:c0 + LANES]
    return sh[b, r0 + 8 * a:r0 + 8 * a + rc, :]


def _acc_store(ref, idx, val, first):
    del first
    ref[idx] += val


def _zero_first(first, *refs):
    @pl.when(first)
    def _():
        for ref in refs:
            ref[...] = jnp.zeros_like(ref)


def _row_counts(t0, rc, w):
    t = t0 + lax.broadcasted_iota(jnp.int32, (rc, LANES), 0)
    return jnp.minimum(t + 1, w).astype(F32)


def mixer_fwd(u, wa, wb, bb, lg, lb, pw, ps, *, name):
    S, DIN = u.shape
    WA, WB, WC = wa.shape[1], wb.shape[1], ps.shape[1]
    DMIX = WA + WB + WC
    tm = _tile(S, TM_MIXER, HALO_MIX)
    HB = HALO_MIX
    r = tm // HB
    oCg, oVa, oVal, oGate, oC = WA, 2 * WA, 3 * WA, 3 * WA + WB, 3 * WA + 2 * WB
    offs_a = [k - (K_A - 1) for k in range(K_A)]
    offs_b = [k - (K_B - 1) for k in range(K_B)]

    def body(u_ref, uh_ref, wa_ref, wb_ref, bb_ref, lg_ref, lb_ref, pw_ref, ps_ref, y_ref, cbuf,
             pbuf, gbuf, shbuf, xbuf, plbuf):
        i = pl.program_id(0)
        hv = jnp.where(i > 0, 1.0, 0.0).astype(F32)

        def fill(src, dst0, n, scale):
            for r0, rc in _chunks(n):
                rows, drows = slice(r0, r0 + rc), slice(dst0 + r0, dst0 + r0 + rc)
                for c0 in range(0, WA, LANES):
                    v = (src[rows, oCg + c0:oCg + c0 + LANES].astype(F32)
                         * src[rows, oVa + c0:oVa + c0 + LANES].astype(F32))
                    pbuf[drows, c0:c0 + LANES] = v if scale is None else v * scale
                for c0 in range(0, WB, LANES):
                    v = (src[rows, oVal + c0:oVal + c0 + LANES].astype(F32)
                         * _sigmoid(src[rows, oGate + c0:oGate + c0 + LANES].astype(F32)))
                    gbuf[drows, c0:c0 + LANES] = v if scale is None else v * scale
                for c0 in range(0, WC, LANES):
                    v = src[rows, oC + c0:oC + c0 + LANES].astype(F32)
                    xbuf[drows, c0:c0 + LANES] = v if scale is None else v * scale

        fill(uh_ref, 0, HB, hv)
        fill(u_ref, HB, tm, None)

        for c0 in range(0, WA, LANES):
            w = [wa_ref[k:k + 1, c0:c0 + LANES] for k in range(K_A)]
            for r0, rc in _chunks(tm):
                q = _tap_sum(pbuf, HB, offs_a, w, r0, rc, c0)
                bg = u_ref[r0:r0 + rc, c0:c0 + LANES].astype(F32)
                y_ref[r0:r0 + rc, c0:c0 + LANES] = (bg * q).astype(BF16)

        for c0 in range(0, WB, LANES):
            w = [wb_ref[k:k + 1, c0:c0 + LANES] for k in range(K_B)]
            bias = bb_ref[:, c0:c0 + LANES]
            _shift_copies(shbuf, gbuf, tm + 24, range(1, 8), c0)
            for r0, rc in _chunks(tm):
                acc = bias
                for k in range(K_B):
                    acc = acc + _tap(shbuf, gbuf, HB - (K_B - 1) + k, r0, rc, c0) * w[k]
                cbuf[r0:r0 + rc, c0:c0 + LANES] = acc
        for r0, rc in _chunks(tm, 32):
            cb = cbuf[r0:r0 + rc, :]
            mu = jnp.mean(cb, axis=-1, keepdims=True)
            d = cb - mu
            n = d * lax.rsqrt(jnp.mean(d * d, axis=-1, keepdims=True) + LN_EPS)
            z = n * lg_ref[...] + lb_ref[...]
            y_ref[r0:r0 + rc, WA:WA + WB] = (z * _sigmoid(z)).astype(BF16)

        for g, win in enumerate(POOL_WINDOWS):
            c0 = g * LANES
            for r0, rc in _chunks(tm):
                s = _tap_sum(xbuf, HB, [-j for j in range(win)], None, r0, rc, c0)
                pooled = s / _row_counts(i * tm + r0, rc, win) - xbuf[HB + r0:HB + r0 + rc, c0:c0 + LANES]
                plbuf[r0:r0 + rc, c0:c0 + LANES] = pooled.astype(BF16)
            mixed = jnp.dot(plbuf[:, c0:c0 + LANES], pw_ref[g], preferred_element_type=F32)
            y_ref[:, WA + WB + c0:WA + WB + c0 + LANES] = (mixed * ps_ref[:, c0:c0 + LANES]).astype(BF16)

    full = lambda a: pl.BlockSpec(a.shape, lambda i: (0,) * a.ndim)
    return pl.pallas_call(
        body, grid=(S // tm,),
        in_specs=[pl.BlockSpec((tm, DIN), lambda i: (i, 0)),
                  pl.BlockSpec((HB, DIN), lambda i: (jnp.maximum(i * r - 1, 0), 0)),
                  full(wa), full(wb), full(bb), full(lg), full(lb), full(pw), full(ps)],
        out_specs=[pl.BlockSpec((tm, DMIX), lambda i: (i, 0)), pl.BlockSpec((tm, WB), lambda i: (i, 0))],
        out_shape=[jax.ShapeDtypeStruct((S, DMIX), BF16), jax.ShapeDtypeStruct((S, WB), F32)],
        scratch_shapes=[pltpu.VMEM((HB + tm, WA), F32), pltpu.VMEM((HB + tm, WB), F32),
                        pltpu.VMEM((8, tm + 24, LANES), F32), pltpu.VMEM((HB + tm, WC), F32),
                        pltpu.VMEM((tm, WC), BF16)],
        compiler_params=_cp(1), name=name,
    )(u, u, wa, wb, bb, lg, lb, pw, ps)


def mixer_bwd(u, cb, dy, wa, wb, bb, lg, lb, pw, ps, *, name):
    S, DIN = u.shape
    WA, WB, WC = wa.shape[1], wb.shape[1], ps.shape[1]
    NG = WC // LANES
    DMIX = WA + WB + WC
    tm = _tile(S, TM_MIXER, HALO_MIX)
    HB = HALO_MIX
    r = tm // HB
    nI = S // tm
    nH = S // HB
    oCg, oVa, oVal, oGate, oC = WA, 2 * WA, 3 * WA, 3 * WA + WB, 3 * WA + 2 * WB
    offs_a = [k - (K_A - 1) for k in range(K_A)]
    offs_b = [k - (K_B - 1) for k in range(K_B)]
    adj_a = [(K_A - 1) - k for k in range(K_A)]
    adj_b = [(K_B - 1) - k for k in range(K_B)]

    def body(u_ref, ub_ref, ua_ref, cb_ref, cba_ref, dy_ref, dya_ref,
             wa_ref, wb_ref, bb_ref, lg_ref, lb_ref, pw_ref, ps_ref,
             du_ref, dwa_ref, dwb_ref, dbb_ref, dlg_ref, dlb_ref, dpw_ref, dps_ref,
             pbuf, dqbuf, gbuf, dcbuf, shbuf, xbuf, plbuf, dmbuf, dplbuf, dpcbuf):
        i = pl.program_id(0)
        first = i == 0
        hvb = jnp.where(i > 0, 1.0, 0.0).astype(F32)
        hva = jnp.where(i < nI - 1, 1.0, 0.0).astype(F32)
        _zero_first(first, dwa_ref, dwb_ref, dbb_ref, dlg_ref, dlb_ref, dpw_ref, dps_ref)

        def fill(src, dst0, n, scale, main):
            for r0, rc in _chunks(n):
                rows, drows = slice(r0, r0 + rc), slice(dst0 + r0, dst0 + r0 + rc)
                for c0 in range(0, WA, LANES):
                    v = (src[rows, oCg + c0:oCg + c0 + LANES].astype(F32)
                         * src[rows, oVa + c0:oVa + c0 + LANES].astype(F32))
                    pbuf[drows, c0:c0 + LANES] = v if scale is None else v * scale
                for c0 in range(0, WC, LANES):
                    v = src[rows, oC + c0:oC + c0 + LANES].astype(F32)
                    xbuf[drows, c0:c0 + LANES] = v if scale is None else v * scale
                if main:
                    for c0 in range(0, WB, LANES):
                        gbuf[rows, c0:c0 + LANES] = (
                            src[rows, oVal + c0:oVal + c0 + LANES].astype(F32)
                            * _sigmoid(src[rows, oGate + c0:oGate + c0 + LANES].astype(F32)))

        fill(ub_ref, 0, HB, hvb, False)
        fill(u_ref, HB, tm, None, True)

        for r0, rc in _chunks(tm):
            for c0 in range(0, WA, LANES):
                dqbuf[r0:r0 + rc, c0:c0 + LANES] = (dy_ref[r0:r0 + rc, c0:c0 + LANES].astype(F32)
                                                     * u_ref[r0:r0 + rc, c0:c0 + LANES].astype(F32))
        for c0 in range(0, WA, LANES):
            dqbuf[tm:tm + HB, c0:c0 + LANES] = (dya_ref[:, c0:c0 + LANES].astype(F32)
                                                * ua_ref[:, c0:c0 + LANES].astype(F32)) * hva
        for c0 in range(0, WA, LANES):
            w = [wa_ref[k:k + 1, c0:c0 + LANES] for k in range(K_A)]
            for r0, rc in _chunks(tm):
                rows = slice(r0, r0 + rc)
                q = _tap_sum(pbuf, HB, offs_a, w, r0, rc, c0)
                du_ref[rows, c0:c0 + LANES] = (dy_ref[rows, c0:c0 + LANES].astype(F32) * q).astype(BF16)
                dp = _tap_sum(dqbuf, 0, adj_a, w, r0, rc, c0)
                cg = u_ref[rows, oCg + c0:oCg + c0 + LANES].astype(F32)
                va = u_ref[rows, oVa + c0:oVa + c0 + LANES].astype(F32)
                du_ref[rows, oCg + c0:oCg + c0 + LANES] = (dp * va).astype(BF16)
                du_ref[rows, oVa + c0:oVa + c0 + LANES] = (dp * cg).astype(BF16)
            _tap_wgrad(dwa_ref, 0, dqbuf, 0, pbuf, HB, offs_a, tm, c0, first)

        def ln_chunk(r0, rc, cb, dyb, scale, main):
            mu = jnp.mean(cb, axis=-1, keepdims=True)
            d = cb - mu
            rs = lax.rsqrt(jnp.mean(d * d, axis=-1, keepdims=True) + LN_EPS)
            n = d * rs
            z = n * lg_ref[...] + lb_ref[...]
            sg = _sigmoid(z)
            dz = dyb * (sg * (1.0 + z * (1.0 - sg)))
            dn = dz * lg_ref[...]
            dcb = rs * (dn - jnp.mean(dn, axis=-1, keepdims=True) - n * jnp.mean(dn * n, axis=-1, keepdims=True))
            if scale is not None:
                dcb = dcb * scale
            dcbuf[r0:r0 + rc, :] = dcb
            if main:
                return (jnp.sum(dz * n, axis=0, keepdims=True), jnp.sum(dz, axis=0, keepdims=True),
                        jnp.sum(dcb, axis=0, keepdims=True))
            return None

        sums = None
        for r0, rc in _chunks(tm, 32):
            part = ln_chunk(r0, rc, cb_ref[r0:r0 + rc, :], dy_ref[r0:r0 + rc, WA:WA + WB].astype(F32), None, True)
            sums = part if sums is None else tuple(a + b for a, b in zip(sums, part))
        ln_chunk(tm, HB, cba_ref[...], dya_ref[:, WA:WA + WB].astype(F32), hva, False)
        _acc_store(dlg_ref, (slice(None), slice(None)), sums[0], first)
        _acc_store(dlb_ref, (slice(None), slice(None)), sums[1], first)
        _acc_store(dbb_ref, (slice(None), slice(None)), sums[2], first)

        for c0 in range(0, WB, LANES):
            w = [wb_ref[k:k + 1, c0:c0 + LANES] for k in range(K_B)]
            _shift_copies(shbuf, dcbuf, tm + 24, range(1, 8), c0)
            for r0, rc in _chunks(tm):
                rows = slice(r0, r0 + rc)
                dglu = None
                for k in range(K_B):
                    t = _tap(shbuf, dcbuf, adj_b[k], r0, rc, c0) * w[k]
                    dglu = t if dglu is None else dglu + t
                val = u_ref[rows, oVal + c0:oVal + c0 + LANES].astype(F32)
                sg = _sigmoid(u_ref[rows, oGate + c0:oGate + c0 + LANES].astype(F32))
                du_ref[rows, oVal + c0:oVal + c0 + LANES] = (dglu * sg).astype(BF16)
                du_ref[rows, oGate + c0:oGate + c0 + LANES] = (dglu * val * sg * (1.0 - sg)).astype(BF16)
            for k in range(K_B):
                acc = None
                for r0, rc in _chunks(tm):
                    t = _tap(shbuf, dcbuf, adj_b[k], r0, rc, c0) * gbuf[r0:r0 + rc, c0:c0 + LANES]
                    acc = t if acc is None else acc + t
                _acc_store(dwb_ref, (slice(k, k + 1), slice(c0, c0 + LANES)),
                           jnp.sum(acc, axis=0, keepdims=True), first)

        for g, win in enumerate(POOL_WINDOWS):
            c0 = g * LANES
            cols = slice(c0, c0 + LANES)
            ycols = slice(WA + WB + c0, WA + WB + c0 + LANES)
            for r0, rc in _chunks(tm):
                s = _tap_sum(xbuf, HB, [-j for j in range(win)], None, r0, rc, c0)
                pooled = s / _row_counts(i * tm + r0, rc, win) - xbuf[HB + r0:HB + r0 + rc, cols]
                plbuf[r0:r0 + rc, cols] = pooled.astype(BF16)
            mixed = jnp.dot(plbuf[:, cols], pw_ref[g], preferred_element_type=F32)
            dyc = dy_ref[:, ycols].astype(F32)
            _acc_store(dps_ref, (slice(None), cols), jnp.sum(dyc * mixed, axis=0, keepdims=True), first)
            dmbuf[0:tm, :] = (dyc * ps_ref[:, cols]).astype(BF16)
            dmbuf[tm:tm + HB, :] = (dya_ref[:, ycols].astype(F32) * ps_ref[:, cols] * hva).astype(BF16)
            dpw = lax.dot_general(plbuf[:, cols], dmbuf[0:tm, :], (((0,), (0,)), ((), ())),
                                  preferred_element_type=F32)
            _acc_store(dpw_ref, (g, slice(None), slice(None)), dpw, first)
            dplbuf[...] = lax.dot_general(dmbuf[...], pw_ref[g], (((1,), (1,)), ((), ())),
                                          preferred_element_type=F32)
            for r0, rc in _chunks(tm + HB):
                dpcbuf[r0:r0 + rc, :] = dplbuf[r0:r0 + rc, :] / _row_counts(i * tm + r0, rc, win)
            for r0, rc in _chunks(tm):
                duc = _tap_sum(dpcbuf, 0, list(range(win)), None, r0, rc, 0) - dplbuf[r0:r0 + rc, :]
                du_ref[r0:r0 + rc, oC + c0:oC + c0 + LANES] = duc.astype(BF16)

    full = lambda a: pl.BlockSpec(a.shape, lambda i: (0,) * a.ndim)
    acc = lambda shape: pl.BlockSpec(shape, lambda i: (0,) * len(shape))
    small = [(K_A, WA), (K_B, WB), (1, WB), (1, WB), (1, WB), (NG, LANES, LANES), (1, WC)]
    outs = pl.pallas_call(
        body, grid=(nI,),
        in_specs=[pl.BlockSpec((tm, DIN), lambda i: (i, 0)),
                  pl.BlockSpec((HB, DIN), lambda i: (jnp.maximum(i * r - 1, 0), 0)),
                  pl.BlockSpec((HB, DIN), lambda i: (jnp.minimum((i + 1) * r, nH - 1), 0)),
                  pl.BlockSpec((tm, WB), lambda i: (i, 0)),
                  pl.BlockSpec((HB, WB), lambda i: (jnp.minimum((i + 1) * r, nH - 1), 0)),
                  pl.BlockSpec((tm, DMIX), lambda i: (i, 0)),
                  pl.BlockSpec((HB, DMIX), lambda i: (jnp.minimum((i + 1) * r, nH - 1), 0)),
                  full(wa), full(wb), full(bb), full(lg), full(lb), full(pw), full(ps)],
        out_specs=[pl.BlockSpec((tm, DIN), lambda i: (i, 0))] + [acc(s) for s in small],
        out_shape=[jax.ShapeDtypeStruct((S, DIN), BF16)] + [jax.ShapeDtypeStruct(s, F32) for s in small],
        scratch_shapes=[pltpu.VMEM((HB + tm, WA), F32), pltpu.VMEM((tm + HB, WA), F32),
                        pltpu.VMEM((tm, WB), F32), pltpu.VMEM((tm + HB, WB), F32),
                        pltpu.VMEM((8, tm + 24, LANES), F32), pltpu.VMEM((HB + tm, WC), F32),
                        pltpu.VMEM((tm, WC), BF16), pltpu.VMEM((tm + HB, LANES), BF16),
                        pltpu.VMEM((tm + HB, LANES), F32), pltpu.VMEM((tm + HB, LANES), F32)],
        compiler_params=_cp(1), name=name,
    )(u, u, u, cb, cb, dy, dy, wa, wb, bb, lg, lb, pw, ps)
    return outs


def ffn_fwd(up, wf, bf, *, name):
    S, F2 = up.shape
    F = F2 // 2
    tm = _tile(S, TM_FFN, HALO_FFN)
    HB = HALO_FFN
    r = tm // HB
    CW = _tile(F, 512)
    offs = [k - (K_F - 1) for k in range(K_F)]

    def body(up_ref, uph_ref, wf_ref, bf_ref, a_ref, upc_ref, ebuf):
        i = pl.program_id(0)
        hv = jnp.where(i > 0, 1.0, 0.0).astype(F32)
        for c0 in range(0, F, CW):
            for h, off in ((0, c0), (1, F + c0)):
                ebuf[h, 0:HB, :] = uph_ref[:, off:off + CW].astype(F32) * hv
                for r0, rc in _chunks(tm):
                    ebuf[h, HB + r0:HB + r0 + rc, :] = up_ref[r0:r0 + rc, off:off + CW].astype(F32)
            for l0 in range(0, CW, LANES):
                cg, cv = c0 + l0, F + c0 + l0
                wg = [wf_ref[k:k + 1, cg:cg + LANES] for k in range(K_F)]
                wv = [wf_ref[k:k + 1, cv:cv + LANES] for k in range(K_F)]
                bg = bf_ref[:, cg:cg + LANES]
                bv = bf_ref[:, cv:cv + LANES]
                for r0, rc in _chunks(tm):
                    gt = _tap_sum(ebuf.at[0], HB, offs, wg, r0, rc, l0) + bg
                    vl = _tap_sum(ebuf.at[1], HB, offs, wv, r0, rc, l0) + bv
                    a_ref[r0:r0 + rc, cg:cg + LANES] = (gt * _sigmoid(gt) * vl).astype(BF16)
                    upc_ref[r0:r0 + rc, cg:cg + LANES] = gt.astype(BF16)
                    upc_ref[r0:r0 + rc, cv:cv + LANES] = vl.astype(BF16)

    full = lambda a: pl.BlockSpec(a.shape, lambda i: (0,) * a.ndim)
    return pl.pallas_call(
        body, grid=(S // tm,),
        in_specs=[pl.BlockSpec((tm, F2), lambda i: (i, 0)),
                  pl.BlockSpec((HB, F2), lambda i: (jnp.maximum(i * r - 1, 0), 0)),
                  full(wf), full(bf)],
        out_specs=[pl.BlockSpec((tm, F), lambda i: (i, 0)), pl.BlockSpec((tm, F2), lambda i: (i, 0))],
        out_shape=[jax.ShapeDtypeStruct((S, F), BF16), jax.ShapeDtypeStruct((S, F2), BF16)],
        scratch_shapes=[pltpu.VMEM((2, HB + tm, CW), F32)],
        compiler_params=_cp(1), name=name,
    )(up, up, wf, bf)


def ffn_bwd(up, upc, da, wf, *, name):
    S, F2 = up.shape
    F = F2 // 2
    tm = _tile(S, TM_FFN_BWD, HALO_FFN)
    HB = HALO_FFN
    r = tm // HB
    nI = S // tm
    nH = S // HB
    CW = _tile(F, 512)
    adj = [(K_F - 1) - k for k in range(K_F)]

    def body(up_ref, upc_ref, upca_ref, da_ref, daa_ref, wf_ref, dup_ref, dwf_ref, dbf_ref, dbuf, shbuf):
        i = pl.program_id(0)
        first = i == 0
        hva = jnp.where(i < nI - 1, 1.0, 0.0).astype(F32)
        _zero_first(first, dwf_ref, dbf_ref)
        for c0 in range(0, F, CW):
            for l0 in range(0, CW, LANES):
                cg, cv = c0 + l0, F + c0 + l0
                wg = [wf_ref[k:k + 1, cg:cg + LANES] for k in range(K_F)]
                wv = [wf_ref[k:k + 1, cv:cv + LANES] for k in range(K_F)]
                sg_sum, sv_sum = None, None
                for r0, rc in _chunks(tm + HB):
                    if r0 < tm:
                        gt = upc_ref[r0:r0 + rc, cg:cg + LANES].astype(F32)
                        vl = upc_ref[r0:r0 + rc, cv:cv + LANES].astype(F32)
                        d = da_ref[r0:r0 + rc, cg:cg + LANES].astype(F32)
                    else:
                        gt = upca_ref[:, cg:cg + LANES].astype(F32)
                        vl = upca_ref[:, cv:cv + LANES].astype(F32)
                        d = daa_ref[:, cg:cg + LANES].astype(F32) * hva
                    s = _sigmoid(gt)
                    dg = d * vl * (s * (1.0 + gt * (1.0 - s)))
                    dv = d * (gt * s)
                    dbuf[0, r0:r0 + rc, :] = dg
                    dbuf[1, r0:r0 + rc, :] = dv
                    if r0 < tm:
                        pg, pv = jnp.sum(dg, axis=0, keepdims=True), jnp.sum(dv, axis=0, keepdims=True)
                        sg_sum = pg if sg_sum is None else sg_sum + pg
                        sv_sum = pv if sv_sum is None else sv_sum + pv
                _acc_store(dbf_ref, (slice(None), slice(cg, cg + LANES)), sg_sum, first)
                _acc_store(dbf_ref, (slice(None), slice(cv, cv + LANES)), sv_sum, first)
                for h, w, col in ((0, wg, cg), (1, wv, cv)):
                    d_h, sh_h = dbuf.at[h], shbuf.at[h]
                    _shift_copies(sh_h, d_h, tm, (1, 2), 0)
                    accs = [None] * K_F
                    for r0, rc in _chunks(tm):
                        taps = [_tap(sh_h, d_h, adj[k], r0, rc, 0) for k in range(K_F)]
                        dup_ref[r0:r0 + rc, col:col + LANES] = (
                            taps[0] * w[0] + taps[1] * w[1] + taps[2] * w[2]).astype(BF16)
                        uv = up_ref[r0:r0 + rc, col:col + LANES].astype(F32)
                        for k in range(K_F):
                            t = taps[k] * uv
                            accs[k] = t if accs[k] is None else accs[k] + t
                    for k in range(K_F):
                        _acc_store(dwf_ref, (slice(k, k + 1), slice(col, col + LANES)),
                                   jnp.sum(accs[k], axis=0, keepdims=True), first)

    full = lambda a: pl.BlockSpec(a.shape, lambda i: (0,) * a.ndim)
    return pl.pallas_call(
        body, grid=(nI,),
        in_specs=[pl.BlockSpec((tm, F2), lambda i: (i, 0)),
                  pl.BlockSpec((tm, F2), lambda i: (i, 0)),
                  pl.BlockSpec((HB, F2), lambda i: (jnp.minimum((i + 1) * r, nH - 1), 0)),
                  pl.BlockSpec((tm, F), lambda i: (i, 0)),
                  pl.BlockSpec((HB, F), lambda i: (jnp.minimum((i + 1) * r, nH - 1), 0)),
                  full(wf)],
        out_specs=[pl.BlockSpec((tm, F2), lambda i: (i, 0)),
                   pl.BlockSpec((K_F, F2), lambda i: (0, 0)), pl.BlockSpec((1, F2), lambda i: (0, 0))],
        out_shape=[jax.ShapeDtypeStruct((S, F2), BF16), jax.ShapeDtypeStruct((K_F, F2), F32),
                   jax.ShapeDtypeStruct((1, F2), F32)],
        scratch_shapes=[pltpu.VMEM((2, tm + HB, LANES), F32), pltpu.VMEM((2, 3, tm, LANES), F32)],
        compiler_params=_cp(1), name=name,
    )(up, upc, upc, da, da, wf)


HBM = pl.BlockSpec(memory_space=pltpu.HBM)


def _place():
    return lax.axis_index("x"), lax.axis_index("y"), lax.axis_index("c")


def allgather8(buf, *, name):
    R, C = buf.shape

    def body(x_ref, o_ref, send_sems, recv_sems, local_sem):
        x, y, c = _place()
        me = 4 * x + 2 * y + c
        mine = pltpu.make_async_copy(x_ref, o_ref.at[me], local_sem)
        mine.start()
        sends = []
        for k in range(1, 8):
            fx, fy, fc = (k >> 2) & 1, (k >> 1) & 1, k & 1
            px, py, pc = (x + fx) % 2, (y + fy) % 2, (c + fc) % 2
            cp = pltpu.make_async_remote_copy(
                src_ref=x_ref, dst_ref=o_ref.at[me], send_sem=send_sems.at[k - 1], recv_sem=recv_sems.at[k - 1],
                device_id=(px, py, pc), device_id_type=MESH)
            cp.start()
            sends.append(cp)
        for k in range(1, 8):
            fx, fy, fc = (k >> 2) & 1, (k >> 1) & 1, k & 1
            peer = 4 * ((x + fx) % 2) + 2 * ((y + fy) % 2) + (c + fc) % 2
            pltpu.make_async_remote_copy(
                src_ref=x_ref, dst_ref=o_ref.at[peer], send_sem=send_sems.at[k - 1], recv_sem=recv_sems.at[k - 1],
                device_id=(x, y, c), device_id_type=MESH).wait_recv()
        for cp in sends:
            cp.wait_send()
        mine.wait()

    return pl.pallas_call(
        body, in_specs=[HBM], out_specs=HBM, out_shape=jax.ShapeDtypeStruct((8, R, C), buf.dtype),
        scratch_shapes=[pltpu.SemaphoreType.DMA((7,)), pltpu.SemaphoreType.DMA((7,)), pltpu.SemaphoreType.DMA],
        name=name,
    )(buf)


def gather_chip_shards(shards, *, name):
    n = len(shards)

    def body(*refs):
        ins, outs = refs[:n], refs[n:2 * n]
        send_sems, recv_sems = refs[2 * n:]
        x, y, c = _place()
        b = 2 * x + y
        chips = [(1 - x, y), (x, 1 - y), (1 - x, 1 - y)]
        sends = []

        def half(a, which):
            rh = ins[a].shape[0] // 2
            return pl.ds(pl.multiple_of(which * rh, 16), rh)

        def copy(a, k, src, dst, to):
            return pltpu.make_async_remote_copy(
                src_ref=src, dst_ref=dst, send_sem=send_sems.at[7 * a + k], recv_sem=recv_sems.at[7 * a + k],
                device_id=to, device_id_type=MESH)

        for a in range(n):
            for j, (cx, cy) in enumerate(chips):
                cp = copy(a, j, ins[a].at[half(a, c)], outs[a].at[b, half(a, c)], (cx, cy, c))
                cp.start()
                sends.append(cp)
        for a in range(n):
            cp = copy(a, 6, ins[a], outs[a].at[b], (x, y, 1 - c))
            cp.start()
            sends.append(cp)
        for a in range(n):
            for j, (cx, cy) in enumerate(chips):
                got = outs[a].at[2 * cx + cy, half(a, c)]
                copy(a, j, got, got, (x, y, c)).wait_recv()
                cp = copy(a, 3 + j, got, got, (x, y, 1 - c))
                cp.start()
                sends.append(cp)
        for a in range(n):
            for j, (cx, cy) in enumerate(chips):
                got = outs[a].at[2 * cx + cy, half(a, 1 - c)]
                copy(a, 3 + j, got, got, (x, y, c)).wait_recv()
            copy(a, 6, ins[a], outs[a].at[b], (x, y, c)).wait_recv()
        for cp in sends:
            cp.wait_send()

    return pl.pallas_call(
        body, in_specs=[HBM] * n, out_specs=[HBM] * n,
        out_shape=[jax.ShapeDtypeStruct((4,) + s.shape, s.dtype) for s in shards],
        scratch_shapes=[pltpu.SemaphoreType.DMA((7 * n,)), pltpu.SemaphoreType.DMA((7 * n,))],
        name=name,
    )(*shards)


def sibling_swap_halves(gs, *, name):
    n = len(gs)

    def body(*refs):
        ins, outs = refs[:n], refs[n:2 * n]
        send_sems, recv_sems = refs[2 * n:]
        x, y, c = _place()
        cps = []
        for a in range(n):
            rh = ins[a].shape[1] // 2
            src = ins[a].at[:, pl.ds(pl.multiple_of((1 - c) * rh, 16), rh)]
            cp = pltpu.make_async_remote_copy(
                src_ref=src, dst_ref=outs[a], send_sem=send_sems.at[a], recv_sem=recv_sems.at[a],
                device_id=(x, y, 1 - c), device_id_type=MESH)
            cp.start()
            cps.append(cp)
        for cp in cps:
            cp.wait()

    return pl.pallas_call(
        body, in_specs=[HBM] * n, out_specs=[HBM] * n,
        out_shape=[jax.ShapeDtypeStruct((4, g.shape[1] // 2, g.shape[2]), g.dtype) for g in gs],
        scratch_shapes=[pltpu.SemaphoreType.DMA((n,)), pltpu.SemaphoreType.DMA((n,))],
        name=name,
    )(*gs)


SEM =pl.BlockSpec(memory_space=pltpu.SEMAPHORE)
SPLIT_COPY = pltpu.CompilerParams(has_side_effects=pltpu.SideEffectType.DATAFLOW_SIDE_EFFECTING)


def _split_start(srcs, lands, n_copies, issue, *, name):
    n, m = len(srcs), len(lands)

    def body(*refs):
        ins, lnd = refs[:n], refs[n:n + m]
        send_sems, recv_sems = refs[n + m], refs[n + m + 1]
        token = refs[-1]

        def copy(k, src, dst, to):
            return pltpu.make_async_remote_copy(src_ref=src, dst_ref=dst, send_sem=send_sems.at[k],
                                                recv_sem=recv_sems.at[k], device_id=to, device_id_type=MESH)

        for cp in issue(ins, lnd, copy):
            cp.start()
        token[...] = jnp.zeros_like(token)

    outs = pl.pallas_call(
        body, name=name,
        out_shape=(pltpu.SemaphoreType.DMA((n_copies,)), pltpu.SemaphoreType.DMA((n_copies,)),
                   *[pltpu.HBM(a.shape, a.dtype) for a in list(srcs) + list(lands)],
                   jax.ShapeDtypeStruct((8, LANES), F32)),
        in_specs=[HBM] * (n + m),
        out_specs=(SEM, SEM, *([HBM] * (n + m)), pl.BlockSpec(memory_space=pltpu.VMEM)),
        input_output_aliases={i: 2 + i for i in range(n + m)},
        compiler_params=SPLIT_COPY,
    )(*[pltpu.with_memory_space_constraint(a, pltpu.HBM) for a in list(srcs) + list(lands)])
    return outs[0], outs[1], list(outs[2:2 + n]), list(outs[2 + n:2 + n + m]), outs[-1]


def _split_wait(send_sems, recv_sems, srcs, lands, after, issue, *, name):
    n, m = len(srcs), len(lands)

    def body(*refs):
        ins, lnd = refs[:n], refs[n:n + m]
        s_sems, r_sems = refs[n + m], refs[n + m + 1]

        def copy(k, src, dst, to):
            return pltpu.make_async_remote_copy(src_ref=src, dst_ref=dst, send_sem=s_sems.at[k],
                                                recv_sem=r_sems.at[k], device_id=to, device_id_type=MESH)

        for cp in issue(ins, lnd, copy):
            cp.wait_send()
            cp.wait_recv()

    outs = pl.pallas_call(
        body, name=name,
        out_shape=tuple(pltpu.HBM(a.shape, a.dtype) for a in list(srcs) + list(lands)),
        in_specs=[HBM] * (n + m) + [SEM, SEM, pl.BlockSpec(memory_space=pl.ANY)],
        out_specs=tuple([HBM] * (n + m)),
        input_output_aliases={i: i for i in range(n + m)},
        compiler_params=SPLIT_COPY,
    )(*srcs, *lands, send_sems, recv_sems, after)
    return list(outs[:n]), list(outs[n:n + m])


def _gather_direct_copies(received):
    def issue(ins, lnd, copy):
        x, y, c = _place()
        b = 2 * x + y
        chips = [(1 - x, y), (x, 1 - y), (1 - x, 1 - y)]
        cps = []
        for a in range(len(ins)):
            for j, (cx, cy) in enumerate(chips):
                slot = 2 * cx + cy if received else b
                cps.append(copy(4 * a + j, ins[a], lnd[a].at[slot], (cx, cy, c)))
            cps.append(copy(4 * a + 3, ins[a], lnd[a].at[b], (x, y, 1 - c)))
        return cps
    return issue


def _exchange_copies(received):
    def issue(ins, lnd, copy):
        x, y, c = _place()
        b = 2 * x + y
        chips = [(1 - x, y), (x, 1 - y), (1 - x, 1 - y)]
        cps = []
        for a in range(len(ins)):
            for j, (cx, cy) in enumerate(chips):
                slot = 2 * cx + cy if received else b
                cps.append(copy(3 * a + j, ins[a].at[2 * cx + cy], lnd[a].at[slot], (cx, cy, c)))
        return cps
    return issue


def sibling_join_halves(fs, *, name):
    n = len(fs)

    def body(*refs):
        ins, outs = refs[:n], refs[n:2 * n]
        send_sems, recv_sems = refs[2 * n:]
        x, y, c = _place()
        sends = []
        for a in range(n):
            rh = ins[a].shape[0] // 2
            mine = pl.ds(pl.multiple_of(c * rh, 8), rh)
            cp = pltpu.make_async_remote_copy(
                src_ref=ins[a].at[mine], dst_ref=outs[a].at[mine], send_sem=send_sems.at[a],
                recv_sem=recv_sems.at[a], device_id=(x, y, 1 - c), device_id_type=MESH)
            cp.start()
            sends.append(cp)
        for a in range(n):
            rh = ins[a].shape[0] // 2
            other = pl.ds(pl.multiple_of((1 - c) * rh, 8), rh)
            pltpu.make_async_remote_copy(
                src_ref=ins[a].at[other], dst_ref=outs[a].at[other], send_sem=send_sems.at[a],
                recv_sem=recv_sems.at[a], device_id=(x, y, c), device_id_type=MESH).wait_recv()
        for cp in sends:
            cp.wait_send()

    return pl.pallas_call(
        body, in_specs=[HBM] * n, out_specs=[HBM] * n,
        out_shape=[jax.ShapeDtypeStruct(f.shape, f.dtype) for f in fs],
        input_output_aliases={a: a for a in range(n)},
        scratch_shapes=[pltpu.SemaphoreType.DMA((n,)), pltpu.SemaphoreType.DMA((n,))],
        name=name,
    )(*fs)


def pair_add(g, t, core, *, name):
    _, R, C = g.shape
    rh = R // 2
    tr = _tile(rh, 256, 16)
    nh = rh // tr

    def body(c_ref, g_ref, t_ref, o_ref):
        o_ref[...] = (g_ref[...].astype(F32) + t_ref[...].astype(F32)).astype(BF16)

    return pl.pallas_call(
        body,
        grid_spec=pltpu.PrefetchScalarGridSpec(
            num_scalar_prefetch=1, grid=(4, nh),
            in_specs=[pl.BlockSpec((None, tr, C), lambda b, i, c_ref: (b, c_ref[0] * nh + i, 0)),
                      pl.BlockSpec((None, tr, C), lambda b, i, c_ref: (b, i, 0))],
            out_specs=pl.BlockSpec((None, tr, C), lambda b, i, c_ref: (b, i, 0))),
        out_shape=jax.ShapeDtypeStruct((4, rh, C), BF16), compiler_params=_cp(2), name=name,
    )(core, g, t)


def sum_slots(q, *, name):
    N, R, C = q.shape
    tr = _tile(R, 256, 16)

    def body(q_ref, o_ref):
        acc = q_ref[0].astype(F32)
        for s in range(1, N):
            acc = acc + q_ref[s].astype(F32)
        o_ref[...] = acc

    return pl.pallas_call(
        body, grid=(R // tr,), in_specs=[pl.BlockSpec((N, tr, C), lambda i: (0, i, 0))],
        out_specs=pl.BlockSpec((tr, C), lambda i: (i, 0)),
        out_shape=jax.ShapeDtypeStruct((R, C), F32), compiler_params=_cp(1), name=name,
    )(q)


def sum_own_and_received(p, q, place, *, name):
    _, rh, C = p.shape
    tr = _tile(rh, 256, 16)
    nh = rh // tr

    def body(s0, s1, s2, s3, cr, p_ref, q1_ref, q2_ref, q3_ref, o_ref):
        o_ref[...] = (p_ref[...].astype(F32) + q1_ref[...].astype(F32)
                      + q2_ref[...].astype(F32) + q3_ref[...].astype(F32))

    def slot(d):
        return pl.BlockSpec((None, tr, C), lambda i, *pc: (pc[d][0], i, 0))

    return pl.pallas_call(
        body,
        grid_spec=pltpu.PrefetchScalarGridSpec(
            num_scalar_prefetch=5, grid=(nh,),
            in_specs=[slot(0), slot(1), slot(2), slot(3)],
            out_specs=pl.BlockSpec((tr, C), lambda i, *pc: (pc[4][0] * nh + i, 0))),
        out_shape=jax.ShapeDtypeStruct((2 * rh, C), F32), compiler_params=_cp(1), name=name,
    )(*place, p, q, q, q)


def _adam_math(w, g, m, v):
    m = ADAM_B1 * m + (1.0 - ADAM_B1) * g
    v = ADAM_B2 * v + (1.0 - ADAM_B2) * (g * g)
    m_hat = m / (1.0 - ADAM_B1 ** ADAM_STEP)
    v_hat = v / (1.0 - ADAM_B2 ** ADAM_STEP)
    delta = -ADAM_LR * (m_hat / (jnp.sqrt(v_hat) + ADAM_EPS) + ADAM_WD * w)
    return delta, m, v


def adam_stacked(w, m, v, grads, *, name):
    L, R, C = w.shape
    tr = _tile(R, max(8, ADAM_BLOCK_BYTES // (4 * C)), 8)
    nr = R // tr

    def body(w_ref, m_ref, v_ref, *rest):
        g_refs, (go_ref, d_ref, mo_ref, vo_ref) = rest[:L], rest[L:]
        lid = pl.program_id(0)
        for l in range(L):
            @pl.when(lid == l)
            def _(l=l):
                g = g_refs[l][...]
                d, mn, vn = _adam_math(w_ref[...], g, m_ref[...], v_ref[...])
                go_ref[...] = g
                d_ref[...] = d
                mo_ref[...] = mn
                vo_ref[...] = vn

    st = pl.BlockSpec((None, tr, C), lambda l, i: (l, i, 0))
    g_specs = [pl.BlockSpec((tr, C), functools.partial(lambda l, i, ll: (jnp.where(l == ll, i, 0), 0), ll=ll))
               for ll in range(L)]
    return pl.pallas_call(
        body, grid=(L, nr), in_specs=[st, st, st] + g_specs, out_specs=[st] * 4,
        out_shape=[jax.ShapeDtypeStruct((L, R, C), F32)] * 4, compiler_params=_cp(2), name=name,
    )(w, m, v, *grads)


def adam_flat(w, g, m, v, *, name):
    R, C = w.shape
    tr = _tile(R, 512, 8)

    def body(w_ref, g_ref, m_ref, v_ref, d_ref, mo_ref, vo_ref):
        d, mn, vn = _adam_math(w_ref[...], g_ref[...], m_ref[...], v_ref[...])
        d_ref[...] = d
        mo_ref[...] = mn
        vo_ref[...] = vn

    row = pl.BlockSpec((tr, C), lambda i: (i, 0))
    return pl.pallas_call(
        body, grid=(R // tr,), in_specs=[row] * 4, out_specs=[row] * 3,
        out_shape=[jax.ShapeDtypeStruct((R, C), F32)] * 3, compiler_params=_cp(1), name=name,
    )(w, g, m, v)


PACK_ROWS = 64


def _pack(arrays):
    flat = jnp.concatenate([a.reshape(-1).astype(F32) for a in arrays])
    n = flat.shape[0]
    unit = PACK_ROWS * LANES
    pad = (-n) % unit
    return jnp.pad(flat, (0, pad)).reshape(-1, LANES)


def _unpack(buf, shapes):
    flat = buf.reshape(-1)
    out, o = [], 0
    for s in shapes:
        n = 1
        for d in s:
            n *= d
        out.append(flat[o:o + n].reshape(s))
        o += n
    return out


def kernel(x, norm_mix_pre, norm_mix_post, norm_ffn_pre, norm_ffn_post, w_in, conv_a_w, conv_b_w, conv_b_bias, ln_b_gain, ln_b_bias, pool_w, pool_scale, w_out, w_up, conv_ffn_w, conv_ffn_bias, w_down, loss_target, m_norm_mix_pre, m_norm_mix_post, m_norm_ffn_pre, m_norm_ffn_post, m_w_in, m_conv_a_w, m_conv_b_w, m_conv_b_bias, m_ln_b_gain, m_ln_b_bias, m_pool_w, m_pool_scale, m_w_out, m_w_up, m_conv_ffn_w, m_conv_ffn_bias, m_w_down, v_norm_mix_pre, v_norm_mix_post, v_norm_ffn_pre, v_norm_ffn_post, v_w_in, v_conv_a_w, v_conv_b_w, v_conv_b_bias, v_ln_b_gain, v_ln_b_bias, v_pool_w, v_pool_scale, v_w_out, v_w_up, v_conv_ffn_w, v_conv_ffn_bias, v_w_down):
    L = w_in.shape[0]
    S, D = x.shape[1], x.shape[2]
    WA, WB, WC = 4 * conv_a_w.shape[2], 4 * conv_b_w.shape[2], pool_scale.shape[1]
    DIN, DMIX, F2 = 4 * w_in.shape[2], 4 * w_out.shape[1], 4 * w_up.shape[2]
    F = F2 // 2
    NG = WC // LANES
    xi, yi, ci = _place()
    chip = 2 * xi + yi
    core = jnp.reshape(ci, (1,)).astype(jnp.int32)

    conv_shapes = [(L, K_A, WA // 4), (L, K_B, WB // 4), (L, K_F, F2 // 4)]
    conv_all = allgather8(_pack([conv_a_w, conv_b_w, conv_ffn_w]), name="gather_conv_taps")
    per_chip = [_unpack(conv_all[2 * b], conv_shapes) for b in range(4)]
    wa_full, wb_full, wf_full = [jnp.concatenate([per_chip[b][k] for b in range(4)], axis=2) for k in range(3)]
    pw_bf = pool_w.astype(BF16)

    def shards_of(l):
        return [w_in[l].astype(BF16), w_up[l].astype(BF16), w_out[l].astype(BF16), w_down[l].astype(BF16)]

    def assemble(g_in, g_up, g_out, g_down):
        return (jnp.concatenate([g_in[b] for b in range(4)], axis=1),
                jnp.concatenate([g_up[b] for b in range(4)], axis=1),
                g_out.reshape(DMIX, D), g_down.reshape(F, D))

    w_full = [assemble(*gather_chip_shards(shards_of(0), name="gather_layer_weights"))]

    def vec(a, l):
        return a[l].reshape(1, -1)

    x0 = x.reshape(S, D)
    h1 = norm_fwd(x0, vec(norm_mix_pre, 0), name="norm_first")
    saved = []
    for l in range(L):
        Win, Wup, Wout, Wdown = w_full[l]
        token = None
        if l + 1 < L:
            srcs = shards_of(l + 1)
            lands = [lax.empty((4,) + s.shape, s.dtype) for s in srcs]
            s_sems, r_sems, srcs, lands, token = _split_start(
                srcs, lands, 4 * len(srcs), _gather_direct_copies(False), name="gather_start_layer%d" % (l + 1))
        u = matmul(h1, Win, out_dtype=BF16, tm=512, tn=2176, tk=2048, j_outer=True, after=token, name="mm_in")
        ymix, cb = mixer_fwd(u, wa_full[l], wb_full[l], vec(conv_b_bias, l), vec(ln_b_gain, l), vec(ln_b_bias, l),
                             pw_bf[l], vec(pool_scale, l), name="mixer_fwd")
        y = matmul(ymix, Wout, out_dtype=F32, tm=512, tn=2048, tk=2048, name="mm_out")
        x1, h2 = resid_norm_fwd(x0, y, vec(norm_mix_post, l), vec(norm_ffn_pre, l), emit_h=True, name="resid_norm_mid")
        up = matmul(h2, Wup, out_dtype=BF16, tm=512, tn=2816, tk=2048, j_outer=True, name="mm_up")
        a, upc = ffn_fwd(up, wf_full[l], vec(conv_ffn_bias, l), name="ffn_fwd")
        f = matmul(a, Wdown, out_dtype=F32, tm=1024, tn=2048, tk=1408, name="mm_down")
        last = l == L - 1
        x2, h_next = resid_norm_fwd(x1, f, vec(norm_ffn_post, l), vec(norm_mix_pre, 0 if last else l + 1),
                                    emit_h=not last, name="resid_norm_last" if last else "resid_norm_end")
        saved.append((x0, h1, u, ymix, y, x1, h2, up, a, f, cb, upc))
        if l + 1 < L:
            w_full.append(assemble(*_split_wait(s_sems, r_sems, srcs, lands, x2, _gather_direct_copies(True),
                                                name="gather_wait_layer%d" % (l + 1))[1]))
        x0, h1 = x2, h_next

    dx, lsum = loss_head(x0, loss_target.reshape(S, D), name="loss_head")
    loss = lax.psum(lsum[0, 0] * (0.5 / D), ("x", "y", "c"))

    small = [None] * L
    big = [None] * L
    dt = dx
    _, df, _, dg4 = norm_bwd(dt, None, None, None, saved[L - 1][9], vec(norm_ffn_post, L - 1), name="norm_bwd_top")
    place = [jnp.reshape(v, (1,)).astype(jnp.int32) for v in
             (2 * xi + yi, 2 * (1 - xi) + yi, 2 * xi + (1 - yi), 2 * (1 - xi) + (1 - yi), ci)]

    def finish_exchange(pend, after):
        lp, s_sems, r_sems, ps, qs = pend
        ps, qs = _split_wait(s_sems, r_sems, ps, qs, after, _exchange_copies(True),
                             name="grad_exchange_wait_layer%d" % lp)
        fh = [sum_own_and_received(p, q, place, name="grad_sum_chips_%d" % k) for k, (p, q) in enumerate(zip(ps, qs))]
        big[lp] = sibling_join_halves(fh, name="grad_join_halves")

    pending, token = None, None
    for l in reversed(range(L)):
        Win, Wup, Wout, Wdown = w_full[l]
        x0, h1, u, ymix, y, x1, h2, up, a, f, cb, upc = saved[l]
        da = matmul(df, Wdown, tb=True, out_dtype=BF16, tm=512, tn=1408, tk=2048, j_outer=True, after=token,
                    name="mm_down_dx")
        g_down = matmul(a, df, ta=True, out_dtype=BF16, tm=1408, tn=1024, tk=2048, name="mm_down_dw")
        dup, dwf, dbf = ffn_bwd(up, upc, da, wf_full[l], name="ffn_bwd")
        dh2 = matmul(dup, Wup, tb=True, out_dtype=F32, tm=512, tn=2048, tk=2816, name="mm_up_dx")
        g_up = matmul(h2, dup, ta=True, out_dtype=BF16, tm=1024, tn=1408, tk=2048, groups=4, name="mm_up_dw")
        dt, dy, dg3, dg2 = norm_bwd(dt, dh2, x1, vec(norm_ffn_pre, l), y, vec(norm_mix_post, l), name="norm_bwd_mid")
        dymix = matmul(dy, Wout, tb=True, out_dtype=BF16, tm=512, tn=2048, tk=2048, name="mm_out_dx")
        g_out = matmul(ymix, dy, ta=True, out_dtype=BF16, tm=1024, tn=1024, tk=2048, name="mm_out_dw")
        du, dwa, dwb, dbb, dlg, dlb, dpw, dps = mixer_bwd(
            u, cb, dymix, wa_full[l], wb_full[l], vec(conv_b_bias, l), vec(ln_b_gain, l), vec(ln_b_bias, l),
            pw_bf[l], vec(pool_scale, l), name="mixer_bwd")
        dh1 = matmul(du, Win, tb=True, out_dtype=F32, tm=512, tn=2048, tk=2176, name="mm_in_dx")
        g_in = matmul(h1, du, ta=True, out_dtype=BF16, tm=1024, tn=2176, tk=1024, name="mm_in_dw")
        dg4_here = dg4
        if l > 0:
            dt, df, dg1, dg4 = norm_bwd(dt, dh1, x0, vec(norm_mix_pre, l), saved[l - 1][9], vec(norm_ffn_post, l - 1),
                                        name="norm_bwd_end")
        else:
            dt, _, dg1, _ = norm_bwd(dt, dh1, x0, vec(norm_mix_pre, 0), None, None, name="norm_bwd_bottom")
        small[l] = dict(norm_mix_pre=dg1, norm_mix_post=dg2, norm_ffn_pre=dg3, norm_ffn_post=dg4_here,
                        conv_a_w=dwa, conv_b_w=dwb, conv_b_bias=dbb, ln_b_gain=dlg, ln_b_bias=dlb,
                        pool_w=dpw, pool_scale=dps, conv_ffn_w=dwf, conv_ffn_bias=dbf)

        gs = [g_in.reshape(D, 4, DIN // 4).transpose(1, 0, 2), g_up,
              g_out.reshape(4, DMIX // 4, D), g_down.reshape(4, F // 4, D)]
        ts = sibling_swap_halves(gs, name="grad_swap_halves")
        ps = [pair_add(g, t, core, name="grad_pair_add_%d" % k) for k, (g, t) in enumerate(zip(gs, ts))]
        if pending is not None:
            finish_exchange(pending, dt)
        qs = [lax.empty(p.shape, p.dtype) for p in ps]
        s_sems, r_sems, ps, qs, token = _split_start(ps, qs, 3 * len(ps), _exchange_copies(False),
                                                     name="grad_exchange_start_layer%d" % l)
        pending = (l, s_sems, r_sems, ps, qs)
    grad_x = dt.reshape(1, S, D)

    rep_names = ["norm_mix_pre", "norm_mix_post", "norm_ffn_pre", "norm_ffn_post", "conv_b_bias", "ln_b_gain",
                 "ln_b_bias", "pool_w", "pool_scale", "conv_ffn_bias"]
    shd_names = ["conv_a_w", "conv_b_w", "conv_ffn_w"]
    given = dict(
        norm_mix_pre=(norm_mix_pre, m_norm_mix_pre, v_norm_mix_pre), norm_mix_post=(norm_mix_post, m_norm_mix_post, v_norm_mix_post),
        norm_ffn_pre=(norm_ffn_pre, m_norm_ffn_pre, v_norm_ffn_pre), norm_ffn_post=(norm_ffn_post, m_norm_ffn_post, v_norm_ffn_post),
        conv_b_bias=(conv_b_bias, m_conv_b_bias, v_conv_b_bias), ln_b_gain=(ln_b_gain, m_ln_b_gain, v_ln_b_gain),
        ln_b_bias=(ln_b_bias, m_ln_b_bias, v_ln_b_bias), pool_w=(pool_w, m_pool_w, v_pool_w),
        pool_scale=(pool_scale, m_pool_scale, v_pool_scale), conv_ffn_bias=(conv_ffn_bias, m_conv_ffn_bias, v_conv_ffn_bias),
        conv_a_w=(conv_a_w, m_conv_a_w, v_conv_a_w), conv_b_w=(conv_b_w, m_conv_b_w, v_conv_b_w),
        conv_ffn_w=(conv_ffn_w, m_conv_ffn_w, v_conv_ffn_w))
    full_shape = dict(conv_a_w=(L, K_A, WA), conv_b_w=(L, K_B, WB), conv_ffn_w=(L, K_F, F2))
    for nme in rep_names:
        full_shape[nme] = given[nme][0].shape
    names = rep_names + shd_names
    stacked = [jnp.stack([small[l][nme] for l in range(L)]).reshape(full_shape[nme]) for nme in names]
    parts = allgather8(_pack(stacked), name="gather_small_grads")
    totals = _unpack(sum_slots(parts, name="sum_small_grads"), [full_shape[nme] for nme in names])
    total = dict(zip(names, totals))
    for nme in shd_names:
        wd = full_shape[nme][2] // 4
        total[nme] = lax.dynamic_slice_in_dim(total[nme], chip * wd, wd, axis=2)
    shapes = [given[nme][0].shape for nme in names]
    d_s, m_s, v_s = adam_flat(_pack([given[nme][0] for nme in names]), _pack([total[nme] for nme in names]),
                              _pack([given[nme][1] for nme in names]), _pack([given[nme][2] for nme in names]),
                              name="adam_small")
    res = dict(zip(names, zip([total[nme] for nme in names], _unpack(d_s, shapes), _unpack(m_s, shapes),
                              _unpack(v_s, shapes))))

    finish_exchange(pending, d_s)

    for k, (nme, trio) in enumerate([("w_in", (w_in, m_w_in, v_w_in)), ("w_up", (w_up, m_w_up, v_w_up)),
                                     ("w_out", (w_out, m_w_out, v_w_out)), ("w_down", (w_down, m_w_down, v_w_down))]):
        res[nme] = adam_stacked(*trio, [big[l][k] for l in range(L)], name="adam_" + nme)

    order = ["norm_mix_pre", "norm_mix_post", "norm_ffn_pre", "norm_ffn_post", "w_in", "conv_a_w", "conv_b_w",
             "conv_b_bias", "ln_b_gain", "ln_b_bias", "pool_w", "pool_scale", "w_out", "w_up", "conv_ffn_w",
             "conv_ffn_bias", "w_down"]
    outs = [loss, grad_x]
    for k in range(4):
        outs += [res[nme][k] for nme in order]
    return tuple(outs)
```

```python
import functools

import jax
import jax.numpy as jnp
from jax import lax
from jax.experimental import pallas as pl
from jax.experimental.pallas import tpu as pltpu

F32 = jnp.float32
BF16 = jnp.bfloat16
MESH = pl.DeviceIdType.MESH

RMS_EPS = 1e-6
LN_EPS = 1e-5
ADAM_LR = 0.001
ADAM_B1 = 0.9
ADAM_B2 = 0.999
ADAM_EPS = 1e-08
ADAM_WD = 0.01
ADAM_STEP = 10
POOL_WINDOWS = (2, 4, 8, 16)
K_A = 3
K_B = 31
K_F = 3

LANES = 128
HALO_MIX = 32
HALO_FFN = 16
ROWS = 64
TM_MIXER = 512
TM_FFN = 256
TM_FFN_BWD = 128
TM_NORM = 256
ADAM_BLOCK_BYTES = 1 << 20
VMEM_LIMIT = 56 * 1024 * 1024


def _cp(n_axes):
    return pltpu.CompilerParams(dimension_semantics=("arbitrary",) * n_axes, vmem_limit_bytes=VMEM_LIMIT)


def _tile(dim, target, mult=LANES):
    if dim <= target:
        return dim
    t = (target // mult) * mult
    while t >= mult:
        if dim % t == 0:
            return t
        t -= mult
    return dim


def _chunks(n, rc=ROWS):
    out, r0 = [], 0
    while r0 < n:
        s = min(rc, n - r0)
        out.append((r0, s))
        r0 += s
    return out


def _sigmoid(x):
    return 1.0 / (1.0 + jnp.exp(-x))


def matmul(a, b, *, ta=False, tb=False, out_dtype, tm, tn, tk, j_outer=False, groups=1, after=None, name):
    if ta:
        K, M = a.shape
    else:
        M, K = a.shape
    if tb:
        N, K2 = b.shape
    else:
        K2, N = b.shape
    assert K == K2, (a.shape, b.shape)
    ng = N // groups
    tm, tn, tk = _tile(M, tm), _tile(ng, tn), _tile(K, tk)
    nm, nn, nk = M // tm, N // tn, K // tk
    per = ng // tn

    def ij(g0, g1):
        return (g1, g0) if j_outer else (g0, g1)

    def a_map(g0, g1, k):
        i, j = ij(g0, g1)
        return (k, i) if ta else (i, k)

    def b_map(g0, g1, k):
        i, j = ij(g0, g1)
        return (j, k) if tb else (k, j)

    def o_map(g0, g1, k):
        i, j = ij(g0, g1)
        return (j // per, i, j % per) if groups > 1 else (i, j)

    dims = (((0 if ta else 1,), (1 if tb else 0,)), ((), ()))
    use_acc = nk > 1 and out_dtype != F32

    def body(a_ref, b_ref, o_ref, *scratch):
        p = lax.dot_general(a_ref[...], b_ref[...], dims, preferred_element_type=F32)
        if nk == 1:
            o_ref[...] = p.astype(o_ref.dtype)
            return
        acc = scratch[0] if use_acc else o_ref
        k = pl.program_id(2)

        @pl.when(k == 0)
        def _():
            acc[...] = p

        @pl.when(k > 0)
        def _():
            acc[...] += p

        if use_acc:
            @pl.when(k == nk - 1)
            def _():
                o_ref[...] = acc[...].astype(o_ref.dtype)

    grid = (nn, nm, nk) if j_outer else (nm, nn, nk)
    if groups > 1:
        out_shape = jax.ShapeDtypeStruct((groups, M, ng), out_dtype)
        out_spec = pl.BlockSpec((None, tm, tn), o_map)
    else:
        out_shape = jax.ShapeDtypeStruct((M, N), out_dtype)
        out_spec = pl.BlockSpec((tm, tn), o_map)
    has_after = after is not None

    def body_after(a_ref, b_ref, after_ref, o_ref, *scratch):
        body(a_ref, b_ref, o_ref, *scratch)

    return pl.pallas_call(
        body_after if has_after else body, grid=grid,
        in_specs=[pl.BlockSpec((tk, tm) if ta else (tm, tk), a_map),
                  pl.BlockSpec((tn, tk) if tb else (tk, tn), b_map)]
                 + ([pl.BlockSpec(memory_space=pl.ANY)] if has_after else []),
        out_specs=out_spec, out_shape=out_shape,
        scratch_shapes=[pltpu.VMEM((tm, tn), F32)] if use_acc else [],
        compiler_params=_cp(3), name=name,
    )(*((a, b, after) if has_after else (a, b)))


def _rms(v):
    return lax.rsqrt(jnp.mean(v * v, axis=-1, keepdims=True) + RMS_EPS)


def norm_fwd(x, g, *, name):
    S, D = x.shape
    tm = _tile(S, TM_NORM, 16)

    def body(x_ref, g_ref, h_ref):
        v = x_ref[...]
        h_ref[...] = (v * _rms(v) * g_ref[...]).astype(BF16)

    return pl.pallas_call(
        body, grid=(S // tm,),
        in_specs=[pl.BlockSpec((tm, D), lambda i: (i, 0)), pl.BlockSpec((1, D), lambda i: (0, 0))],
        out_specs=pl.BlockSpec((tm, D), lambda i: (i, 0)),
        out_shape=jax.ShapeDtypeStruct((S, D), BF16), compiler_params=_cp(1), name=name,
    )(x, g)


def resid_norm_fwd(x, y, gp, gn, *, emit_h, name):
    S, D = x.shape
    tm = _tile(S, TM_NORM, 16)

    def body(x_ref, y_ref, gp_ref, gn_ref, xn_ref, *rest):
        yv = y_ref[...]
        xn = x_ref[...] + yv * _rms(yv) * gp_ref[...]
        xn_ref[...] = xn
        if emit_h:
            rest[0][...] = (xn * _rms(xn) * gn_ref[...]).astype(BF16)

    row = pl.BlockSpec((tm, D), lambda i: (i, 0))
    vec = pl.BlockSpec((1, D), lambda i: (0, 0))
    outs = pl.pallas_call(
        body, grid=(S // tm,), in_specs=[row, row, vec, vec],
        out_specs=[row, row] if emit_h else [row],
        out_shape=[jax.ShapeDtypeStruct((S, D), F32)] + ([jax.ShapeDtypeStruct((S, D), BF16)] if emit_h else []),
        compiler_params=_cp(1), name=name,
    )(x, y, gp, gn)
    return (outs[0], outs[1]) if emit_h else (outs[0], None)


def _rms_bwd(v, g, dout):
    r = _rms(v)
    gd = g * dout
    dv = r * gd - v * (r * r * r) * jnp.mean(v * gd, axis=-1, keepdims=True)
    return dv, dout * v * r


def norm_bwd(d_direct, dh, xn, gn, y, gp, *, name):
    S, D = d_direct.shape
    tm = _tile(S, TM_NORM, 16)
    has_h, has_y = dh is not None, y is not None

    def body(*refs):
        refs = list(refs)
        dd_ref = refs.pop(0)
        if has_h:
            dh_ref, xn_ref, gn_ref = refs.pop(0), refs.pop(0), refs.pop(0)
        if has_y:
            y_ref, gp_ref = refs.pop(0), refs.pop(0)
        if has_h:
            dt_ref = refs.pop(0)
        if has_y:
            dy_ref = refs.pop(0)
        if has_h:
            dgn_ref = refs.pop(0)
        if has_y:
            dgp_ref = refs.pop(0)
        i = pl.program_id(0)
        dt = dd_ref[...]
        if has_h:
            dv, gterm = _rms_bwd(xn_ref[...], gn_ref[...], dh_ref[...])
            dt = dt + dv
            dt_ref[...] = dt
            part = jnp.sum(gterm, axis=0, keepdims=True)

            @pl.when(i == 0)
            def _():
                dgn_ref[...] = part

            @pl.when(i > 0)
            def _():
                dgn_ref[...] += part
        if has_y:
            dy, gterm = _rms_bwd(y_ref[...], gp_ref[...], dt)
            dy_ref[...] = dy.astype(BF16)
            part2 = jnp.sum(gterm, axis=0, keepdims=True)

            @pl.when(i == 0)
            def _():
                dgp_ref[...] = part2

            @pl.when(i > 0)
            def _():
                dgp_ref[...] += part2

    row = pl.BlockSpec((tm, D), lambda i: (i, 0))
    vec = pl.BlockSpec((1, D), lambda i: (0, 0))
    ins, in_specs = [d_direct], [row]
    if has_h:
        ins += [dh, xn, gn]
        in_specs += [row, row, vec]
    if has_y:
        ins += [y, gp]
        in_specs += [row, vec]
    out_specs, out_shape = [], []
    if has_h:
        out_specs.append(row)
        out_shape.append(jax.ShapeDtypeStruct((S, D), F32))
    if has_y:
        out_specs.append(row)
        out_shape.append(jax.ShapeDtypeStruct((S, D), BF16))
    if has_h:
        out_specs.append(vec)
        out_shape.append(jax.ShapeDtypeStruct((1, D), F32))
    if has_y:
        out_specs.append(vec)
        out_shape.append(jax.ShapeDtypeStruct((1, D), F32))
    outs = list(pl.pallas_call(
        body, grid=(S // tm,), in_specs=in_specs, out_specs=out_specs, out_shape=out_shape,
        compiler_params=_cp(1), name=name,
    )(*ins))
    dt = outs.pop(0) if has_h else d_direct
    dy = outs.pop(0) if has_y else None
    dgn = outs.pop(0) if has_h else None
    dgp = outs.pop(0) if has_y else None
    return dt, dy, dgn, dgp


def loss_head(xl, target, *, name):
    S, D = xl.shape
    tm = _tile(S, TM_NORM, 16)

    def body(x_ref, t_ref, dx_ref, l_ref):
        i = pl.program_id(0)
        e = x_ref[...] - t_ref[...]
        dx_ref[...] = e * (1.0 / D)
        part = jnp.sum(e * e)

        @pl.when(i == 0)
        def _():
            l_ref[...] = jnp.zeros_like(l_ref) + part

        @pl.when(i > 0)
        def _():
            l_ref[...] += part

    row = pl.BlockSpec((tm, D), lambda i: (i, 0))
    return pl.pallas_call(
        body, grid=(S // tm,), in_specs=[row, row],
        out_specs=[row, pl.BlockSpec((8, LANES), lambda i: (0, 0))],
        out_shape=[jax.ShapeDtypeStruct((S, D), F32), jax.ShapeDtypeStruct((8, LANES), F32)],
        compiler_params=_cp(1), name=name,
    )(xl, target)


def _tap_sum(src, base, offs, wrows, r0, rc, c0):
    acc = None
    for k, off in enumerate(offs):
        t = src[base + r0 + off: base + r0 + off + rc, c0:c0 + LANES]
        if wrows is not None:
            t = t * wrows[k]
        acc = t if acc is None else acc + t
    return acc


def _tap_wgrad(out_ref, o0, a, a_base, b, b_base, offs, n, c0, first):
    for k, off in enumerate(offs):
        acc = None
        for r0, rc in _chunks(n):
            t = (a[a_base + r0: a_base + r0 + rc, c0:c0 + LANES]
                 * b[b_base + r0 + off: b_base + r0 + off + rc, c0:c0 + LANES])
            t = jnp.sum(t, axis=0, keepdims=True)
            acc = t if acc is None else acc + t
        _acc_store(out_ref, (slice(k, k + 1), slice(o0 + c0, o0 + c0 + LANES)), acc, first)


def _shift_copies(sh, src, n, shifts, c0):
    for b in shifts:
        for r0, rc in _chunks(n):
            sh[b, r0:r0 + rc, :] = src[r0 + b:r0 + b + rc, c0:c0 + LANES]


def _tap(sh, src, o, r0, rc, c0):
    a, b = divmod(o, 8)
    if b == 0:
        return src[r0 + o:r0 + o + rc, c0:c0 + LANES]
    return sh[b, r0 + 8 * a:r0 + 8 * a + rc, :]


def _acc_store(ref, idx, val, first):
    del first
    ref[idx] += val


def _zero_first(first, *refs):
    @pl.when(first)
    def _():
        for ref in refs:
            ref[...] = jnp.zeros_like(ref)


def _row_counts(t0, rc, w):
    t = t0 + lax.broadcasted_iota(jnp.int32, (rc, LANES), 0)
    return jnp.minimum(t + 1, w).astype(F32)


def mixer_fwd(u, wa, wb, bb, lg, lb, pw, ps, *, name):
    S, DIN = u.shape
    WA, WB, WC = wa.shape[1], wb.shape[1], ps.shape[1]
    DMIX = WA + WB + WC
    tm = _tile(S, TM_MIXER, HALO_MIX)
    HB = HALO_MIX
    r = tm // HB
    oCg, oVa, oVal, oGate, oC = WA, 2 * WA, 3 * WA, 3 * WA + WB, 3 * WA + 2 * WB
    offs_a = [k - (K_A - 1) for k in range(K_A)]
    offs_b = [k - (K_B - 1) for k in range(K_B)]

    def body(u_ref, uh_ref, wa_ref, wb_ref, bb_ref, lg_ref, lb_ref, pw_ref, ps_ref, y_ref, cbuf,
             pbuf, gbuf, shbuf, xbuf, plbuf):
        i = pl.program_id(0)
        hv = jnp.where(i > 0, 1.0, 0.0).astype(F32)

        def fill(src, dst0, n, scale):
            for r0, rc in _chunks(n):
                rows, drows = slice(r0, r0 + rc), slice(dst0 + r0, dst0 + r0 + rc)
                for c0 in range(0, WA, LANES):
                    v = (src[rows, oCg + c0:oCg + c0 + LANES].astype(F32)
                         * src[rows, oVa + c0:oVa + c0 + LANES].astype(F32))
                    pbuf[drows, c0:c0 + LANES] = v if scale is None else v * scale
                for c0 in range(0, WB, LANES):
                    v = (src[rows, oVal + c0:oVal + c0 + LANES].astype(F32)
                         * _sigmoid(src[rows, oGate + c0:oGate + c0 + LANES].astype(F32)))
                    gbuf[drows, c0:c0 + LANES] = v if scale is None else v * scale
                for c0 in range(0, WC, LANES):
                    v = src[rows, oC + c0:oC + c0 + LANES].astype(F32)
                    xbuf[drows, c0:c0 + LANES] = v if scale is None else v * scale

        fill(uh_ref, 0, HB, hv)
        fill(u_ref, HB, tm, None)

        for c0 in range(0, WA, LANES):
            w = [wa_ref[k:k + 1, c0:c0 + LANES] for k in range(K_A)]
            for r0, rc in _chunks(tm):
                q = _tap_sum(pbuf, HB, offs_a, w, r0, rc, c0)
                bg = u_ref[r0:r0 + rc, c0:c0 + LANES].astype(F32)
                y_ref[r0:r0 + rc, c0:c0 + LANES] = (bg * q).astype(BF16)

        for c0 in range(0, WB, LANES):
            w = [wb_ref[k:k + 1, c0:c0 + LANES] for k in range(K_B)]
            bias = bb_ref[:, c0:c0 + LANES]
            _shift_copies(shbuf, gbuf, tm + 24, range(1, 8), c0)
            for r0, rc in _chunks(tm):
                acc = bias
                for k in range(K_B):
                    acc = acc + _tap(shbuf, gbuf, HB - (K_B - 1) + k, r0, rc, c0) * w[k]
                cbuf[r0:r0 + rc, c0:c0 + LANES] = acc
        for r0, rc in _chunks(tm, 32):
            cb = cbuf[r0:r0 + rc, :]
            mu = jnp.mean(cb, axis=-1, keepdims=True)
            d = cb - mu
            n = d * lax.rsqrt(jnp.mean(d * d, axis=-1, keepdims=True) + LN_EPS)
            z = n * lg_ref[...] + lb_ref[...]
            y_ref[r0:r0 + rc, WA:WA + WB] = (z * _sigmoid(z)).astype(BF16)

        for g, win in enumerate(POOL_WINDOWS):
            c0 = g * LANES
            for r0, rc in _chunks(tm):
                s = _tap_sum(xbuf, HB, [-j for j in range(win)], None, r0, rc, c0)
                pooled = s / _row_counts(i * tm + r0, rc, win) - xbuf[HB + r0:HB + r0 + rc, c0:c0 + LANES]
                plbuf[r0:r0 + rc, c0:c0 + LANES] = pooled.astype(BF16)
            mixed = jnp.dot(plbuf[:, c0:c0 + LANES], pw_ref[g], preferred_element_type=F32)
            y_ref[:, WA + WB + c0:WA + WB + c0 + LANES] = (mixed * ps_ref[:, c0:c0 + LANES]).astype(BF16)

    full = lambda a: pl.BlockSpec(a.shape, lambda i: (0,) * a.ndim)
    return pl.pallas_call(
        body, grid=(S // tm,),
        in_specs=[pl.BlockSpec((tm, DIN), lambda i: (i, 0)),
                  pl.BlockSpec((HB, DIN), lambda i: (jnp.maximum(i * r - 1, 0), 0)),
                  full(wa), full(wb), full(bb), full(lg), full(lb), full(pw), full(ps)],
        out_specs=[pl.BlockSpec((tm, DMIX), lambda i: (i, 0)), pl.BlockSpec((tm, WB), lambda i: (i, 0))],
        out_shape=[jax.ShapeDtypeStruct((S, DMIX), BF16), jax.ShapeDtypeStruct((S, WB), F32)],
        scratch_shapes=[pltpu.VMEM((HB + tm, WA), F32), pltpu.VMEM((HB + tm, WB), F32),
                        pltpu.VMEM((8, tm + 24, LANES), F32), pltpu.VMEM((HB + tm, WC), F32),
                        pltpu.VMEM((tm, WC), BF16)],
        compiler_params=_cp(1), name=name,
    )(u, u, wa, wb, bb, lg, lb, pw, ps)


def mixer_bwd(u, cb, dy, wa, wb, bb, lg, lb, pw, ps, *, name):
    S, DIN = u.shape
    WA, WB, WC = wa.shape[1], wb.shape[1], ps.shape[1]
    NG = WC // LANES
    DMIX = WA + WB + WC
    tm = _tile(S, TM_MIXER, HALO_MIX)
    HB = HALO_MIX
    r = tm // HB
    nI = S // tm
    nH = S // HB
    oCg, oVa, oVal, oGate, oC = WA, 2 * WA, 3 * WA, 3 * WA + WB, 3 * WA + 2 * WB
    offs_a = [k - (K_A - 1) for k in range(K_A)]
    offs_b = [k - (K_B - 1) for k in range(K_B)]
    adj_a = [(K_A - 1) - k for k in range(K_A)]
    adj_b = [(K_B - 1) - k for k in range(K_B)]

    def body(u_ref, ub_ref, ua_ref, cb_ref, cba_ref, dy_ref, dya_ref,
             wa_ref, wb_ref, bb_ref, lg_ref, lb_ref, pw_ref, ps_ref,
             du_ref, dwa_ref, dwb_ref, dbb_ref, dlg_ref, dlb_ref, dpw_ref, dps_ref,
             pbuf, dqbuf, gbuf, dcbuf, shbuf, xbuf, plbuf, dmbuf, dplbuf, dpcbuf):
        i = pl.program_id(0)
        first = i == 0
        hvb = jnp.where(i > 0, 1.0, 0.0).astype(F32)
        hva = jnp.where(i < nI - 1, 1.0, 0.0).astype(F32)
        _zero_first(first, dwa_ref, dwb_ref, dbb_ref, dlg_ref, dlb_ref, dpw_ref, dps_ref)

        def fill(src, dst0, n, scale, main):
            for r0, rc in _chunks(n):
                rows, drows = slice(r0, r0 + rc), slice(dst0 + r0, dst0 + r0 + rc)
                for c0 in range(0, WA, LANES):
                    v = (src[rows, oCg + c0:oCg + c0 + LANES].astype(F32)
                         * src[rows, oVa + c0:oVa + c0 + LANES].astype(F32))
                    pbuf[drows, c0:c0 + LANES] = v if scale is None else v * scale
                for c0 in range(0, WC, LANES):
                    v = src[rows, oC + c0:oC + c0 + LANES].astype(F32)
                    xbuf[drows, c0:c0 + LANES] = v if scale is None else v * scale
                if main:
                    for c0 in range(0, WB, LANES):
                        gbuf[rows, c0:c0 + LANES] = (
                            src[rows, oVal + c0:oVal + c0 + LANES].astype(F32)
                            * _sigmoid(src[rows, oGate + c0:oGate + c0 + LANES].astype(F32)))

        fill(ub_ref, 0, HB, hvb, False)
        fill(u_ref, HB, tm, None, True)

        for r0, rc in _chunks(tm):
            for c0 in range(0, WA, LANES):
                dqbuf[r0:r0 + rc, c0:c0 + LANES] = (dy_ref[r0:r0 + rc, c0:c0 + LANES].astype(F32)
                                                     * u_ref[r0:r0 + rc, c0:c0 + LANES].astype(F32))
        for c0 in range(0, WA, LANES):
            dqbuf[tm:tm + HB, c0:c0 + LANES] = (dya_ref[:, c0:c0 + LANES].astype(F32)
                                                * ua_ref[:, c0:c0 + LANES].astype(F32)) * hva
        for c0 in range(0, WA, LANES):
            w = [wa_ref[k:k + 1, c0:c0 + LANES] for k in range(K_A)]
            for r0, rc in _chunks(tm):
                rows = slice(r0, r0 + rc)
                q = _tap_sum(pbuf, HB, offs_a, w, r0, rc, c0)
                du_ref[rows, c0:c0 + LANES] = (dy_ref[rows, c0:c0 + LANES].astype(F32) * q).astype(BF16)
                dp = _tap_sum(dqbuf, 0, adj_a, w, r0, rc, c0)
                cg = u_ref[rows, oCg + c0:oCg + c0 + LANES].astype(F32)
                va = u_ref[rows, oVa + c0:oVa + c0 + LANES].astype(F32)
                du_ref[rows, oCg + c0:oCg + c0 + LANES] = (dp * va).astype(BF16)
                du_ref[rows, oVa + c0:oVa + c0 + LANES] = (dp * cg).astype(BF16)
            _tap_wgrad(dwa_ref, 0, dqbuf, 0, pbuf, HB, offs_a, tm, c0, first)

        def ln_chunk(r0, rc, cb, dyb, scale, main):
            mu = jnp.mean(cb, axis=-1, keepdims=True)
            d = cb - mu
            rs = lax.rsqrt(jnp.mean(d * d, axis=-1, keepdims=True) + LN_EPS)
            n = d * rs
            z = n * lg_ref[...] + lb_ref[...]
            sg = _sigmoid(z)
            dz = dyb * (sg * (1.0 + z * (1.0 - sg)))
            dn = dz * lg_ref[...]
            dcb = rs * (dn - jnp.mean(dn, axis=-1, keepdims=True) - n * jnp.mean(dn * n, axis=-1, keepdims=True))
            if scale is not None:
                dcb = dcb * scale
            dcbuf[r0:r0 + rc, :] = dcb
            if main:
                return (jnp.sum(dz * n, axis=0, keepdims=True), jnp.sum(dz, axis=0, keepdims=True),
                        jnp.sum(dcb, axis=0, keepdims=True))
            return None

        sums = None
        for r0, rc in _chunks(tm, 32):
            part = ln_chunk(r0, rc, cb_ref[r0:r0 + rc, :], dy_ref[r0:r0 + rc, WA:WA + WB].astype(F32), None, True)
            sums = part if sums is None else tuple(a + b for a, b in zip(sums, part))
        ln_chunk(tm, HB, cba_ref[...], dya_ref[:, WA:WA + WB].astype(F32), hva, False)
        _acc_store(dlg_ref, (slice(None), slice(None)), sums[0], first)
        _acc_store(dlb_ref, (slice(None), slice(None)), sums[1], first)
        _acc_store(dbb_ref, (slice(None), slice(None)), sums[2], first)

        for c0 in range(0, WB, LANES):
            w = [wb_ref[k:k + 1, c0:c0 + LANES] for k in range(K_B)]
            _shift_copies(shbuf, dcbuf, tm + 24, range(1, 8), c0)
            for r0, rc in _chunks(tm):
                rows = slice(r0, r0 + rc)
                dglu = None
                for k in range(K_B):
                    t = _tap(shbuf, dcbuf, adj_b[k], r0, rc, c0) * w[k]
                    dglu = t if dglu is None else dglu + t
                val = u_ref[rows, oVal + c0:oVal + c0 + LANES].astype(F32)
                sg = _sigmoid(u_ref[rows, oGate + c0:oGate + c0 + LANES].astype(F32))
                du_ref[rows, oVal + c0:oVal + c0 + LANES] = (dglu * sg).astype(BF16)
                du_ref[rows, oGate + c0:oGate + c0 + LANES] = (dglu * val * sg * (1.0 - sg)).astype(BF16)
            for k in range(K_B):
                acc = None
                for r0, rc in _chunks(tm):
                    t = _tap(shbuf, dcbuf, adj_b[k], r0, rc, c0) * gbuf[r0:r0 + rc, c0:c0 + LANES]
                    acc = t if acc is None else acc + t
                _acc_store(dwb_ref, (slice(k, k + 1), slice(c0, c0 + LANES)),
                           jnp.sum(acc, axis=0, keepdims=True), first)

        for g, win in enumerate(POOL_WINDOWS):
            c0 = g * LANES
            cols = slice(c0, c0 + LANES)
            ycols = slice(WA + WB + c0, WA + WB + c0 + LANES)
            for r0, rc in _chunks(tm):
                s = _tap_sum(xbuf, HB, [-j for j in range(win)], None, r0, rc, c0)
                pooled = s / _row_counts(i * tm + r0, rc, win) - xbuf[HB + r0:HB + r0 + rc, cols]
                plbuf[r0:r0 + rc, cols] = pooled.astype(BF16)
            mixed = jnp.dot(plbuf[:, cols], pw_ref[g], preferred_element_type=F32)
            dyc = dy_ref[:, ycols].astype(F32)
            _acc_store(dps_ref, (slice(None), cols), jnp.sum(dyc * mixed, axis=0, keepdims=True), first)
            dmbuf[0:tm, :] = (dyc * ps_ref[:, cols]).astype(BF16)
            dmbuf[tm:tm + HB, :] = (dya_ref[:, ycols].astype(F32) * ps_ref[:, cols] * hva).astype(BF16)
            dpw = lax.dot_general(plbuf[:, cols], dmbuf[0:tm, :], (((0,), (0,)), ((), ())),
                                  preferred_element_type=F32)
            _acc_store(dpw_ref, (g, slice(None), slice(None)), dpw, first)
            dplbuf[...] = lax.dot_general(dmbuf[...], pw_ref[g], (((1,), (1,)), ((), ())),
                                          preferred_element_type=F32)
            for r0, rc in _chunks(tm + HB):
                dpcbuf[r0:r0 + rc, :] = dplbuf[r0:r0 + rc, :] / _row_counts(i * tm + r0, rc, win)
            for r0, rc in _chunks(tm):
                duc = _tap_sum(dpcbuf, 0, list(range(win)), None, r0, rc, 0) - dplbuf[r0:r0 + rc, :]
                du_ref[r0:r0 + rc, oC + c0:oC + c0 + LANES] = duc.astype(BF16)

    full = lambda a: pl.BlockSpec(a.shape, lambda i: (0,) * a.ndim)
    acc = lambda shape: pl.BlockSpec(shape, lambda i: (0,) * len(shape))
    small = [(K_A, WA), (K_B, WB), (1, WB), (1, WB), (1, WB), (NG, LANES, LANES), (1, WC)]
    outs = pl.pallas_call(
        body, grid=(nI,),
        in_specs=[pl.BlockSpec((tm, DIN), lambda i: (i, 0)),
                  pl.BlockSpec((HB, DIN), lambda i: (jnp.maximum(i * r - 1, 0), 0)),
                  pl.BlockSpec((HB, DIN), lambda i: (jnp.minimum((i + 1) * r, nH - 1), 0)),
                  pl.BlockSpec((tm, WB), lambda i: (i, 0)),
                  pl.BlockSpec((HB, WB), lambda i: (jnp.minimum((i + 1) * r, nH - 1), 0)),
                  pl.BlockSpec((tm, DMIX), lambda i: (i, 0)),
                  pl.BlockSpec((HB, DMIX), lambda i: (jnp.minimum((i + 1) * r, nH - 1), 0)),
                  full(wa), full(wb), full(bb), full(lg), full(lb), full(pw), full(ps)],
        out_specs=[pl.BlockSpec((tm, DIN), lambda i: (i, 0))] + [acc(s) for s in small],
        out_shape=[jax.ShapeDtypeStruct((S, DIN), BF16)] + [jax.ShapeDtypeStruct(s, F32) for s in small],
        scratch_shapes=[pltpu.VMEM((HB + tm, WA), F32), pltpu.VMEM((tm + HB, WA), F32),
                        pltpu.VMEM((tm, WB), F32), pltpu.VMEM((tm + HB, WB), F32),
                        pltpu.VMEM((8, tm + 24, LANES), F32), pltpu.VMEM((HB + tm, WC), F32),
                        pltpu.VMEM((tm, WC), BF16), pltpu.VMEM((tm + HB, LANES), BF16),
                        pltpu.VMEM((tm + HB, LANES), F32), pltpu.VMEM((tm + HB, LANES), F32)],
        compiler_params=_cp(1), name=name,
    )(u, u, u, cb, cb, dy, dy, wa, wb, bb, lg, lb, pw, ps)
    return outs


def ffn_fwd(up, wf, bf, *, name):
    S, F2 = up.shape
    F = F2 // 2
    tm = _tile(S, TM_FFN, HALO_FFN)
    HB = HALO_FFN
    r = tm // HB
    CW = _tile(F, 512)
    offs = [k - (K_F - 1) for k in range(K_F)]

    def body(up_ref, uph_ref, wf_ref, bf_ref, a_ref, upc_ref, ebuf):
        i = pl.program_id(0)
        hv = jnp.where(i > 0, 1.0, 0.0).astype(F32)
        for c0 in range(0, F, CW):
            for h, off in ((0, c0), (1, F + c0)):
                ebuf[h, 0:HB, :] = uph_ref[:, off:off + CW].astype(F32) * hv
                for r0, rc in _chunks(tm):
                    ebuf[h, HB + r0:HB + r0 + rc, :] = up_ref[r0:r0 + rc, off:off + CW].astype(F32)
            for l0 in range(0, CW, LANES):
                cg, cv = c0 + l0, F + c0 + l0
                wg = [wf_ref[k:k + 1, cg:cg + LANES] for k in range(K_F)]
                wv = [wf_ref[k:k + 1, cv:cv + LANES] for k in range(K_F)]
                bg = bf_ref[:, cg:cg + LANES]
                bv = bf_ref[:, cv:cv + LANES]
                for r0, rc in _chunks(tm):
                    gt = _tap_sum(ebuf.at[0], HB, offs, wg, r0, rc, l0) + bg
                    vl = _tap_sum(ebuf.at[1], HB, offs, wv, r0, rc, l0) + bv
                    a_ref[r0:r0 + rc, cg:cg + LANES] = (gt * _sigmoid(gt) * vl).astype(BF16)
                    upc_ref[r0:r0 + rc, cg:cg + LANES] = gt.astype(BF16)
                    upc_ref[r0:r0 + rc, cv:cv + LANES] = vl.astype(BF16)

    full = lambda a: pl.BlockSpec(a.shape, lambda i: (0,) * a.ndim)
    return pl.pallas_call(
        body, grid=(S // tm,),
        in_specs=[pl.BlockSpec((tm, F2), lambda i: (i, 0)),
                  pl.BlockSpec((HB, F2), lambda i: (jnp.maximum(i * r - 1, 0), 0)),
                  full(wf), full(bf)],
        out_specs=[pl.BlockSpec((tm, F), lambda i: (i, 0)), pl.BlockSpec((tm, F2), lambda i: (i, 0))],
        out_shape=[jax.ShapeDtypeStruct((S, F), BF16), jax.ShapeDtypeStruct((S, F2), BF16)],
        scratch_shapes=[pltpu.VMEM((2, HB + tm, CW), F32)],
        compiler_params=_cp(1), name=name,
    )(up, up, wf, bf)


def ffn_bwd(up, upc, da, wf, *, name):
    S, F2 = up.shape
    F = F2 // 2
    tm = _tile(S, TM_FFN_BWD, HALO_FFN)
    HB = HALO_FFN
    r = tm // HB
    nI = S // tm
    nH = S // HB
    CW = _tile(F, 512)
    adj = [(K_F - 1) - k for k in range(K_F)]

    def body(up_ref, upc_ref, upca_ref, da_ref, daa_ref, wf_ref, dup_ref, dwf_ref, dbf_ref, dbuf, shbuf):
        i = pl.program_id(0)
        first = i == 0
        hva = jnp.where(i < nI - 1, 1.0, 0.0).astype(F32)
        _zero_first(first, dwf_ref, dbf_ref)
        for c0 in range(0, F, CW):
            for l0 in range(0, CW, LANES):
                cg, cv = c0 + l0, F + c0 + l0
                wg = [wf_ref[k:k + 1, cg:cg + LANES] for k in range(K_F)]
                wv = [wf_ref[k:k + 1, cv:cv + LANES] for k in range(K_F)]
                sg_sum, sv_sum = None, None
                for r0, rc in _chunks(tm + HB):
                    if r0 < tm:
                        gt = upc_ref[r0:r0 + rc, cg:cg + LANES].astype(F32)
                        vl = upc_ref[r0:r0 + rc, cv:cv + LANES].astype(F32)
                        d = da_ref[r0:r0 + rc, cg:cg + LANES].astype(F32)
                    else:
                        gt = upca_ref[:, cg:cg + LANES].astype(F32)
                        vl = upca_ref[:, cv:cv + LANES].astype(F32)
                        d = daa_ref[:, cg:cg + LANES].astype(F32) * hva
                    s = _sigmoid(gt)
                    dg = d * vl * (s * (1.0 + gt * (1.0 - s)))
                    dv = d * (gt * s)
                    dbuf[0, r0:r0 + rc, :] = dg
                    dbuf[1, r0:r0 + rc, :] = dv
                    if r0 < tm:
                        pg, pv = jnp.sum(dg, axis=0, keepdims=True), jnp.sum(dv, axis=0, keepdims=True)
                        sg_sum = pg if sg_sum is None else sg_sum + pg
                        sv_sum = pv if sv_sum is None else sv_sum + pv
                _acc_store(dbf_ref, (slice(None), slice(cg, cg + LANES)), sg_sum, first)
                _acc_store(dbf_ref, (slice(None), slice(cv, cv + LANES)), sv_sum, first)
                for h, w, col in ((0, wg, cg), (1, wv, cv)):
                    d_h, sh_h = dbuf.at[h], shbuf.at[h]
                    _shift_copies(sh_h, d_h, tm, (1, 2), 0)
                    accs = [None] * K_F
                    for r0, rc in _chunks(tm):
                        taps = [_tap(sh_h, d_h, adj[k], r0, rc, 0) for k in range(K_F)]
                        dup_ref[r0:r0 + rc, col:col + LANES] = (
                            taps[0] * w[0] + taps[1] * w[1] + taps[2] * w[2]).astype(BF16)
                        uv = up_ref[r0:r0 + rc, col:col + LANES].astype(F32)
                        for k in range(K_F):
                            t = taps[k] * uv
                            accs[k] = t if accs[k] is None else accs[k] + t
                    for k in range(K_F):
                        _acc_store(dwf_ref, (slice(k, k + 1), slice(col, col + LANES)),
                                   jnp.sum(accs[k], axis=0, keepdims=True), first)

    full = lambda a: pl.BlockSpec(a.shape, lambda i: (0,) * a.ndim)
    return pl.pallas_call(
        body, grid=(nI,),
        in_specs=[pl.BlockSpec((tm, F2), lambda i: (i, 0)),
                  pl.BlockSpec((tm, F2), lambda i: (i, 0)),
                  pl.BlockSpec((HB, F2), lambda i: (jnp.minimum((i + 1) * r, nH - 1), 0)),
                  pl.BlockSpec((tm, F), lambda i: (i, 0)),
                  pl.BlockSpec((HB, F), lambda i: (jnp.minimum((i + 1) * r, nH - 1), 0)),
                  full(wf)],
        out_specs=[pl.BlockSpec((tm, F2), lambda i: (i, 0)),
                   pl.BlockSpec((K_F, F2), lambda i: (0, 0)), pl.BlockSpec((1, F2), lambda i: (0, 0))],
        out_shape=[jax.ShapeDtypeStruct((S, F2), BF16), jax.ShapeDtypeStruct((K_F, F2), F32),
                   jax.ShapeDtypeStruct((1, F2), F32)],
        scratch_shapes=[pltpu.VMEM((2, tm + HB, LANES), F32), pltpu.VMEM((2, 3, tm, LANES), F32)],
        compiler_params=_cp(1), name=name,
    )(up, upc, upc, da, da, wf)


HBM = pl.BlockSpec(memory_space=pltpu.HBM)


def _place():
    return lax.axis_index("x"), lax.axis_index("y"), lax.axis_index("c")


def allgather8(buf, *, name):
    R, C = buf.shape

    def body(x_ref, o_ref, send_sems, recv_sems, local_sem):
        x, y, c = _place()
        me = 4 * x + 2 * y + c
        mine = pltpu.make_async_copy(x_ref, o_ref.at[me], local_sem)
        mine.start()
        sends = []
        for k in range(1, 8):
            fx, fy, fc = (k >> 2) & 1, (k >> 1) & 1, k & 1
            px, py, pc = (x + fx) % 2, (y + fy) % 2, (c + fc) % 2
            cp = pltpu.make_async_remote_copy(
                src_ref=x_ref, dst_ref=o_ref.at[me], send_sem=send_sems.at[k - 1], recv_sem=recv_sems.at[k - 1],
                device_id=(px, py, pc), device_id_type=MESH)
            cp.start()
            sends.append(cp)
        for k in range(1, 8):
            fx, fy, fc = (k >> 2) & 1, (k >> 1) & 1, k & 1
            peer = 4 * ((x + fx) % 2) + 2 * ((y + fy) % 2) + (c + fc) % 2
            pltpu.make_async_remote_copy(
                src_ref=x_ref, dst_ref=o_ref.at[peer], send_sem=send_sems.at[k - 1], recv_sem=recv_sems.at[k - 1],
                device_id=(x, y, c), device_id_type=MESH).wait_recv()
        for cp in sends:
            cp.wait_send()
        mine.wait()

    return pl.pallas_call(
        body, in_specs=[HBM], out_specs=HBM, out_shape=jax.ShapeDtypeStruct((8, R, C), buf.dtype),
        scratch_shapes=[pltpu.SemaphoreType.DMA((7,)), pltpu.SemaphoreType.DMA((7,)), pltpu.SemaphoreType.DMA],
        name=name,
    )(buf)


def gather_chip_shards(shards, *, name):
    n = len(shards)

    def body(*refs):
        ins, outs = refs[:n], refs[n:2 * n]
        send_sems, recv_sems = refs[2 * n:]
        x, y, c = _place()
        b = 2 * x + y
        chips = [(1 - x, y), (x, 1 - y), (1 - x, 1 - y)]
        sends = []

        def half(a, which):
            rh = ins[a].shape[0] // 2
            return pl.ds(pl.multiple_of(which * rh, 16), rh)

        def copy(a, k, src, dst, to):
            return pltpu.make_async_remote_copy(
                src_ref=src, dst_ref=dst, send_sem=send_sems.at[7 * a + k], recv_sem=recv_sems.at[7 * a + k],
                device_id=to, device_id_type=MESH)

        for a in range(n):
            for j, (cx, cy) in enumerate(chips):
                cp = copy(a, j, ins[a].at[half(a, c)], outs[a].at[b, half(a, c)], (cx, cy, c))
                cp.start()
                sends.append(cp)
        for a in range(n):
            cp = copy(a, 6, ins[a], outs[a].at[b], (x, y, 1 - c))
            cp.start()
            sends.append(cp)
        for a in range(n):
            for j, (cx, cy) in enumerate(chips):
                got = outs[a].at[2 * cx + cy, half(a, c)]
                copy(a, j, got, got, (x, y, c)).wait_recv()
                cp = copy(a, 3 + j, got, got, (x, y, 1 - c))
                cp.start()
                sends.append(cp)
        for a in range(n):
            for j, (cx, cy) in enumerate(chips):
                got = outs[a].at[2 * cx + cy, half(a, 1 - c)]
                copy(a, 3 + j, got, got, (x, y, c)).wait_recv()
            copy(a, 6, ins[a], outs[a].at[b], (x, y, c)).wait_recv()
        for cp in sends:
            cp.wait_send()

    return pl.pallas_call(
        body, in_specs=[HBM] * n, out_specs=[HBM] * n,
        out_shape=[jax.ShapeDtypeStruct((4,) + s.shape, s.dtype) for s in shards],
        scratch_shapes=[pltpu.SemaphoreType.DMA((7 * n,)), pltpu.SemaphoreType.DMA((7 * n,))],
        name=name,
    )(*shards)


def sibling_swap_halves(gs, *, name):
    n = len(gs)

    def body(*refs):
        ins, outs = refs[:n], refs[n:2 * n]
        send_sems, recv_sems = refs[2 * n:]
        x, y, c = _place()
        cps = []
        for a in range(n):
            rh = ins[a].shape[1] // 2
            src = ins[a].at[:, pl.ds(pl.multiple_of((1 - c) * rh, 16), rh)]
            cp = pltpu.make_async_remote_copy(
                src_ref=src, dst_ref=outs[a], send_sem=send_sems.at[a], recv_sem=recv_sems.at[a],
                device_id=(x, y, 1 - c), device_id_type=MESH)
            cp.start()
            cps.append(cp)
        for cp in cps:
            cp.wait()

    return pl.pallas_call(
        body, in_specs=[HBM] * n, out_specs=[HBM] * n,
        out_shape=[jax.ShapeDtypeStruct((4, g.shape[1] // 2, g.shape[2]), g.dtype) for g in gs],
        scratch_shapes=[pltpu.SemaphoreType.DMA((n,)), pltpu.SemaphoreType.DMA((n,))],
        name=name,
    )(*gs)


SEM =pl.BlockSpec(memory_space=pltpu.SEMAPHORE)
SPLIT_COPY = pltpu.CompilerParams(has_side_effects=pltpu.SideEffectType.DATAFLOW_SIDE_EFFECTING)


def _split_start(srcs, lands, n_copies, issue, *, name):
    n, m = len(srcs), len(lands)

    def body(*refs):
        ins, lnd = refs[:n], refs[n:n + m]
        send_sems, recv_sems = refs[n + m], refs[n + m + 1]
        token = refs[-1]

        def copy(k, src, dst, to):
            return pltpu.make_async_remote_copy(src_ref=src, dst_ref=dst, send_sem=send_sems.at[k],
                                                recv_sem=recv_sems.at[k], device_id=to, device_id_type=MESH)

        for cp in issue(ins, lnd, copy):
            cp.start()
        token[...] = jnp.zeros_like(token)

    outs = pl.pallas_call(
        body, name=name,
        out_shape=(pltpu.SemaphoreType.DMA((n_copies,)), pltpu.SemaphoreType.DMA((n_copies,)),
                   *[pltpu.HBM(a.shape, a.dtype) for a in list(srcs) + list(lands)],
                   jax.ShapeDtypeStruct((8, LANES), F32)),
        in_specs=[HBM] * (n + m),
        out_specs=(SEM, SEM, *([HBM] * (n + m)), pl.BlockSpec(memory_space=pltpu.VMEM)),
        input_output_aliases={i: 2 + i for i in range(n + m)},
        compiler_params=SPLIT_COPY,
    )(*[pltpu.with_memory_space_constraint(a, pltpu.HBM) for a in list(srcs) + list(lands)])
    return outs[0], outs[1], list(outs[2:2 + n]), list(outs[2 + n:2 + n + m]), outs[-1]


def _split_wait(send_sems, recv_sems, srcs, lands, after, issue, *, name):
    n, m = len(srcs), len(lands)

    def body(*refs):
        ins, lnd = refs[:n], refs[n:n + m]
        s_sems, r_sems = refs[n + m], refs[n + m + 1]

        def copy(k, src, dst, to):
            return pltpu.make_async_remote_copy(src_ref=src, dst_ref=dst, send_sem=s_sems.at[k],
                                                recv_sem=r_sems.at[k], device_id=to, device_id_type=MESH)

        for cp in issue(ins, lnd, copy):
            cp.wait_send()
            cp.wait_recv()

    outs = pl.pallas_call(
        body, name=name,
        out_shape=tuple(pltpu.HBM(a.shape, a.dtype) for a in list(srcs) + list(lands)),
        in_specs=[HBM] * (n + m) + [SEM, SEM, pl.BlockSpec(memory_space=pl.ANY)],
        out_specs=tuple([HBM] * (n + m)),
        input_output_aliases={i: i for i in range(n + m)},
        compiler_params=SPLIT_COPY,
    )(*srcs, *lands, send_sems, recv_sems, after)
    return list(outs[:n]), list(outs[n:n + m])


def _gather_direct_copies(received):
    def issue(ins, lnd, copy):
        x, y, c = _place()
        b = 2 * x + y
        chips = [(1 - x, y), (x, 1 - y), (1 - x, 1 - y)]
        cps = []
        for a in range(len(ins)):
            for j, (cx, cy) in enumerate(chips):
                slot = 2 * cx + cy if received else b
                cps.append(copy(4 * a + j, ins[a], lnd[a].at[slot], (cx, cy, c)))
            cps.append(copy(4 * a + 3, ins[a], lnd[a].at[b], (x, y, 1 - c)))
        return cps
    return issue


def _exchange_copies(received):
    def issue(ins, lnd, copy):
        x, y, c = _place()
        b = 2 * x + y
        chips = [(1 - x, y), (x, 1 - y), (1 - x, 1 - y)]
        cps = []
        for a in range(len(ins)):
            for j, (cx, cy) in enumerate(chips):
                slot = 2 * cx + cy if received else b
                cps.append(copy(3 * a + j, ins[a].at[2 * cx + cy], lnd[a].at[slot], (cx, cy, c)))
        return cps
    return issue


def sibling_join_halves(fs, *, name):
    n = len(fs)

    def body(*refs):
        ins, outs = refs[:n], refs[n:2 * n]
        send_sems, recv_sems = refs[2 * n:]
        x, y, c = _place()
        sends = []
        for a in range(n):
            rh = ins[a].shape[0] // 2
            mine = pl.ds(pl.multiple_of(c * rh, 8), rh)
            cp = pltpu.make_async_remote_copy(
                src_ref=ins[a].at[mine], dst_ref=outs[a].at[mine], send_sem=send_sems.at[a],
                recv_sem=recv_sems.at[a], device_id=(x, y, 1 - c), device_id_type=MESH)
            cp.start()
            sends.append(cp)
        for a in range(n):
            rh = ins[a].shape[0] // 2
            other = pl.ds(pl.multiple_of((1 - c) * rh, 8), rh)
            pltpu.make_async_remote_copy(
                src_ref=ins[a].at[other], dst_ref=outs[a].at[other], send_sem=send_sems.at[a],
                recv_sem=recv_sems.at[a], device_id=(x, y, c), device_id_type=MESH).wait_recv()
        for cp in sends:
            cp.wait_send()

    return pl.pallas_call(
        body, in_specs=[HBM] * n, out_specs=[HBM] * n,
        out_shape=[jax.ShapeDtypeStruct(f.shape, f.dtype) for f in fs],
        input_output_aliases={a: a for a in range(n)},
        scratch_shapes=[pltpu.SemaphoreType.DMA((n,)), pltpu.SemaphoreType.DMA((n,))],
        name=name,
    )(*fs)


def pair_add(g, t, core, *, name):
    _, R, C = g.shape
    rh = R // 2
    tr = _tile(rh, 256, 16)
    nh = rh // tr

    def body(c_ref, g_ref, t_ref, o_ref):
        o_ref[...] = (g_ref[...].astype(F32) + t_ref[...].astype(F32)).astype(BF16)

    return pl.pallas_call(
        body,
        grid_spec=pltpu.PrefetchScalarGridSpec(
            num_scalar_prefetch=1, grid=(4, nh),
            in_specs=[pl.BlockSpec((None, tr, C), lambda b, i, c_ref: (b, c_ref[0] * nh + i, 0)),
                      pl.BlockSpec((None, tr, C), lambda b, i, c_ref: (b, i, 0))],
            out_specs=pl.BlockSpec((None, tr, C), lambda b, i, c_ref: (b, i, 0))),
        out_shape=jax.ShapeDtypeStruct((4, rh, C), BF16), compiler_params=_cp(2), name=name,
    )(core, g, t)


def sum_slots(q, *, name):
    N, R, C = q.shape
    tr = _tile(R, 256, 16)

    def body(q_ref, o_ref):
        acc = q_ref[0].astype(F32)
        for s in range(1, N):
            acc = acc + q_ref[s].astype(F32)
        o_ref[...] = acc

    return pl.pallas_call(
        body, grid=(R // tr,), in_specs=[pl.BlockSpec((N, tr, C), lambda i: (0, i, 0))],
        out_specs=pl.BlockSpec((tr, C), lambda i: (i, 0)),
        out_shape=jax.ShapeDtypeStruct((R, C), F32), compiler_params=_cp(1), name=name,
    )(q)


def sum_own_and_received(p, q, place, *, name):
    _, rh, C = p.shape
    tr = _tile(rh, 256, 16)
    nh = rh // tr

    def body(s0, s1, s2, s3, cr, p_ref, q1_ref, q2_ref, q3_ref, o_ref):
        o_ref[...] = (p_ref[...].astype(F32) + q1_ref[...].astype(F32)
                      + q2_ref[...].astype(F32) + q3_ref[...].astype(F32))

    def slot(d):
        return pl.BlockSpec((None, tr, C), lambda i, *pc: (pc[d][0], i, 0))

    return pl.pallas_call(
        body,
        grid_spec=pltpu.PrefetchScalarGridSpec(
            num_scalar_prefetch=5, grid=(nh,),
            in_specs=[slot(0), slot(1), slot(2), slot(3)],
            out_specs=pl.BlockSpec((tr, C), lambda i, *pc: (pc[4][0] * nh + i, 0))),
        out_shape=jax.ShapeDtypeStruct((2 * rh, C), F32), compiler_params=_cp(1), name=name,
    )(*place, p, q, q, q)


def _adam_math(w, g, m, v):
    m = ADAM_B1 * m + (1.0 - ADAM_B1) * g
    v = ADAM_B2 * v + (1.0 - ADAM_B2) * (g * g)
    m_hat = m / (1.0 - ADAM_B1 ** ADAM_STEP)
    v_hat = v / (1.0 - ADAM_B2 ** ADAM_STEP)
    delta = -ADAM_LR * (m_hat / (jnp.sqrt(v_hat) + ADAM_EPS) + ADAM_WD * w)
    return delta, m, v


def adam_stacked(w, m, v, grads, *, first_layer=0, earlier=None, name):
    L, R, C = w.shape
    n = len(grads)
    tr = _tile(R, max(8, ADAM_BLOCK_BYTES // (4 * C)), 8)
    nr = R // tr
    n_in = 3 + n + (4 if earlier is not None else 0)

    def body(*refs):
        w_ref, m_ref, v_ref = refs[:3]
        g_refs = refs[3:3 + n]
        go_ref, d_ref, mo_ref, vo_ref = refs[n_in:]
        lid = pl.program_id(0)
        for l in range(n):
            @pl.when(lid == l)
            def _(l=l):
                g = g_refs[l][...]
                d, mn, vn = _adam_math(w_ref[...], g, m_ref[...], v_ref[...])
                go_ref[...] = g
                d_ref[...] = d
                mo_ref[...] = mn
                vo_ref[...] = vn

    st = pl.BlockSpec((None, tr, C), lambda l, i: (l + first_layer, i, 0))
    g_specs = [pl.BlockSpec((tr, C), functools.partial(lambda l, i, ll: (jnp.where(l == ll, i, 0), 0), ll=ll))
               for ll in range(n)]
    extra = [pl.BlockSpec(memory_space=pl.ANY)] * 4 if earlier is not None else []
    return pl.pallas_call(
        body, grid=(n, nr), in_specs=[st, st, st] + g_specs + extra, out_specs=[st] * 4,
        out_shape=[jax.ShapeDtypeStruct((L, R, C), F32)] * 4,
        input_output_aliases={3 + n + k: k for k in range(4)} if earlier is not None else {},
        compiler_params=_cp(2), name=name,
    )(w, m, v, *grads, *(earlier if earlier is not None else ()))


def adam_flat(w, g, m, v, *, name):
    R, C = w.shape
    tr = _tile(R, 512, 8)

    def body(w_ref, g_ref, m_ref, v_ref, d_ref, mo_ref, vo_ref):
        d, mn, vn = _adam_math(w_ref[...], g_ref[...], m_ref[...], v_ref[...])
        d_ref[...] = d
        mo_ref[...] = mn
        vo_ref[...] = vn

    row = pl.BlockSpec((tr, C), lambda i: (i, 0))
    return pl.pallas_call(
        body, grid=(R // tr,), in_specs=[row] * 4, out_specs=[row] * 3,
        out_shape=[jax.ShapeDtypeStruct((R, C), F32)] * 3, compiler_params=_cp(1), name=name,
    )(w, g, m, v)


PACK_ROWS = 64


def _pack(arrays):
    flat = jnp.concatenate([a.reshape(-1).astype(F32) for a in arrays])
    n = flat.shape[0]
    unit = PACK_ROWS * LANES
    pad = (-n) % unit
    return jnp.pad(flat, (0, pad)).reshape(-1, LANES)


def _unpack(buf, shapes):
    flat = buf.reshape(-1)
    out, o = [], 0
    for s in shapes:
        n = 1
        for d in s:
            n *= d
        out.append(flat[o:o + n].reshape(s))
        o += n
    return out


def kernel(x, norm_mix_pre, norm_mix_post, norm_ffn_pre, norm_ffn_post, w_in, conv_a_w, conv_b_w, conv_b_bias, ln_b_gain, ln_b_bias, pool_w, pool_scale, w_out, w_up, conv_ffn_w, conv_ffn_bias, w_down, loss_target, m_norm_mix_pre, m_norm_mix_post, m_norm_ffn_pre, m_norm_ffn_post, m_w_in, m_conv_a_w, m_conv_b_w, m_conv_b_bias, m_ln_b_gain, m_ln_b_bias, m_pool_w, m_pool_scale, m_w_out, m_w_up, m_conv_ffn_w, m_conv_ffn_bias, m_w_down, v_norm_mix_pre, v_norm_mix_post, v_norm_ffn_pre, v_norm_ffn_post, v_w_in, v_conv_a_w, v_conv_b_w, v_conv_b_bias, v_ln_b_gain, v_ln_b_bias, v_pool_w, v_pool_scale, v_w_out, v_w_up, v_conv_ffn_w, v_conv_ffn_bias, v_w_down):
    L = w_in.shape[0]
    S, D = x.shape[1], x.shape[2]
    WA, WB, WC = 4 * conv_a_w.shape[2], 4 * conv_b_w.shape[2], pool_scale.shape[1]
    DIN, DMIX, F2 = 4 * w_in.shape[2], 4 * w_out.shape[1], 4 * w_up.shape[2]
    F = F2 // 2
    NG = WC // LANES
    xi, yi, ci = _place()
    chip = 2 * xi + yi
    core = jnp.reshape(ci, (1,)).astype(jnp.int32)

    conv_shapes = [(L, K_A, WA // 4), (L, K_B, WB // 4), (L, K_F, F2 // 4)]
    conv_all = allgather8(_pack([conv_a_w, conv_b_w, conv_ffn_w]), name="gather_conv_taps")
    per_chip = [_unpack(conv_all[2 * b], conv_shapes) for b in range(4)]
    wa_full, wb_full, wf_full = [jnp.concatenate([per_chip[b][k] for b in range(4)], axis=2) for k in range(3)]
    pw_bf = pool_w.astype(BF16)

    def shards_of(l):
        return [w_in[l].astype(BF16), w_up[l].astype(BF16), w_out[l].astype(BF16), w_down[l].astype(BF16)]

    def assemble(g_in, g_up, g_out, g_down):
        return (jnp.concatenate([g_in[b] for b in range(4)], axis=1),
                jnp.concatenate([g_up[b] for b in range(4)], axis=1),
                g_out.reshape(DMIX, D), g_down.reshape(F, D))

    w_full = [assemble(*gather_chip_shards(shards_of(0), name="gather_layer_weights"))]

    def vec(a, l):
        return a[l].reshape(1, -1)

    x0 = x.reshape(S, D)
    h1 = norm_fwd(x0, vec(norm_mix_pre, 0), name="norm_first")
    saved = []
    for l in range(L):
        Win, Wup, Wout, Wdown = w_full[l]
        token = None
        if l + 1 < L:
            srcs = shards_of(l + 1)
            lands = [lax.empty((4,) + s.shape, s.dtype) for s in srcs]
            s_sems, r_sems, srcs, lands, token = _split_start(
                srcs, lands, 4 * len(srcs), _gather_direct_copies(False), name="gather_start_layer%d" % (l + 1))
        u = matmul(h1, Win, out_dtype=BF16, tm=1024, tn=2176, tk=2048, j_outer=True, after=token, name="mm_in")
        ymix, cb = mixer_fwd(u, wa_full[l], wb_full[l], vec(conv_b_bias, l), vec(ln_b_gain, l), vec(ln_b_bias, l),
                             pw_bf[l], vec(pool_scale, l), name="mixer_fwd")
        y = matmul(ymix, Wout, out_dtype=F32, tm=512, tn=2048, tk=2048, name="mm_out")
        x1, h2 = resid_norm_fwd(x0, y, vec(norm_mix_post, l), vec(norm_ffn_pre, l), emit_h=True, name="resid_norm_mid")
        up = matmul(h2, Wup, out_dtype=BF16, tm=512, tn=2816, tk=2048, j_outer=True, name="mm_up")
        a, upc = ffn_fwd(up, wf_full[l], vec(conv_ffn_bias, l), name="ffn_fwd")
        f = matmul(a, Wdown, out_dtype=F32, tm=1024, tn=2048, tk=1408, name="mm_down")
        last = l == L - 1
        x2, h_next = resid_norm_fwd(x1, f, vec(norm_ffn_post, l), vec(norm_mix_pre, 0 if last else l + 1),
                                    emit_h=not last, name="resid_norm_last" if last else "resid_norm_end")
        saved.append((x0, h1, u, ymix, y, x1, h2, up, a, f, cb, upc))
        if l + 1 < L:
            w_full.append(assemble(*_split_wait(s_sems, r_sems, srcs, lands, x2, _gather_direct_copies(True),
                                                name="gather_wait_layer%d" % (l + 1))[1]))
        x0, h1 = x2, h_next

    dx, lsum = loss_head(x0, loss_target.reshape(S, D), name="loss_head")
    loss = lax.psum(lsum[0, 0] * (0.5 / D), ("x", "y", "c"))

    small = [None] * L
    big = [None] * L
    dt = dx
    _, df, _, dg4 = norm_bwd(dt, None, None, None, saved[L - 1][9], vec(norm_ffn_post, L - 1), name="norm_bwd_top")
    place = [jnp.reshape(v, (1,)).astype(jnp.int32) for v in
             (2 * xi + yi, 2 * (1 - xi) + yi, 2 * xi + (1 - yi), 2 * (1 - xi) + (1 - yi), ci)]

    def finish_exchange(pend, after):
        lp, s_sems, r_sems, ps, qs = pend
        ps, qs = _split_wait(s_sems, r_sems, ps, qs, after, _exchange_copies(True),
                             name="grad_exchange_wait_layer%d" % lp)
        fh = [sum_own_and_received(p, q, place, name="grad_sum_chips_%d" % k) for k, (p, q) in enumerate(zip(ps, qs))]
        big[lp] = sibling_join_halves(fh, name="grad_join_halves")

    pending, token = None, None
    for l in reversed(range(L)):
        Win, Wup, Wout, Wdown = w_full[l]
        x0, h1, u, ymix, y, x1, h2, up, a, f, cb, upc = saved[l]
        da = matmul(df, Wdown, tb=True, out_dtype=BF16, tm=1024, tn=1408, tk=2048, j_outer=True, after=token,
                    name="mm_down_dx")
        g_down = matmul(a, df, ta=True, out_dtype=BF16, tm=1408, tn=1024, tk=2048, name="mm_down_dw")
        dup, dwf, dbf = ffn_bwd(up, upc, da, wf_full[l], name="ffn_bwd")
        dh2 = matmul(dup, Wup, tb=True, out_dtype=F32, tm=512, tn=2048, tk=2816, name="mm_up_dx")
        g_up = matmul(h2, dup, ta=True, out_dtype=BF16, tm=1024, tn=1408, tk=2048, groups=4, name="mm_up_dw")
        dt, dy, dg3, dg2 = norm_bwd(dt, dh2, x1, vec(norm_ffn_pre, l), y, vec(norm_mix_post, l), name="norm_bwd_mid")
        dymix = matmul(dy, Wout, tb=True, out_dtype=BF16, tm=1024, tn=2048, tk=2048, name="mm_out_dx")
        g_out = matmul(ymix, dy, ta=True, out_dtype=BF16, tm=1024, tn=1024, tk=2048, name="mm_out_dw")
        du, dwa, dwb, dbb, dlg, dlb, dpw, dps = mixer_bwd(
            u, cb, dymix, wa_full[l], wb_full[l], vec(conv_b_bias, l), vec(ln_b_gain, l), vec(ln_b_bias, l),
            pw_bf[l], vec(pool_scale, l), name="mixer_bwd")
        dh1 = matmul(du, Win, tb=True, out_dtype=F32, tm=512, tn=2048, tk=2176, name="mm_in_dx")
        g_in = matmul(h1, du, ta=True, out_dtype=BF16, tm=1024, tn=2176, tk=1024, name="mm_in_dw")
        dg4_here = dg4
        if l > 0:
            dt, df, dg1, dg4 = norm_bwd(dt, dh1, x0, vec(norm_mix_pre, l), saved[l - 1][9], vec(norm_ffn_post, l - 1),
                                        name="norm_bwd_end")
        else:
            dt, _, dg1, _ = norm_bwd(dt, dh1, x0, vec(norm_mix_pre, 0), None, None, name="norm_bwd_bottom")
        small[l] = dict(norm_mix_pre=dg1, norm_mix_post=dg2, norm_ffn_pre=dg3, norm_ffn_post=dg4_here,
                        conv_a_w=dwa, conv_b_w=dwb, conv_b_bias=dbb, ln_b_gain=dlg, ln_b_bias=dlb,
                        pool_w=dpw, pool_scale=dps, conv_ffn_w=dwf, conv_ffn_bias=dbf)

        gs = [g_in.reshape(D, 4, DIN // 4).transpose(1, 0, 2), g_up,
              g_out.reshape(4, DMIX // 4, D), g_down.reshape(4, F // 4, D)]
        ts = sibling_swap_halves(gs, name="grad_swap_halves")
        ps = [pair_add(g, t, core, name="grad_pair_add_%d" % k) for k, (g, t) in enumerate(zip(gs, ts))]
        if pending is not None:
            finish_exchange(pending, dt)
        qs = [lax.empty(p.shape, p.dtype) for p in ps]
        s_sems, r_sems, ps, qs, token = _split_start(ps, qs, 3 * len(ps), _exchange_copies(False),
                                                     name="grad_exchange_start_layer%d" % l)
        pending = (l, s_sems, r_sems, ps, qs)
    grad_x = dt.reshape(1, S, D)

    rep_names = ["norm_mix_pre", "norm_mix_post", "norm_ffn_pre", "norm_ffn_post", "conv_b_bias", "ln_b_gain",
                 "ln_b_bias", "pool_w", "pool_scale", "conv_ffn_bias"]
    shd_names = ["conv_a_w", "conv_b_w", "conv_ffn_w"]
    given = dict(
        norm_mix_pre=(norm_mix_pre, m_norm_mix_pre, v_norm_mix_pre), norm_mix_post=(norm_mix_post, m_norm_mix_post, v_norm_mix_post),
        norm_ffn_pre=(norm_ffn_pre, m_norm_ffn_pre, v_norm_ffn_pre), norm_ffn_post=(norm_ffn_post, m_norm_ffn_post, v_norm_ffn_post),
        conv_b_bias=(conv_b_bias, m_conv_b_bias, v_conv_b_bias), ln_b_gain=(ln_b_gain, m_ln_b_gain, v_ln_b_gain),
        ln_b_bias=(ln_b_bias, m_ln_b_bias, v_ln_b_bias), pool_w=(pool_w, m_pool_w, v_pool_w),
        pool_scale=(pool_scale, m_pool_scale, v_pool_scale), conv_ffn_bias=(conv_ffn_bias, m_conv_ffn_bias, v_conv_ffn_bias),
        conv_a_w=(conv_a_w, m_conv_a_w, v_conv_a_w), conv_b_w=(conv_b_w, m_conv_b_w, v_conv_b_w),
        conv_ffn_w=(conv_ffn_w, m_conv_ffn_w, v_conv_ffn_w))
    full_shape = dict(conv_a_w=(L, K_A, WA), conv_b_w=(L, K_B, WB), conv_ffn_w=(L, K_F, F2))
    for nme in rep_names:
        full_shape[nme] = given[nme][0].shape
    names = rep_names + shd_names
    stacked = [jnp.stack([small[l][nme] for l in range(L)]).reshape(full_shape[nme]) for nme in names]
    parts = allgather8(_pack(stacked), name="gather_small_grads")
    totals = _unpack(sum_slots(parts, name="sum_small_grads"), [full_shape[nme] for nme in names])
    total = dict(zip(names, totals))
    for nme in shd_names:
        wd = full_shape[nme][2] // 4
        total[nme] = lax.dynamic_slice_in_dim(total[nme], chip * wd, wd, axis=2)
    shapes = [given[nme][0].shape for nme in names]
    d_s, m_s, v_s = adam_flat(_pack([given[nme][0] for nme in names]), _pack([total[nme] for nme in names]),
                              _pack([given[nme][1] for nme in names]), _pack([given[nme][2] for nme in names]),
                              name="adam_small")
    res = dict(zip(names, zip([total[nme] for nme in names], _unpack(d_s, shapes), _unpack(m_s, shapes),
                              _unpack(v_s, shapes))))

    large = [("w_in", (w_in, m_w_in, v_w_in)), ("w_up", (w_up, m_w_up, v_w_up)),
             ("w_out", (w_out, m_w_out, v_w_out)), ("w_down", (w_down, m_w_down, v_w_down))]
    assert L > 1
    upper = [adam_stacked(*trio, [big[l][k] for l in range(1, L)], first_layer=1, name="adam_upper_" + nme)
             for k, (nme, trio) in enumerate(large)]
    finish_exchange(pending, upper[-1][1])
    for k, (nme, trio) in enumerate(large):
        res[nme] = adam_stacked(*trio, [big[0][k]], earlier=upper[k], name="adam_first_" + nme)

    order = ["norm_mix_pre", "norm_mix_post", "norm_ffn_pre", "norm_ffn_post", "w_in", "conv_a_w", "conv_b_w",
             "conv_b_bias", "ln_b_gain", "ln_b_bias", "pool_w", "pool_scale", "w_out", "w_up", "conv_ffn_w",
             "conv_ffn_bias", "w_down"]
    outs = [loss, grad_x]
    for k in range(4):
        outs += [res[nme][k] for nme in order]
    return tuple(outs)
```

```python
import functools

import jax
import jax.numpy as jnp
from jax import lax
from jax.experimental import pallas as pl
from jax.experimental.pallas import tpu as pltpu

F32 = jnp.float32
BF16 = jnp.bfloat16
MESH = pl.DeviceIdType.MESH

RMS_EPS = 1e-6
LN_EPS = 1e-5
ADAM_LR = 0.001
ADAM_B1 = 0.9
ADAM_B2 = 0.999
ADAM_EPS = 1e-08
ADAM_WD = 0.01
ADAM_STEP = 10
POOL_WINDOWS = (2, 4, 8, 16)
K_A = 3
K_B = 31
K_F = 3

LANES = 128
HALO_MIX = 32
HALO_FFN = 16
ROWS = 64
TM_MIXER = 512
TM_FFN = 256
TM_FFN_BWD = 128
TM_NORM = 256
ADAM_BLOCK_BYTES = 1 << 20
VMEM_LIMIT = 56 * 1024 * 1024


def _cp(n_axes):
    return pltpu.CompilerParams(dimension_semantics=("arbitrary",) * n_axes, vmem_limit_bytes=VMEM_LIMIT)


def _tile(dim, target, mult=LANES):
    if dim <= target:
        return dim
    t = (target // mult) * mult
    while t >= mult:
        if dim % t == 0:
            return t
        t -= mult
    return dim


def _chunks(n, rc=ROWS):
    out, r0 = [], 0
    while r0 < n:
        s = min(rc, n - r0)
        out.append((r0, s))
        r0 += s
    return out


def _sigmoid(x):
    return 1.0 / (1.0 + jnp.exp(-x))


def matmul(a, b, *, ta=False, tb=False, out_dtype, tm, tn, tk, j_outer=False, groups=1, after=None, name):
    if ta:
        K, M = a.shape
    else:
        M, K = a.shape
    if tb:
        N, K2 = b.shape
    else:
        K2, N = b.shape
    assert K == K2, (a.shape, b.shape)
    ng = N // groups
    tm, tn, tk = _tile(M, tm), _tile(ng, tn), _tile(K, tk)
    nm, nn, nk = M // tm, N // tn, K // tk
    per = ng // tn

    def ij(g0, g1):
        return (g1, g0) if j_outer else (g0, g1)

    def a_map(g0, g1, k):
        i, j = ij(g0, g1)
        return (k, i) if ta else (i, k)

    def b_map(g0, g1, k):
        i, j = ij(g0, g1)
        return (j, k) if tb else (k, j)

    def o_map(g0, g1, k):
        i, j = ij(g0, g1)
        return (j // per, i, j % per) if groups > 1 else (i, j)

    dims = (((0 if ta else 1,), (1 if tb else 0,)), ((), ()))
    use_acc = nk > 1 and out_dtype != F32

    def body(a_ref, b_ref, o_ref, *scratch):
        p = lax.dot_general(a_ref[...], b_ref[...], dims, preferred_element_type=F32)
        if nk == 1:
            o_ref[...] = p.astype(o_ref.dtype)
            return
        acc = scratch[0] if use_acc else o_ref
        k = pl.program_id(2)

        @pl.when(k == 0)
        def _():
            acc[...] = p

        @pl.when(k > 0)
        def _():
            acc[...] += p

        if use_acc:
            @pl.when(k == nk - 1)
            def _():
                o_ref[...] = acc[...].astype(o_ref.dtype)

    grid = (nn, nm, nk) if j_outer else (nm, nn, nk)
    if groups > 1:
        out_shape = jax.ShapeDtypeStruct((groups, M, ng), out_dtype)
        out_spec = pl.BlockSpec((None, tm, tn), o_map)
    else:
        out_shape = jax.ShapeDtypeStruct((M, N), out_dtype)
        out_spec = pl.BlockSpec((tm, tn), o_map)
    has_after = after is not None

    def body_after(a_ref, b_ref, after_ref, o_ref, *scratch):
        body(a_ref, b_ref, o_ref, *scratch)

    return pl.pallas_call(
        body_after if has_after else body, grid=grid,
        in_specs=[pl.BlockSpec((tk, tm) if ta else (tm, tk), a_map),
                  pl.BlockSpec((tn, tk) if tb else (tk, tn), b_map)]
                 + ([pl.BlockSpec(memory_space=pl.ANY)] if has_after else []),
        out_specs=out_spec, out_shape=out_shape,
        scratch_shapes=[pltpu.VMEM((tm, tn), F32)] if use_acc else [],
        compiler_params=_cp(3), name=name,
    )(*((a, b, after) if has_after else (a, b)))


def _rms(v):
    return lax.rsqrt(jnp.mean(v * v, axis=-1, keepdims=True) + RMS_EPS)


def norm_fwd(x, g, *, name):
    S, D = x.shape
    tm = _tile(S, TM_NORM, 16)

    def body(x_ref, g_ref, h_ref):
        v = x_ref[...]
        h_ref[...] = (v * _rms(v) * g_ref[...]).astype(BF16)

    return pl.pallas_call(
        body, grid=(S // tm,),
        in_specs=[pl.BlockSpec((tm, D), lambda i: (i, 0)), pl.BlockSpec((1, D), lambda i: (0, 0))],
        out_specs=pl.BlockSpec((tm, D), lambda i: (i, 0)),
        out_shape=jax.ShapeDtypeStruct((S, D), BF16), compiler_params=_cp(1), name=name,
    )(x, g)


def resid_norm_fwd(x, y, gp, gn, *, emit_h, name):
    S, D = x.shape
    tm = _tile(S, TM_NORM, 16)

    def body(x_ref, y_ref, gp_ref, gn_ref, xn_ref, *rest):
        yv = y_ref[...]
        xn = x_ref[...] + yv * _rms(yv) * gp_ref[...]
        xn_ref[...] = xn
        if emit_h:
            rest[0][...] = (xn * _rms(xn) * gn_ref[...]).astype(BF16)

    row = pl.BlockSpec((tm, D), lambda i: (i, 0))
    vec = pl.BlockSpec((1, D), lambda i: (0, 0))
    outs = pl.pallas_call(
        body, grid=(S // tm,), in_specs=[row, row, vec, vec],
        out_specs=[row, row] if emit_h else [row],
        out_shape=[jax.ShapeDtypeStruct((S, D), F32)] + ([jax.ShapeDtypeStruct((S, D), BF16)] if emit_h else []),
        compiler_params=_cp(1), name=name,
    )(x, y, gp, gn)
    return (outs[0], outs[1]) if emit_h else (outs[0], None)


def _rms_bwd(v, g, dout):
    r = _rms(v)
    gd = g * dout
    dv = r * gd - v * (r * r * r) * jnp.mean(v * gd, axis=-1, keepdims=True)
    return dv, dout * v * r


def norm_bwd(d_direct, dh, xn, gn, y, gp, *, name):
    S, D = d_direct.shape
    tm = _tile(S, TM_NORM, 16)
    has_h, has_y = dh is not None, y is not None

    def body(*refs):
        refs = list(refs)
        dd_ref = refs.pop(0)
        if has_h:
            dh_ref, xn_ref, gn_ref = refs.pop(0), refs.pop(0), refs.pop(0)
        if has_y:
            y_ref, gp_ref = refs.pop(0), refs.pop(0)
        if has_h:
            dt_ref = refs.pop(0)
        if has_y:
            dy_ref = refs.pop(0)
        if has_h:
            dgn_ref = refs.pop(0)
        if has_y:
            dgp_ref = refs.pop(0)
        i = pl.program_id(0)
        dt = dd_ref[...]
        if has_h:
            dv, gterm = _rms_bwd(xn_ref[...], gn_ref[...], dh_ref[...])
            dt = dt + dv
            dt_ref[...] = dt
            part = jnp.sum(gterm, axis=0, keepdims=True)

            @pl.when(i == 0)
            def _():
                dgn_ref[...] = part

            @pl.when(i > 0)
            def _():
                dgn_ref[...] += part
        if has_y:
            dy, gterm = _rms_bwd(y_ref[...], gp_ref[...], dt)
            dy_ref[...] = dy.astype(BF16)
            part2 = jnp.sum(gterm, axis=0, keepdims=True)

            @pl.when(i == 0)
            def _():
                dgp_ref[...] = part2

            @pl.when(i > 0)
            def _():
                dgp_ref[...] += part2

    row = pl.BlockSpec((tm, D), lambda i: (i, 0))
    vec = pl.BlockSpec((1, D), lambda i: (0, 0))
    ins, in_specs = [d_direct], [row]
    if has_h:
        ins += [dh, xn, gn]
        in_specs += [row, row, vec]
    if has_y:
        ins += [y, gp]
        in_specs += [row, vec]
    out_specs, out_shape = [], []
    if has_h:
        out_specs.append(row)
        out_shape.append(jax.ShapeDtypeStruct((S, D), F32))
    if has_y:
        out_specs.append(row)
        out_shape.append(jax.ShapeDtypeStruct((S, D), BF16))
    if has_h:
        out_specs.append(vec)
        out_shape.append(jax.ShapeDtypeStruct((1, D), F32))
    if has_y:
        out_specs.append(vec)
        out_shape.append(jax.ShapeDtypeStruct((1, D), F32))
    outs = list(pl.pallas_call(
        body, grid=(S // tm,), in_specs=in_specs, out_specs=out_specs, out_shape=out_shape,
        compiler_params=_cp(1), name=name,
    )(*ins))
    dt = outs.pop(0) if has_h else d_direct
    dy = outs.pop(0) if has_y else None
    dgn = outs.pop(0) if has_h else None
    dgp = outs.pop(0) if has_y else None
    return dt, dy, dgn, dgp


def loss_head(xl, target, *, name):
    S, D = xl.shape
    tm = _tile(S, TM_NORM, 16)

    def body(x_ref, t_ref, dx_ref, l_ref):
        i = pl.program_id(0)
        e = x_ref[...] - t_ref[...]
        dx_ref[...] = e * (1.0 / D)
        part = jnp.sum(e * e)

        @pl.when(i == 0)
        def _():
            l_ref[...] = jnp.zeros_like(l_ref) + part

        @pl.when(i > 0)
        def _():
            l_ref[...] += part

    row = pl.BlockSpec((tm, D), lambda i: (i, 0))
    return pl.pallas_call(
        body, grid=(S // tm,), in_specs=[row, row],
        out_specs=[row, pl.BlockSpec((8, LANES), lambda i: (0, 0))],
        out_shape=[jax.ShapeDtypeStruct((S, D), F32), jax.ShapeDtypeStruct((8, LANES), F32)],
        compiler_params=_cp(1), name=name,
    )(xl, target)


def _tap_sum(src, base, offs, wrows, r0, rc, c0):
    acc = None
    for k, off in enumerate(offs):
        t = src[base + r0 + off: base + r0 + off + rc, c0:c0 + LANES]
        if wrows is not None:
            t = t * wrows[k]
        acc = t if acc is None else acc + t
    return acc


def _tap_wgrad(out_ref, o0, a, a_base, b, b_base, offs, n, c0, first):
    for k, off in enumerate(offs):
        acc = None
        for r0, rc in _chunks(n):
            t = (a[a_base + r0: a_base + r0 + rc, c0:c0 + LANES]
                 * b[b_base + r0 + off: b_base + r0 + off + rc, c0:c0 + LANES])
            t = jnp.sum(t, axis=0, keepdims=True)
            acc = t if acc is None else acc + t
        _acc_store(out_ref, (slice(k, k + 1), slice(o0 + c0, o0 + c0 + LANES)), acc, first)


def _shift_copies(sh, src, n, shifts, c0):
    for b in shifts:
        for r0, rc in _chunks(n):
            sh[b, r0:r0 + rc, :] = src[r0 + b:r0 + b + rc, c0:c0 + LANES]


def _tap(sh, src, o, r0, rc, c0):
    a, b = divmod(o, 8)
    if b == 0:
        return src[r0 + o:r0 + o + rc, c0:c0 + LANES]
    return sh[b, r0 + 8 * a:r0 + 8 * a + rc, :]


def _acc_store(ref, idx, val, first):
    del first
    ref[idx] += val


def _zero_first(first, *refs):
    @pl.when(first)
    def _():
        for ref in refs:
            ref[...] = jnp.zeros_like(ref)


def _row_counts(t0, rc, w):
    t = t0 + lax.broadcasted_iota(jnp.int32, (rc, LANES), 0)
    return jnp.minimum(t + 1, w).astype(F32)


def mixer_fwd(u, wa, wb, bb, lg, lb, pw, ps, *, name):
    S, DIN = u.shape
    WA, WB, WC = wa.shape[1], wb.shape[1], ps.shape[1]
    DMIX = WA + WB + WC
    tm = _tile(S, TM_MIXER, HALO_MIX)
    HB = HALO_MIX
    r = tm // HB
    oCg, oVa, oVal, oGate, oC = WA, 2 * WA, 3 * WA, 3 * WA + WB, 3 * WA + 2 * WB
    offs_a = [k - (K_A - 1) for k in range(K_A)]
    offs_b = [k - (K_B - 1) for k in range(K_B)]

    def body(u_ref, uh_ref, wa_ref, wb_ref, bb_ref, lg_ref, lb_ref, pw_ref, ps_ref, y_ref, cbuf,
             pbuf, gbuf, shbuf, xbuf, plbuf):
        i = pl.program_id(0)
        hv = jnp.where(i > 0, 1.0, 0.0).astype(F32)

        def fill(src, dst0, n, scale):
            for r0, rc in _chunks(n):
                rows, drows = slice(r0, r0 + rc), slice(dst0 + r0, dst0 + r0 + rc)
                for c0 in range(0, WA, LANES):
                    v = (src[rows, oCg + c0:oCg + c0 + LANES].astype(F32)
                         * src[rows, oVa + c0:oVa + c0 + LANES].astype(F32))
                    pbuf[drows, c0:c0 + LANES] = v if scale is None else v * scale
                for c0 in range(0, WB, LANES):
                    v = (src[rows, oVal + c0:oVal + c0 + LANES].astype(F32)
                         * _sigmoid(src[rows, oGate + c0:oGate + c0 + LANES].astype(F32)))
                    gbuf[drows, c0:c0 + LANES] = v if scale is None else v * scale
                for c0 in range(0, WC, LANES):
                    v = src[rows, oC + c0:oC + c0 + LANES].astype(F32)
                    xbuf[drows, c0:c0 + LANES] = v if scale is None else v * scale

        fill(uh_ref, 0, HB, hv)
        fill(u_ref, HB, tm, None)

        for c0 in range(0, WA, LANES):
            w = [wa_ref[k:k + 1, c0:c0 + LANES] for k in range(K_A)]
            for r0, rc in _chunks(tm):
                q = _tap_sum(pbuf, HB, offs_a, w, r0, rc, c0)
                bg = u_ref[r0:r0 + rc, c0:c0 + LANES].astype(F32)
                y_ref[r0:r0 + rc, c0:c0 + LANES] = (bg * q).astype(BF16)

        for c0 in range(0, WB, LANES):
            w = [wb_ref[k:k + 1, c0:c0 + LANES] for k in range(K_B)]
            bias = bb_ref[:, c0:c0 + LANES]
            _shift_copies(shbuf, gbuf, tm + 24, range(1, 8), c0)
            for r0, rc in _chunks(tm):
                acc = bias
                for k in range(K_B):
                    acc = acc + _tap(shbuf, gbuf, HB - (K_B - 1) + k, r0, rc, c0) * w[k]
                cbuf[r0:r0 + rc, c0:c0 + LANES] = acc
        for r0, rc in _chunks(tm, 32):
            cb = cbuf[r0:r0 + rc, :]
            mu = jnp.mean(cb, axis=-1, keepdims=True)
            d = cb - mu
            n = d * lax.rsqrt(jnp.mean(d * d, axis=-1, keepdims=True) + LN_EPS)
            z = n * lg_ref[...] + lb_ref[...]
            y_ref[r0:r0 + rc, WA:WA + WB] = (z * _sigmoid(z)).astype(BF16)

        for g, win in enumerate(POOL_WINDOWS):
            c0 = g * LANES
            for r0, rc in _chunks(tm):
                s = _tap_sum(xbuf, HB, [-j for j in range(win)], None, r0, rc, c0)
                pooled = s / _row_counts(i * tm + r0, rc, win) - xbuf[HB + r0:HB + r0 + rc, c0:c0 + LANES]
                plbuf[r0:r0 + rc, c0:c0 + LANES] = pooled.astype(BF16)
            mixed = jnp.dot(plbuf[:, c0:c0 + LANES], pw_ref[g], preferred_element_type=F32)
            y_ref[:, WA + WB + c0:WA + WB + c0 + LANES] = (mixed * ps_ref[:, c0:c0 + LANES]).astype(BF16)

    full = lambda a: pl.BlockSpec(a.shape, lambda i: (0,) * a.ndim)
    return pl.pallas_call(
        body, grid=(S // tm,),
        in_specs=[pl.BlockSpec((tm, DIN), lambda i: (i, 0)),
                  pl.BlockSpec((HB, DIN), lambda i: (jnp.maximum(i * r - 1, 0), 0)),
                  full(wa), full(wb), full(bb), full(lg), full(lb), full(pw), full(ps)],
        out_specs=[pl.BlockSpec((tm, DMIX), lambda i: (i, 0)), pl.BlockSpec((tm, WB), lambda i: (i, 0))],
        out_shape=[jax.ShapeDtypeStruct((S, DMIX), BF16), jax.ShapeDtypeStruct((S, WB), F32)],
        scratch_shapes=[pltpu.VMEM((HB + tm, WA), F32), pltpu.VMEM((HB + tm, WB), F32),
                        pltpu.VMEM((8, tm + 24, LANES), F32), pltpu.VMEM((HB + tm, WC), F32),
                        pltpu.VMEM((tm, WC), BF16)],
        compiler_params=_cp(1), name=name,
    )(u, u, wa, wb, bb, lg, lb, pw, ps)


def mixer_bwd(u, cb, dy, wa, wb, bb, lg, lb, pw, ps, *, name):
    S, DIN = u.shape
    WA, WB, WC = wa.shape[1], wb.shape[1], ps.shape[1]
    NG = WC // LANES
    DMIX = WA + WB + WC
    tm = _tile(S, TM_MIXER, HALO_MIX)
    HB = HALO_MIX
    r = tm // HB
    nI = S // tm
    nH = S // HB
    oCg, oVa, oVal, oGate, oC = WA, 2 * WA, 3 * WA, 3 * WA + WB, 3 * WA + 2 * WB
    offs_a = [k - (K_A - 1) for k in range(K_A)]
    offs_b = [k - (K_B - 1) for k in range(K_B)]
    adj_a = [(K_A - 1) - k for k in range(K_A)]
    adj_b = [(K_B - 1) - k for k in range(K_B)]

    def body(u_ref, ub_ref, ua_ref, cb_ref, cba_ref, dy_ref, dya_ref,
             wa_ref, wb_ref, bb_ref, lg_ref, lb_ref, pw_ref, ps_ref,
             du_ref, dwa_ref, dwb_ref, dbb_ref, dlg_ref, dlb_ref, dpw_ref, dps_ref,
             pbuf, dqbuf, gbuf, dcbuf, shbuf, xbuf, plbuf, dmbuf, dplbuf, dpcbuf):
        i = pl.program_id(0)
        first = i == 0
        hvb = jnp.where(i > 0, 1.0, 0.0).astype(F32)
        hva = jnp.where(i < nI - 1, 1.0, 0.0).astype(F32)
        _zero_first(first, dwa_ref, dwb_ref, dbb_ref, dlg_ref, dlb_ref, dpw_ref, dps_ref)

        def fill(src, dst0, n, scale, main):
            for r0, rc in _chunks(n):
                rows, drows = slice(r0, r0 + rc), slice(dst0 + r0, dst0 + r0 + rc)
                for c0 in range(0, WA, LANES):
                    v = (src[rows, oCg + c0:oCg + c0 + LANES].astype(F32)
                         * src[rows, oVa + c0:oVa + c0 + LANES].astype(F32))
                    pbuf[drows, c0:c0 + LANES] = v if scale is None else v * scale
                for c0 in range(0, WC, LANES):
                    v = src[rows, oC + c0:oC + c0 + LANES].astype(F32)
                    xbuf[drows, c0:c0 + LANES] = v if scale is None else v * scale
                if main:
                    for c0 in range(0, WB, LANES):
                        gbuf[rows, c0:c0 + LANES] = (
                            src[rows, oVal + c0:oVal + c0 + LANES].astype(F32)
                            * _sigmoid(src[rows, oGate + c0:oGate + c0 + LANES].astype(F32)))

        fill(ub_ref, 0, HB, hvb, False)
        fill(u_ref, HB, tm, None, True)

        for r0, rc in _chunks(tm):
            for c0 in range(0, WA, LANES):
                dqbuf[r0:r0 + rc, c0:c0 + LANES] = (dy_ref[r0:r0 + rc, c0:c0 + LANES].astype(F32)
                                                     * u_ref[r0:r0 + rc, c0:c0 + LANES].astype(F32))
        for c0 in range(0, WA, LANES):
            dqbuf[tm:tm + HB, c0:c0 + LANES] = (dya_ref[:, c0:c0 + LANES].astype(F32)
                                                * ua_ref[:, c0:c0 + LANES].astype(F32)) * hva
        for c0 in range(0, WA, LANES):
            w = [wa_ref[k:k + 1, c0:c0 + LANES] for k in range(K_A)]
            for r0, rc in _chunks(tm):
                rows = slice(r0, r0 + rc)
                q = _tap_sum(pbuf, HB, offs_a, w, r0, rc, c0)
                du_ref[rows, c0:c0 + LANES] = (dy_ref[rows, c0:c0 + LANES].astype(F32) * q).astype(BF16)
                dp = _tap_sum(dqbuf, 0, adj_a, w, r0, rc, c0)
                cg = u_ref[rows, oCg + c0:oCg + c0 + LANES].astype(F32)
                va = u_ref[rows, oVa + c0:oVa + c0 + LANES].astype(F32)
                du_ref[rows, oCg + c0:oCg + c0 + LANES] = (dp * va).astype(BF16)
                du_ref[rows, oVa + c0:oVa + c0 + LANES] = (dp * cg).astype(BF16)
            _tap_wgrad(dwa_ref, 0, dqbuf, 0, pbuf, HB, offs_a, tm, c0, first)

        def ln_chunk(r0, rc, cb, dyb, scale, main):
            mu = jnp.mean(cb, axis=-1, keepdims=True)
            d = cb - mu
            rs = lax.rsqrt(jnp.mean(d * d, axis=-1, keepdims=True) + LN_EPS)
            n = d * rs
            z = n * lg_ref[...] + lb_ref[...]
            sg = _sigmoid(z)
            dz = dyb * (sg * (1.0 + z * (1.0 - sg)))
            dn = dz * lg_ref[...]
            dcb = rs * (dn - jnp.mean(dn, axis=-1, keepdims=True) - n * jnp.mean(dn * n, axis=-1, keepdims=True))
            if scale is not None:
                dcb = dcb * scale
            dcbuf[r0:r0 + rc, :] = dcb
            if main:
                return (jnp.sum(dz * n, axis=0, keepdims=True), jnp.sum(dz, axis=0, keepdims=True),
                        jnp.sum(dcb, axis=0, keepdims=True))
            return None

        sums = None
        for r0, rc in _chunks(tm, 32):
            part = ln_chunk(r0, rc, cb_ref[r0:r0 + rc, :], dy_ref[r0:r0 + rc, WA:WA + WB].astype(F32), None, True)
            sums = part if sums is None else tuple(a + b for a, b in zip(sums, part))
        ln_chunk(tm, HB, cba_ref[...], dya_ref[:, WA:WA + WB].astype(F32), hva, False)
        _acc_store(dlg_ref, (slice(None), slice(None)), sums[0], first)
        _acc_store(dlb_ref, (slice(None), slice(None)), sums[1], first)
        _acc_store(dbb_ref, (slice(None), slice(None)), sums[2], first)

        for c0 in range(0, WB, LANES):
            w = [wb_ref[k:k + 1, c0:c0 + LANES] for k in range(K_B)]
            _shift_copies(shbuf, dcbuf, tm + 24, range(1, 8), c0)
            for r0, rc in _chunks(tm):
                rows = slice(r0, r0 + rc)
                dglu = None
                for k in range(K_B):
                    t = _tap(shbuf, dcbuf, adj_b[k], r0, rc, c0) * w[k]
                    dglu = t if dglu is None else dglu + t
                val = u_ref[rows, oVal + c0:oVal + c0 + LANES].astype(F32)
                sg = _sigmoid(u_ref[rows, oGate + c0:oGate + c0 + LANES].astype(F32))
                du_ref[rows, oVal + c0:oVal + c0 + LANES] = (dglu * sg).astype(BF16)
                du_ref[rows, oGate + c0:oGate + c0 + LANES] = (dglu * val * sg * (1.0 - sg)).astype(BF16)
            for k in range(K_B):
                acc = None
                for r0, rc in _chunks(tm):
                    t = _tap(shbuf, dcbuf, adj_b[k], r0, rc, c0) * gbuf[r0:r0 + rc, c0:c0 + LANES]
                    acc = t if acc is None else acc + t
                _acc_store(dwb_ref, (slice(k, k + 1), slice(c0, c0 + LANES)),
                           jnp.sum(acc, axis=0, keepdims=True), first)

        for g, win in enumerate(POOL_WINDOWS):
            c0 = g * LANES
            cols = slice(c0, c0 + LANES)
            ycols = slice(WA + WB + c0, WA + WB + c0 + LANES)
            for r0, rc in _chunks(tm):
                s = _tap_sum(xbuf, HB, [-j for j in range(win)], None, r0, rc, c0)
                pooled = s / _row_counts(i * tm + r0, rc, win) - xbuf[HB + r0:HB + r0 + rc, cols]
                plbuf[r0:r0 + rc, cols] = pooled.astype(BF16)
            mixed = jnp.dot(plbuf[:, cols], pw_ref[g], preferred_element_type=F32)
            dyc = dy_ref[:, ycols].astype(F32)
            _acc_store(dps_ref, (slice(None), cols), jnp.sum(dyc * mixed, axis=0, keepdims=True), first)
            dmbuf[0:tm, :] = (dyc * ps_ref[:, cols]).astype(BF16)
            dmbuf[tm:tm + HB, :] = (dya_ref[:, ycols].astype(F32) * ps_ref[:, cols] * hva).astype(BF16)
            dpw = lax.dot_general(plbuf[:, cols], dmbuf[0:tm, :], (((0,), (0,)), ((), ())),
                                  preferred_element_type=F32)
            _acc_store(dpw_ref, (g, slice(None), slice(None)), dpw, first)
            dplbuf[...] = lax.dot_general(dmbuf[...], pw_ref[g], (((1,), (1,)), ((), ())),
                                          preferred_element_type=F32)
            for r0, rc in _chunks(tm + HB):
                dpcbuf[r0:r0 + rc, :] = dplbuf[r0:r0 + rc, :] / _row_counts(i * tm + r0, rc, win)
            for r0, rc in _chunks(tm):
                duc = _tap_sum(dpcbuf, 0, list(range(win)), None, r0, rc, 0) - dplbuf[r0:r0 + rc, :]
                du_ref[r0:r0 + rc, oC + c0:oC + c0 + LANES] = duc.astype(BF16)

    full = lambda a: pl.BlockSpec(a.shape, lambda i: (0,) * a.ndim)
    acc = lambda shape: pl.BlockSpec(shape, lambda i: (0,) * len(shape))
    small = [(K_A, WA), (K_B, WB), (1, WB), (1, WB), (1, WB), (NG, LANES, LANES), (1, WC)]
    outs = pl.pallas_call(
        body, grid=(nI,),
        in_specs=[pl.BlockSpec((tm, DIN), lambda i: (i, 0)),
                  pl.BlockSpec((HB, DIN), lambda i: (jnp.maximum(i * r - 1, 0), 0)),
                  pl.BlockSpec((HB, DIN), lambda i: (jnp.minimum((i + 1) * r, nH - 1), 0)),
                  pl.BlockSpec((tm, WB), lambda i: (i, 0)),
                  pl.BlockSpec((HB, WB), lambda i: (jnp.minimum((i + 1) * r, nH - 1), 0)),
                  pl.BlockSpec((tm, DMIX), lambda i: (i, 0)),
                  pl.BlockSpec((HB, DMIX), lambda i: (jnp.minimum((i + 1) * r, nH - 1), 0)),
                  full(wa), full(wb), full(bb), full(lg), full(lb), full(pw), full(ps)],
        out_specs=[pl.BlockSpec((tm, DIN), lambda i: (i, 0))] + [acc(s) for s in small],
        out_shape=[jax.ShapeDtypeStruct((S, DIN), BF16)] + [jax.ShapeDtypeStruct(s, F32) for s in small],
        scratch_shapes=[pltpu.VMEM((HB + tm, WA), F32), pltpu.VMEM((tm + HB, WA), F32),
                        pltpu.VMEM((tm, WB), F32), pltpu.VMEM((tm + HB, WB), F32),
                        pltpu.VMEM((8, tm + 24, LANES), F32), pltpu.VMEM((HB + tm, WC), F32),
                        pltpu.VMEM((tm, WC), BF16), pltpu.VMEM((tm + HB, LANES), BF16),
                        pltpu.VMEM((tm + HB, LANES), F32), pltpu.VMEM((tm + HB, LANES), F32)],
        compiler_params=_cp(1), name=name,
    )(u, u, u, cb, cb, dy, dy, wa, wb, bb, lg, lb, pw, ps)
    return outs


def ffn_fwd(up, wf, bf, *, name):
    S, F2 = up.shape
    F = F2 // 2
    tm = _tile(S, TM_FFN, HALO_FFN)
    HB = HALO_FFN
    r = tm // HB
    CW = _tile(F, 512)
    offs = [k - (K_F - 1) for k in range(K_F)]

    def body(up_ref, uph_ref, wf_ref, bf_ref, a_ref, upc_ref, ebuf):
        i = pl.program_id(0)
        hv = jnp.where(i > 0, 1.0, 0.0).astype(F32)
        for c0 in range(0, F, CW):
            for h, off in ((0, c0), (1, F + c0)):
                ebuf[h, 0:HB, :] = uph_ref[:, off:off + CW].astype(F32) * hv
                for r0, rc in _chunks(tm):
                    ebuf[h, HB + r0:HB + r0 + rc, :] = up_ref[r0:r0 + rc, off:off + CW].astype(F32)
            for l0 in range(0, CW, LANES):
                cg, cv = c0 + l0, F + c0 + l0
                wg = [wf_ref[k:k + 1, cg:cg + LANES] for k in range(K_F)]
                wv = [wf_ref[k:k + 1, cv:cv + LANES] for k in range(K_F)]
                bg = bf_ref[:, cg:cg + LANES]
                bv = bf_ref[:, cv:cv + LANES]
                for r0, rc in _chunks(tm):
                    gt = _tap_sum(ebuf.at[0], HB, offs, wg, r0, rc, l0) + bg
                    vl = _tap_sum(ebuf.at[1], HB, offs, wv, r0, rc, l0) + bv
                    a_ref[r0:r0 + rc, cg:cg + LANES] = (gt * _sigmoid(gt) * vl).astype(BF16)
                    upc_ref[r0:r0 + rc, cg:cg + LANES] = gt.astype(BF16)
                    upc_ref[r0:r0 + rc, cv:cv + LANES] = vl.astype(BF16)

    full = lambda a: pl.BlockSpec(a.shape, lambda i: (0,) * a.ndim)
    return pl.pallas_call(
        body, grid=(S // tm,),
        in_specs=[pl.BlockSpec((tm, F2), lambda i: (i, 0)),
                  pl.BlockSpec((HB, F2), lambda i: (jnp.maximum(i * r - 1, 0), 0)),
                  full(wf), full(bf)],
        out_specs=[pl.BlockSpec((tm, F), lambda i: (i, 0)), pl.BlockSpec((tm, F2), lambda i: (i, 0))],
        out_shape=[jax.ShapeDtypeStruct((S, F), BF16), jax.ShapeDtypeStruct((S, F2), BF16)],
        scratch_shapes=[pltpu.VMEM((2, HB + tm, CW), F32)],
        compiler_params=_cp(1), name=name,
    )(up, up, wf, bf)


def ffn_bwd(up, upc, da, wf, *, name):
    S, F2 = up.shape
    F = F2 // 2
    tm = _tile(S, TM_FFN_BWD, HALO_FFN)
    HB = HALO_FFN
    r = tm // HB
    nI = S // tm
    nH = S // HB
    CW = _tile(F, 512)
    adj = [(K_F - 1) - k for k in range(K_F)]

    def body(up_ref, upc_ref, upca_ref, da_ref, daa_ref, wf_ref, dup_ref, dwf_ref, dbf_ref, dbuf, shbuf):
        i = pl.program_id(0)
        first = i == 0
        hva = jnp.where(i < nI - 1, 1.0, 0.0).astype(F32)
        _zero_first(first, dwf_ref, dbf_ref)
        for c0 in range(0, F, CW):
            for l0 in range(0, CW, LANES):
                cg, cv = c0 + l0, F + c0 + l0
                wg = [wf_ref[k:k + 1, cg:cg + LANES] for k in range(K_F)]
                wv = [wf_ref[k:k + 1, cv:cv + LANES] for k in range(K_F)]
                sg_sum, sv_sum = None, None
                for r0, rc in _chunks(tm + HB):
                    if r0 < tm:
                        gt = upc_ref[r0:r0 + rc, cg:cg + LANES].astype(F32)
                        vl = upc_ref[r0:r0 + rc, cv:cv + LANES].astype(F32)
                        d = da_ref[r0:r0 + rc, cg:cg + LANES].astype(F32)
                    else:
                        gt = upca_ref[:, cg:cg + LANES].astype(F32)
                        vl = upca_ref[:, cv:cv + LANES].astype(F32)
                        d = daa_ref[:, cg:cg + LANES].astype(F32) * hva
                    s = _sigmoid(gt)
                    dg = d * vl * (s * (1.0 + gt * (1.0 - s)))
                    dv = d * (gt * s)
                    dbuf[0, r0:r0 + rc, :] = dg
                    dbuf[1, r0:r0 + rc, :] = dv
                    if r0 < tm:
                        pg, pv = jnp.sum(dg, axis=0, keepdims=True), jnp.sum(dv, axis=0, keepdims=True)
                        sg_sum = pg if sg_sum is None else sg_sum + pg
                        sv_sum = pv if sv_sum is None else sv_sum + pv
                _acc_store(dbf_ref, (slice(None), slice(cg, cg + LANES)), sg_sum, first)
                _acc_store(dbf_ref, (slice(None), slice(cv, cv + LANES)), sv_sum, first)
                for h, w, col in ((0, wg, cg), (1, wv, cv)):
                    d_h, sh_h = dbuf.at[h], shbuf.at[h]
                    _shift_copies(sh_h, d_h, tm, (1, 2), 0)
                    accs = [None] * K_F
                    for r0, rc in _chunks(tm):
                        taps = [_tap(sh_h, d_h, adj[k], r0, rc, 0) for k in range(K_F)]
                        dup_ref[r0:r0 + rc, col:col + LANES] = (
                            taps[0] * w[0] + taps[1] * w[1] + taps[2] * w[2]).astype(BF16)
                        uv = up_ref[r0:r0 + rc, col:col + LANES].astype(F32)
                        for k in range(K_F):
                            t = taps[k] * uv
                            accs[k] = t if accs[k] is None else accs[k] + t
                    for k in range(K_F):
                        _acc_store(dwf_ref, (slice(k, k + 1), slice(col, col + LANES)),
                                   jnp.sum(accs[k], axis=0, keepdims=True), first)

    full = lambda a: pl.BlockSpec(a.shape, lambda i: (0,) * a.ndim)
    return pl.pallas_call(
        body, grid=(nI,),
        in_specs=[pl.BlockSpec((tm, F2), lambda i: (i, 0)),
                  pl.BlockSpec((tm, F2), lambda i: (i, 0)),
                  pl.BlockSpec((HB, F2), lambda i: (jnp.minimum((i + 1) * r, nH - 1), 0)),
                  pl.BlockSpec((tm, F), lambda i: (i, 0)),
                  pl.BlockSpec((HB, F), lambda i: (jnp.minimum((i + 1) * r, nH - 1), 0)),
                  full(wf)],
        out_specs=[pl.BlockSpec((tm, F2), lambda i: (i, 0)),
                   pl.BlockSpec((K_F, F2), lambda i: (0, 0)), pl.BlockSpec((1, F2), lambda i: (0, 0))],
        out_shape=[jax.ShapeDtypeStruct((S, F2), BF16), jax.ShapeDtypeStruct((K_F, F2), F32),
                   jax.ShapeDtypeStruct((1, F2), F32)],
        scratch_shapes=[pltpu.VMEM((2, tm + HB, LANES), F32), pltpu.VMEM((2, 3, tm, LANES), F32)],
        compiler_params=_cp(1), name=name,
    )(up, upc, upc, da, da, wf)


HBM = pl.BlockSpec(memory_space=pltpu.HBM)


def _place():
    return lax.axis_index("x"), lax.axis_index("y"), lax.axis_index("c")


def allgather8(buf, *, name):
    R, C = buf.shape

    def body(x_ref, o_ref, send_sems, recv_sems, local_sem):
        x, y, c = _place()
        me = 4 * x + 2 * y + c
        mine = pltpu.make_async_copy(x_ref, o_ref.at[me], local_sem)
        mine.start()
        sends = []
        for k in range(1, 8):
            fx, fy, fc = (k >> 2) & 1, (k >> 1) & 1, k & 1
            px, py, pc = (x + fx) % 2, (y + fy) % 2, (c + fc) % 2
            cp = pltpu.make_async_remote_copy(
                src_ref=x_ref, dst_ref=o_ref.at[me], send_sem=send_sems.at[k - 1], recv_sem=recv_sems.at[k - 1],
                device_id=(px, py, pc), device_id_type=MESH)
            cp.start()
            sends.append(cp)
        for k in range(1, 8):
            fx, fy, fc = (k >> 2) & 1, (k >> 1) & 1, k & 1
            peer = 4 * ((x + fx) % 2) + 2 * ((y + fy) % 2) + (c + fc) % 2
            pltpu.make_async_remote_copy(
                src_ref=x_ref, dst_ref=o_ref.at[peer], send_sem=send_sems.at[k - 1], recv_sem=recv_sems.at[k - 1],
                device_id=(x, y, c), device_id_type=MESH).wait_recv()
        for cp in sends:
            cp.wait_send()
        mine.wait()

    return pl.pallas_call(
        body, in_specs=[HBM], out_specs=HBM, out_shape=jax.ShapeDtypeStruct((8, R, C), buf.dtype),
        scratch_shapes=[pltpu.SemaphoreType.DMA((7,)), pltpu.SemaphoreType.DMA((7,)), pltpu.SemaphoreType.DMA],
        name=name,
    )(buf)


def gather_chip_shards(shards, *, name):
    n = len(shards)

    def body(*refs):
        ins, outs = refs[:n], refs[n:2 * n]
        send_sems, recv_sems = refs[2 * n:]
        x, y, c = _place()
        b = 2 * x + y
        chips = [(1 - x, y), (x, 1 - y), (1 - x, 1 - y)]
        sends = []

        def half(a, which):
            rh = ins[a].shape[0] // 2
            return pl.ds(pl.multiple_of(which * rh, 16), rh)

        def copy(a, k, src, dst, to):
            return pltpu.make_async_remote_copy(
                src_ref=src, dst_ref=dst, send_sem=send_sems.at[7 * a + k], recv_sem=recv_sems.at[7 * a + k],
                device_id=to, device_id_type=MESH)

        for a in range(n):
            for j, (cx, cy) in enumerate(chips):
                cp = copy(a, j, ins[a].at[half(a, c)], outs[a].at[b, half(a, c)], (cx, cy, c))
                cp.start()
                sends.append(cp)
        for a in range(n):
            cp = copy(a, 6, ins[a], outs[a].at[b], (x, y, 1 - c))
            cp.start()
            sends.append(cp)
        for a in range(n):
            for j, (cx, cy) in enumerate(chips):
                got = outs[a].at[2 * cx + cy, half(a, c)]
                copy(a, j, got, got, (x, y, c)).wait_recv()
                cp = copy(a, 3 + j, got, got, (x, y, 1 - c))
                cp.start()
                sends.append(cp)
        for a in range(n):
            for j, (cx, cy) in enumerate(chips):
                got = outs[a].at[2 * cx + cy, half(a, 1 - c)]
                copy(a, 3 + j, got, got, (x, y, c)).wait_recv()
            copy(a, 6, ins[a], outs[a].at[b], (x, y, c)).wait_recv()
        for cp in sends:
            cp.wait_send()

    return pl.pallas_call(
        body, in_specs=[HBM] * n, out_specs=[HBM] * n,
        out_shape=[jax.ShapeDtypeStruct((4,) + s.shape, s.dtype) for s in shards],
        scratch_shapes=[pltpu.SemaphoreType.DMA((7 * n,)), pltpu.SemaphoreType.DMA((7 * n,))],
        name=name,
    )(*shards)


def sibling_swap_halves(gs, *, name):
    n = len(gs)

    def body(*refs):
        ins, outs = refs[:n], refs[n:2 * n]
        send_sems, recv_sems = refs[2 * n:]
        x, y, c = _place()
        cps = []
        for a in range(n):
            rh = ins[a].shape[1] // 2
            src = ins[a].at[:, pl.ds(pl.multiple_of((1 - c) * rh, 16), rh)]
            cp = pltpu.make_async_remote_copy(
                src_ref=src, dst_ref=outs[a], send_sem=send_sems.at[a], recv_sem=recv_sems.at[a],
                device_id=(x, y, 1 - c), device_id_type=MESH)
            cp.start()
            cps.append(cp)
        for cp in cps:
            cp.wait()

    return pl.pallas_call(
        body, in_specs=[HBM] * n, out_specs=[HBM] * n,
        out_shape=[jax.ShapeDtypeStruct((4, g.shape[1] // 2, g.shape[2]), g.dtype) for g in gs],
        scratch_shapes=[pltpu.SemaphoreType.DMA((n,)), pltpu.SemaphoreType.DMA((n,))],
        name=name,
    )(*gs)


SEM =pl.BlockSpec(memory_space=pltpu.SEMAPHORE)
SPLIT_COPY = pltpu.CompilerParams(has_side_effects=pltpu.SideEffectType.DATAFLOW_SIDE_EFFECTING)


def _split_start(srcs, lands, n_copies, issue, *, name):
    n, m = len(srcs), len(lands)

    def body(*refs):
        ins, lnd = refs[:n], refs[n:n + m]
        send_sems, recv_sems = refs[n + m], refs[n + m + 1]
        token = refs[-1]

        def copy(k, src, dst, to):
            return pltpu.make_async_remote_copy(src_ref=src, dst_ref=dst, send_sem=send_sems.at[k],
                                                recv_sem=recv_sems.at[k], device_id=to, device_id_type=MESH)

        for cp in issue(ins, lnd, copy):
            cp.start()
        token[...] = jnp.zeros_like(token)

    outs = pl.pallas_call(
        body, name=name,
        out_shape=(pltpu.SemaphoreType.DMA((n_copies,)), pltpu.SemaphoreType.DMA((n_copies,)),
                   *[pltpu.HBM(a.shape, a.dtype) for a in list(srcs) + list(lands)],
                   jax.ShapeDtypeStruct((8, LANES), F32)),
        in_specs=[HBM] * (n + m),
        out_specs=(SEM, SEM, *([HBM] * (n + m)), pl.BlockSpec(memory_space=pltpu.VMEM)),
        input_output_aliases={i: 2 + i for i in range(n + m)},
        compiler_params=SPLIT_COPY,
    )(*[pltpu.with_memory_space_constraint(a, pltpu.HBM) for a in list(srcs) + list(lands)])
    return outs[0], outs[1], list(outs[2:2 + n]), list(outs[2 + n:2 + n + m]), outs[-1]


def _split_wait(send_sems, recv_sems, srcs, lands, after, issue, *, name):
    n, m = len(srcs), len(lands)

    def body(*refs):
        ins, lnd = refs[:n], refs[n:n + m]
        s_sems, r_sems = refs[n + m], refs[n + m + 1]

        def copy(k, src, dst, to):
            return pltpu.make_async_remote_copy(src_ref=src, dst_ref=dst, send_sem=s_sems.at[k],
                                                recv_sem=r_sems.at[k], device_id=to, device_id_type=MESH)

        for cp in issue(ins, lnd, copy):
            cp.wait_send()
            cp.wait_recv()

    outs = pl.pallas_call(
        body, name=name,
        out_shape=tuple(pltpu.HBM(a.shape, a.dtype) for a in list(srcs) + list(lands)),
        in_specs=[HBM] * (n + m) + [SEM, SEM, pl.BlockSpec(memory_space=pl.ANY)],
        out_specs=tuple([HBM] * (n + m)),
        input_output_aliases={i: i for i in range(n + m)},
        compiler_params=SPLIT_COPY,
    )(*srcs, *lands, send_sems, recv_sems, after)
    return list(outs[:n]), list(outs[n:n + m])


def _gather_direct_copies(received):
    def issue(ins, lnd, copy):
        x, y, c = _place()
        b = 2 * x + y
        chips = [(1 - x, y), (x, 1 - y), (1 - x, 1 - y)]
        cps = []
        for a in range(len(ins)):
            for j, (cx, cy) in enumerate(chips):
                slot = 2 * cx + cy if received else b
                cps.append(copy(4 * a + j, ins[a], lnd[a].at[slot], (cx, cy, c)))
            cps.append(copy(4 * a + 3, ins[a], lnd[a].at[b], (x, y, 1 - c)))
        return cps
    return issue


def _exchange_copies(received):
    def issue(ins, lnd, copy):
        x, y, c = _place()
        b = 2 * x + y
        chips = [(1 - x, y), (x, 1 - y), (1 - x, 1 - y)]
        cps = []
        for a in range(len(ins)):
            for j, (cx, cy) in enumerate(chips):
                slot = 2 * cx + cy if received else b
                cps.append(copy(3 * a + j, ins[a].at[2 * cx + cy], lnd[a].at[slot], (cx, cy, c)))
        return cps
    return issue


def sibling_join_halves(fs, *, name):
    n = len(fs)

    def body(*refs):
        ins, outs = refs[:n], refs[n:2 * n]
        send_sems, recv_sems = refs[2 * n:]
        x, y, c = _place()
        sends = []
        for a in range(n):
            rh = ins[a].shape[0] // 2
            mine = pl.ds(pl.multiple_of(c * rh, 8), rh)
            cp = pltpu.make_async_remote_copy(
                src_ref=ins[a].at[mine], dst_ref=outs[a].at[mine], send_sem=send_sems.at[a],
                recv_sem=recv_sems.at[a], device_id=(x, y, 1 - c), device_id_type=MESH)
            cp.start()
            sends.append(cp)
        for a in range(n):
            rh = ins[a].shape[0] // 2
            other = pl.ds(pl.multiple_of((1 - c) * rh, 8), rh)
            pltpu.make_async_remote_copy(
                src_ref=ins[a].at[other], dst_ref=outs[a].at[other], send_sem=send_sems.at[a],
                recv_sem=recv_sems.at[a], device_id=(x, y, c), device_id_type=MESH).wait_recv()
        for cp in sends:
            cp.wait_send()

    return pl.pallas_call(
        body, in_specs=[HBM] * n, out_specs=[HBM] * n,
        out_shape=[jax.ShapeDtypeStruct(f.shape, f.dtype) for f in fs],
        input_output_aliases={a: a for a in range(n)},
        scratch_shapes=[pltpu.SemaphoreType.DMA((n,)), pltpu.SemaphoreType.DMA((n,))],
        name=name,
    )(*fs)


def pair_add(g, t, core, *, name):
    _, R, C = g.shape
    rh = R // 2
    tr = _tile(rh, 256, 16)
    nh = rh // tr

    def body(c_ref, g_ref, t_ref, o_ref):
        o_ref[...] = (g_ref[...].astype(F32) + t_ref[...].astype(F32)).astype(BF16)

    return pl.pallas_call(
        body,
        grid_spec=pltpu.PrefetchScalarGridSpec(
            num_scalar_prefetch=1, grid=(4, nh),
            in_specs=[pl.BlockSpec((None, tr, C), lambda b, i, c_ref: (b, c_ref[0] * nh + i, 0)),
                      pl.BlockSpec((None, tr, C), lambda b, i, c_ref: (b, i, 0))],
            out_specs=pl.BlockSpec((None, tr, C), lambda b, i, c_ref: (b, i, 0))),
        out_shape=jax.ShapeDtypeStruct((4, rh, C), BF16), compiler_params=_cp(2), name=name,
    )(core, g, t)


def sum_slots(q, *, name):
    N, R, C = q.shape
    tr = _tile(R, 256, 16)

    def body(q_ref, o_ref):
        acc = q_ref[0].astype(F32)
        for s in range(1, N):
            acc = acc + q_ref[s].astype(F32)
        o_ref[...] = acc

    return pl.pallas_call(
        body, grid=(R // tr,), in_specs=[pl.BlockSpec((N, tr, C), lambda i: (0, i, 0))],
        out_specs=pl.BlockSpec((tr, C), lambda i: (i, 0)),
        out_shape=jax.ShapeDtypeStruct((R, C), F32), compiler_params=_cp(1), name=name,
    )(q)


def sum_own_and_received(p, q, place, *, name):
    _, rh, C = p.shape
    tr = _tile(rh, 256, 16)
    nh = rh // tr

    def body(s0, s1, s2, s3, cr, p_ref, q1_ref, q2_ref, q3_ref, o_ref):
        o_ref[...] = (p_ref[...].astype(F32) + q1_ref[...].astype(F32)
                      + q2_ref[...].astype(F32) + q3_ref[...].astype(F32))

    def slot(d):
        return pl.BlockSpec((None, tr, C), lambda i, *pc: (pc[d][0], i, 0))

    return pl.pallas_call(
        body,
        grid_spec=pltpu.PrefetchScalarGridSpec(
            num_scalar_prefetch=5, grid=(nh,),
            in_specs=[slot(0), slot(1), slot(2), slot(3)],
            out_specs=pl.BlockSpec((tr, C), lambda i, *pc: (pc[4][0] * nh + i, 0))),
        out_shape=jax.ShapeDtypeStruct((2 * rh, C), F32), compiler_params=_cp(1), name=name,
    )(*place, p, q, q, q)


def _adam_math(w, g, m, v):
    m = ADAM_B1 * m + (1.0 - ADAM_B1) * g
    v = ADAM_B2 * v + (1.0 - ADAM_B2) * (g * g)
    m_hat = m / (1.0 - ADAM_B1 ** ADAM_STEP)
    v_hat = v / (1.0 - ADAM_B2 ** ADAM_STEP)
    delta = -ADAM_LR * (m_hat / (jnp.sqrt(v_hat) + ADAM_EPS) + ADAM_WD * w)
    return delta, m, v


def adam_stacked(w, m, v, grads, *, first_layer=0, earlier=None, after=None, name):
    L, R, C = w.shape
    n = len(grads)
    tr = _tile(R, max(8, ADAM_BLOCK_BYTES // (4 * C)), 8)
    nr = R // tr
    tail = list(earlier) if earlier is not None else []
    if after is not None:
        tail.append(after)
    n_in = 3 + n + len(tail)

    def body(*refs):
        w_ref, m_ref, v_ref = refs[:3]
        g_refs = refs[3:3 + n]
        go_ref, d_ref, mo_ref, vo_ref = refs[n_in:]
        lid = pl.program_id(0)
        for l in range(n):
            @pl.when(lid == l)
            def _(l=l):
                g = g_refs[l][...]
                d, mn, vn = _adam_math(w_ref[...], g, m_ref[...], v_ref[...])
                go_ref[...] = g
                d_ref[...] = d
                mo_ref[...] = mn
                vo_ref[...] = vn

    st = pl.BlockSpec((None, tr, C), lambda l, i: (l + first_layer, i, 0))
    g_specs = [pl.BlockSpec((tr, C), functools.partial(lambda l, i, ll: (jnp.where(l == ll, i, 0), 0), ll=ll))
               for ll in range(n)]
    extra = [pl.BlockSpec(memory_space=pl.ANY)] * len(tail)
    return pl.pallas_call(
        body, grid=(n, nr), in_specs=[st, st, st] + g_specs + extra, out_specs=[st] * 4,
        out_shape=[jax.ShapeDtypeStruct((L, R, C), F32)] * 4,
        input_output_aliases={3 + n + k: k for k in range(4)} if earlier is not None else {},
        compiler_params=_cp(2), name=name,
    )(w, m, v, *grads, *tail)


def adam_flat(w, g, m, v, *, name):
    R, C = w.shape
    tr = _tile(R, 512, 8)

    def body(w_ref, g_ref, m_ref, v_ref, d_ref, mo_ref, vo_ref):
        d, mn, vn = _adam_math(w_ref[...], g_ref[...], m_ref[...], v_ref[...])
        d_ref[...] = d
        mo_ref[...] = mn
        vo_ref[...] = vn

    row = pl.BlockSpec((tr, C), lambda i: (i, 0))
    return pl.pallas_call(
        body, grid=(R // tr,), in_specs=[row] * 4, out_specs=[row] * 3,
        out_shape=[jax.ShapeDtypeStruct((R, C), F32)] * 3, compiler_params=_cp(1), name=name,
    )(w, g, m, v)


PACK_ROWS = 64


def _pack(arrays):
    flat = jnp.concatenate([a.reshape(-1).astype(F32) for a in arrays])
    n = flat.shape[0]
    unit = PACK_ROWS * LANES
    pad = (-n) % unit
    return jnp.pad(flat, (0, pad)).reshape(-1, LANES)


def _unpack(buf, shapes):
    flat = buf.reshape(-1)
    out, o = [], 0
    for s in shapes:
        n = 1
        for d in s:
            n *= d
        out.append(flat[o:o + n].reshape(s))
        o += n
    return out


def kernel(x, norm_mix_pre, norm_mix_post, norm_ffn_pre, norm_ffn_post, w_in, conv_a_w, conv_b_w, conv_b_bias, ln_b_gain, ln_b_bias, pool_w, pool_scale, w_out, w_up, conv_ffn_w, conv_ffn_bias, w_down, loss_target, m_norm_mix_pre, m_norm_mix_post, m_norm_ffn_pre, m_norm_ffn_post, m_w_in, m_conv_a_w, m_conv_b_w, m_conv_b_bias, m_ln_b_gain, m_ln_b_bias, m_pool_w, m_pool_scale, m_w_out, m_w_up, m_conv_ffn_w, m_conv_ffn_bias, m_w_down, v_norm_mix_pre, v_norm_mix_post, v_norm_ffn_pre, v_norm_ffn_post, v_w_in, v_conv_a_w, v_conv_b_w, v_conv_b_bias, v_ln_b_gain, v_ln_b_bias, v_pool_w, v_pool_scale, v_w_out, v_w_up, v_conv_ffn_w, v_conv_ffn_bias, v_w_down):
    L = w_in.shape[0]
    S, D = x.shape[1], x.shape[2]
    WA, WB, WC = 4 * conv_a_w.shape[2], 4 * conv_b_w.shape[2], pool_scale.shape[1]
    DIN, DMIX, F2 = 4 * w_in.shape[2], 4 * w_out.shape[1], 4 * w_up.shape[2]
    F = F2 // 2
    NG = WC // LANES
    xi, yi, ci = _place()
    chip = 2 * xi + yi
    core = jnp.reshape(ci, (1,)).astype(jnp.int32)

    conv_shapes = [(L, K_A, WA // 4), (L, K_B, WB // 4), (L, K_F, F2 // 4)]
    conv_all = allgather8(_pack([conv_a_w, conv_b_w, conv_ffn_w]), name="gather_conv_taps")
    per_chip = [_unpack(conv_all[2 * b], conv_shapes) for b in range(4)]
    wa_full, wb_full, wf_full = [jnp.concatenate([per_chip[b][k] for b in range(4)], axis=2) for k in range(3)]
    pw_bf = pool_w.astype(BF16)

    def shards_of(l):
        return [w_in[l].astype(BF16), w_up[l].astype(BF16), w_out[l].astype(BF16), w_down[l].astype(BF16)]

    def assemble(g_in, g_up, g_out, g_down):
        return (jnp.concatenate([g_in[b] for b in range(4)], axis=1),
                jnp.concatenate([g_up[b] for b in range(4)], axis=1),
                g_out.reshape(DMIX, D), g_down.reshape(F, D))

    w_full = [assemble(*gather_chip_shards(shards_of(0), name="gather_layer_weights"))]

    def vec(a, l):
        return a[l].reshape(1, -1)

    x0 = x.reshape(S, D)
    h1 = norm_fwd(x0, vec(norm_mix_pre, 0), name="norm_first")
    saved = []
    for l in range(L):
        Win, Wup, Wout, Wdown = w_full[l]
        token = None
        if l + 1 < L:
            srcs = shards_of(l + 1)
            lands = [lax.empty((4,) + s.shape, s.dtype) for s in srcs]
            s_sems, r_sems, srcs, lands, token = _split_start(
                srcs, lands, 4 * len(srcs), _gather_direct_copies(False), name="gather_start_layer%d" % (l + 1))
        u = matmul(h1, Win, out_dtype=BF16, tm=1024, tn=2176, tk=2048, j_outer=True, after=token, name="mm_in")
        ymix, cb = mixer_fwd(u, wa_full[l], wb_full[l], vec(conv_b_bias, l), vec(ln_b_gain, l), vec(ln_b_bias, l),
                             pw_bf[l], vec(pool_scale, l), name="mixer_fwd")
        y = matmul(ymix, Wout, out_dtype=F32, tm=512, tn=2048, tk=2048, name="mm_out")
        x1, h2 = resid_norm_fwd(x0, y, vec(norm_mix_post, l), vec(norm_ffn_pre, l), emit_h=True, name="resid_norm_mid")
        up = matmul(h2, Wup, out_dtype=BF16, tm=512, tn=2816, tk=2048, j_outer=True, name="mm_up")
        a, upc = ffn_fwd(up, wf_full[l], vec(conv_ffn_bias, l), name="ffn_fwd")
        f = matmul(a, Wdown, out_dtype=F32, tm=1024, tn=2048, tk=1408, name="mm_down")
        last = l == L - 1
        x2, h_next = resid_norm_fwd(x1, f, vec(norm_ffn_post, l), vec(norm_mix_pre, 0 if last else l + 1),
                                    emit_h=not last, name="resid_norm_last" if last else "resid_norm_end")
        saved.append((x0, h1, u, ymix, y, x1, h2, up, a, f, cb, upc))
        if l + 1 < L:
            w_full.append(assemble(*_split_wait(s_sems, r_sems, srcs, lands, x2, _gather_direct_copies(True),
                                                name="gather_wait_layer%d" % (l + 1))[1]))
        x0, h1 = x2, h_next

    dx, lsum = loss_head(x0, loss_target.reshape(S, D), name="loss_head")
    loss = lax.psum(lsum[0, 0] * (0.5 / D), ("x", "y", "c"))

    small = [None] * L
    big = [None] * L
    dt = dx
    _, df, _, dg4 = norm_bwd(dt, None, None, None, saved[L - 1][9], vec(norm_ffn_post, L - 1), name="norm_bwd_top")
    place = [jnp.reshape(v, (1,)).astype(jnp.int32) for v in
             (2 * xi + yi, 2 * (1 - xi) + yi, 2 * xi + (1 - yi), 2 * (1 - xi) + (1 - yi), ci)]

    def finish_exchange(pend, after):
        lp, s_sems, r_sems, ps, qs = pend
        ps, qs = _split_wait(s_sems, r_sems, ps, qs, after, _exchange_copies(True),
                             name="grad_exchange_wait_layer%d" % lp)
        fh = [sum_own_and_received(p, q, place, name="grad_sum_chips_%d" % k) for k, (p, q) in enumerate(zip(ps, qs))]
        big[lp] = sibling_join_halves(fh, name="grad_join_halves")

    pending, token = None, None
    for l in reversed(range(L)):
        Win, Wup, Wout, Wdown = w_full[l]
        x0, h1, u, ymix, y, x1, h2, up, a, f, cb, upc = saved[l]
        da = matmul(df, Wdown, tb=True, out_dtype=BF16, tm=1024, tn=1408, tk=2048, j_outer=True, after=token,
                    name="mm_down_dx")
        g_down = matmul(a, df, ta=True, out_dtype=BF16, tm=1408, tn=1024, tk=2048, name="mm_down_dw")
        dup, dwf, dbf = ffn_bwd(up, upc, da, wf_full[l], name="ffn_bwd")
        dh2 = matmul(dup, Wup, tb=True, out_dtype=F32, tm=512, tn=2048, tk=2816, name="mm_up_dx")
        g_up = matmul(h2, dup, ta=True, out_dtype=BF16, tm=1024, tn=1408, tk=2048, groups=4, name="mm_up_dw")
        dt, dy, dg3, dg2 = norm_bwd(dt, dh2, x1, vec(norm_ffn_pre, l), y, vec(norm_mix_post, l), name="norm_bwd_mid")
        dymix = matmul(dy, Wout, tb=True, out_dtype=BF16, tm=1024, tn=2048, tk=2048, name="mm_out_dx")
        g_out = matmul(ymix, dy, ta=True, out_dtype=BF16, tm=1024, tn=1024, tk=2048, name="mm_out_dw")
        du, dwa, dwb, dbb, dlg, dlb, dpw, dps = mixer_bwd(
            u, cb, dymix, wa_full[l], wb_full[l], vec(conv_b_bias, l), vec(ln_b_gain, l), vec(ln_b_bias, l),
            pw_bf[l], vec(pool_scale, l), name="mixer_bwd")
        dh1 = matmul(du, Win, tb=True, out_dtype=F32, tm=512, tn=2048, tk=2176, name="mm_in_dx")
        g_in = matmul(h1, du, ta=True, out_dtype=BF16, tm=1024, tn=2176, tk=1024, name="mm_in_dw")
        dg4_here = dg4
        if l > 0:
            dt, df, dg1, dg4 = norm_bwd(dt, dh1, x0, vec(norm_mix_pre, l), saved[l - 1][9], vec(norm_ffn_post, l - 1),
                                        name="norm_bwd_end")
        else:
            dt, _, dg1, _ = norm_bwd(dt, dh1, x0, vec(norm_mix_pre, 0), None, None, name="norm_bwd_bottom")
        small[l] = dict(norm_mix_pre=dg1, norm_mix_post=dg2, norm_ffn_pre=dg3, norm_ffn_post=dg4_here,
                        conv_a_w=dwa, conv_b_w=dwb, conv_b_bias=dbb, ln_b_gain=dlg, ln_b_bias=dlb,
                        pool_w=dpw, pool_scale=dps, conv_ffn_w=dwf, conv_ffn_bias=dbf)

        gs = [g_in.reshape(D, 4, DIN // 4).transpose(1, 0, 2), g_up,
              g_out.reshape(4, DMIX // 4, D), g_down.reshape(4, F // 4, D)]
        ts = sibling_swap_halves(gs, name="grad_swap_halves")
        ps = [pair_add(g, t, core, name="grad_pair_add_%d" % k) for k, (g, t) in enumerate(zip(gs, ts))]
        if pending is not None:
            finish_exchange(pending, dt)
        qs = [lax.empty(p.shape, p.dtype) for p in ps]
        s_sems, r_sems, ps, qs, token = _split_start(ps, qs, 3 * len(ps), _exchange_copies(False),
                                                     name="grad_exchange_start_layer%d" % l)
        pending = (l, s_sems, r_sems, ps, qs)
    grad_x = dt.reshape(1, S, D)

    rep_names = ["norm_mix_pre", "norm_mix_post", "norm_ffn_pre", "norm_ffn_post", "conv_b_bias", "ln_b_gain",
                 "ln_b_bias", "pool_w", "pool_scale", "conv_ffn_bias"]
    shd_names = ["conv_a_w", "conv_b_w", "conv_ffn_w"]
    given = dict(
        norm_mix_pre=(norm_mix_pre, m_norm_mix_pre, v_norm_mix_pre), norm_mix_post=(norm_mix_post, m_norm_mix_post, v_norm_mix_post),
        norm_ffn_pre=(norm_ffn_pre, m_norm_ffn_pre, v_norm_ffn_pre), norm_ffn_post=(norm_ffn_post, m_norm_ffn_post, v_norm_ffn_post),
        conv_b_bias=(conv_b_bias, m_conv_b_bias, v_conv_b_bias), ln_b_gain=(ln_b_gain, m_ln_b_gain, v_ln_b_gain),
        ln_b_bias=(ln_b_bias, m_ln_b_bias, v_ln_b_bias), pool_w=(pool_w, m_pool_w, v_pool_w),
        pool_scale=(pool_scale, m_pool_scale, v_pool_scale), conv_ffn_bias=(conv_ffn_bias, m_conv_ffn_bias, v_conv_ffn_bias),
        conv_a_w=(conv_a_w, m_conv_a_w, v_conv_a_w), conv_b_w=(conv_b_w, m_conv_b_w, v_conv_b_w),
        conv_ffn_w=(conv_ffn_w, m_conv_ffn_w, v_conv_ffn_w))
    full_shape = dict(conv_a_w=(L, K_A, WA), conv_b_w=(L, K_B, WB), conv_ffn_w=(L, K_F, F2))
    for nme in rep_names:
        full_shape[nme] = given[nme][0].shape
    names = rep_names + shd_names
    stacked = [jnp.stack([small[l][nme] for l in range(L)]).reshape(full_shape[nme]) for nme in names]
    parts = allgather8(_pack(stacked), name="gather_small_grads")
    totals = _unpack(sum_slots(parts, name="sum_small_grads"), [full_shape[nme] for nme in names])
    total = dict(zip(names, totals))
    for nme in shd_names:
        wd = full_shape[nme][2] // 4
        total[nme] = lax.dynamic_slice_in_dim(total[nme], chip * wd, wd, axis=2)
    shapes = [given[nme][0].shape for nme in names]
    d_s, m_s, v_s = adam_flat(_pack([given[nme][0] for nme in names]), _pack([total[nme] for nme in names]),
                              _pack([given[nme][1] for nme in names]), _pack([given[nme][2] for nme in names]),
                              name="adam_small")
    res = dict(zip(names, zip([total[nme] for nme in names], _unpack(d_s, shapes), _unpack(m_s, shapes),
                              _unpack(v_s, shapes))))

    large = [("w_in", (w_in, m_w_in, v_w_in)), ("w_up", (w_up, m_w_up, v_w_up)),
             ("w_out", (w_out, m_w_out, v_w_out)), ("w_down", (w_down, m_w_down, v_w_down))]
    assert L > 1
    upper = [adam_stacked(*trio, [big[l][k] for l in range(1, L)], first_layer=1, after=token,
                          name="adam_upper_" + nme)
             for k, (nme, trio) in enumerate(large)]
    finish_exchange(pending, upper[-1][1])
    for k, (nme, trio) in enumerate(large):
        res[nme] = adam_stacked(*trio, [big[0][k]], earlier=upper[k], name="adam_first_" + nme)

    order = ["norm_mix_pre", "norm_mix_post", "norm_ffn_pre", "norm_ffn_post", "w_in", "conv_a_w", "conv_b_w",
             "conv_b_bias", "ln_b_gain", "ln_b_bias", "pool_w", "pool_scale", "w_out", "w_up", "conv_ffn_w",
             "conv_ffn_bias", "w_down"]
    outs = [loss, grad_x]
    for k in range(4):
        outs += [res[nme][k] for nme in order]
    return tuple(outs)
```

```python
import functools

import jax
import jax.numpy as jnp
from jax import lax
from jax.experimental import pallas as pl
from jax.experimental.pallas import tpu as pltpu

F32 = jnp.float32
BF16 = jnp.bfloat16
MESH = pl.DeviceIdType.MESH

RMS_EPS = 1e-6
LN_EPS = 1e-5
ADAM_LR = 0.001
ADAM_B1 = 0.9
ADAM_B2 = 0.999
ADAM_EPS = 1e-08
ADAM_WD = 0.01
ADAM_STEP = 10
POOL_WINDOWS = (2, 4, 8, 16)
K_A = 3
K_B = 31
K_F = 3

LANES = 128
HALO_MIX = 32
HALO_FFN = 16
ROWS = 64
TM_MIXER = 512
TM_FFN = 256
TM_FFN_BWD = 128
TM_NORM = 256
ADAM_BLOCK_BYTES = 1 << 20
VMEM_LIMIT = 56 * 1024 * 1024


def _cp(n_axes):
    return pltpu.CompilerParams(dimension_semantics=("arbitrary",) * n_axes, vmem_limit_bytes=VMEM_LIMIT)


def _tile(dim, target, mult=LANES):
    if dim <= target:
        return dim
    t = (target // mult) * mult
    while t >= mult:
        if dim % t == 0:
            return t
        t -= mult
    return dim


def _chunks(n, rc=ROWS):
    out, r0 = [], 0
    while r0 < n:
        s = min(rc, n - r0)
        out.append((r0, s))
        r0 += s
    return out


def _sigmoid(x):
    return 1.0 / (1.0 + jnp.exp(-x))


def matmul(a, b, *, ta=False, tb=False, out_dtype, tm, tn, tk, j_outer=False, groups=1, after=None, name):
    if ta:
        K, M = a.shape
    else:
        M, K = a.shape
    bg = b.shape[0] if b.ndim == 3 else 1
    if tb:
        N, K2 = b.shape[-2], bg * b.shape[-1]
    else:
        K2, N = b.shape[-2], bg * b.shape[-1]
    assert K == K2, (a.shape, b.shape)
    ng = N // groups
    tm, tn, tk = _tile(M, tm), _tile(ng, tn), _tile(K, tk)
    if bg > 1 and tb:
        tk = _tile(b.shape[-1], tk)
    elif bg > 1:
        tn = _tile(b.shape[-1], tn)
    nm, nn, nk = M // tm, N // tn, K // tk
    per = ng // tn

    def ij(g0, g1):
        return (g1, g0) if j_outer else (g0, g1)

    def a_map(g0, g1, k):
        i, j = ij(g0, g1)
        return (k, i) if ta else (i, k)

    bcols = b.shape[-1]

    def b_map(g0, g1, k):
        i, j = ij(g0, g1)
        if bg == 1:
            return (j, k) if tb else (k, j)
        if tb:
            return (k // (bcols // tk), j, k % (bcols // tk))
        return (j // (bcols // tn), k, j % (bcols // tn))

    def o_map(g0, g1, k):
        i, j = ij(g0, g1)
        return (j // per, i, j % per) if groups > 1 else (i, j)

    dims = (((0 if ta else 1,), (1 if tb else 0,)), ((), ()))
    use_acc = nk > 1 and out_dtype != F32

    def body(a_ref, b_ref, o_ref, *scratch):
        p = lax.dot_general(a_ref[...], b_ref[...], dims, preferred_element_type=F32)
        if nk == 1:
            o_ref[...] = p.astype(o_ref.dtype)
            return
        acc = scratch[0] if use_acc else o_ref
        k = pl.program_id(2)

        @pl.when(k == 0)
        def _():
            acc[...] = p

        @pl.when(k > 0)
        def _():
            acc[...] += p

        if use_acc:
            @pl.when(k == nk - 1)
            def _():
                o_ref[...] = acc[...].astype(o_ref.dtype)

    grid = (nn, nm, nk) if j_outer else (nm, nn, nk)
    if groups > 1:
        out_shape = jax.ShapeDtypeStruct((groups, M, ng), out_dtype)
        out_spec = pl.BlockSpec((None, tm, tn), o_map)
    else:
        out_shape = jax.ShapeDtypeStruct((M, N), out_dtype)
        out_spec = pl.BlockSpec((tm, tn), o_map)
    has_after = after is not None

    def body_after(a_ref, b_ref, after_ref, o_ref, *scratch):
        body(a_ref, b_ref, o_ref, *scratch)

    return pl.pallas_call(
        body_after if has_after else body, grid=grid,
        in_specs=[pl.BlockSpec((tk, tm) if ta else (tm, tk), a_map),
                  pl.BlockSpec(((None,) if bg > 1 else ()) + ((tn, tk) if tb else (tk, tn)), b_map)]
                 + ([pl.BlockSpec(memory_space=pl.ANY)] if has_after else []),
        out_specs=out_spec, out_shape=out_shape,
        scratch_shapes=[pltpu.VMEM((tm, tn), F32)] if use_acc else [],
        compiler_params=_cp(3), name=name,
    )(*((a, b, after) if has_after else (a, b)))


def _rms(v):
    return lax.rsqrt(jnp.mean(v * v, axis=-1, keepdims=True) + RMS_EPS)


def norm_fwd(x, g, *, name):
    S, D = x.shape
    tm = _tile(S, TM_NORM, 16)

    def body(x_ref, g_ref, h_ref):
        v = x_ref[...]
        h_ref[...] = (v * _rms(v) * g_ref[...]).astype(BF16)

    return pl.pallas_call(
        body, grid=(S // tm,),
        in_specs=[pl.BlockSpec((tm, D), lambda i: (i, 0)), pl.BlockSpec((1, D), lambda i: (0, 0))],
        out_specs=pl.BlockSpec((tm, D), lambda i: (i, 0)),
        out_shape=jax.ShapeDtypeStruct((S, D), BF16), compiler_params=_cp(1), name=name,
    )(x, g)


def resid_norm_fwd(x, y, gp, gn, *, emit_h, name):
    S, D = x.shape
    tm = _tile(S, TM_NORM, 16)

    def body(x_ref, y_ref, gp_ref, gn_ref, xn_ref, *rest):
        yv = y_ref[...]
        xn = x_ref[...] + yv * _rms(yv) * gp_ref[...]
        xn_ref[...] = xn
        if emit_h:
            rest[0][...] = (xn * _rms(xn) * gn_ref[...]).astype(BF16)

    row = pl.BlockSpec((tm, D), lambda i: (i, 0))
    vec = pl.BlockSpec((1, D), lambda i: (0, 0))
    outs = pl.pallas_call(
        body, grid=(S // tm,), in_specs=[row, row, vec, vec],
        out_specs=[row, row] if emit_h else [row],
        out_shape=[jax.ShapeDtypeStruct((S, D), F32)] + ([jax.ShapeDtypeStruct((S, D), BF16)] if emit_h else []),
        compiler_params=_cp(1), name=name,
    )(x, y, gp, gn)
    return (outs[0], outs[1]) if emit_h else (outs[0], None)


def _rms_bwd(v, g, dout):
    r = _rms(v)
    gd = g * dout
    dv = r * gd - v * (r * r * r) * jnp.mean(v * gd, axis=-1, keepdims=True)
    return dv, dout * v * r


def norm_bwd(d_direct, dh, xn, gn, y, gp, *, name):
    S, D = d_direct.shape
    tm = _tile(S, TM_NORM, 16)
    has_h, has_y = dh is not None, y is not None

    def body(*refs):
        refs = list(refs)
        dd_ref = refs.pop(0)
        if has_h:
            dh_ref, xn_ref, gn_ref = refs.pop(0), refs.pop(0), refs.pop(0)
        if has_y:
            y_ref, gp_ref = refs.pop(0), refs.pop(0)
        if has_h:
            dt_ref = refs.pop(0)
        if has_y:
            dy_ref = refs.pop(0)
        if has_h:
            dgn_ref = refs.pop(0)
        if has_y:
            dgp_ref = refs.pop(0)
        i = pl.program_id(0)
        dt = dd_ref[...]
        if has_h:
            dv, gterm = _rms_bwd(xn_ref[...], gn_ref[...], dh_ref[...])
            dt = dt + dv
            dt_ref[...] = dt
            part = jnp.sum(gterm, axis=0, keepdims=True)

            @pl.when(i == 0)
            def _():
                dgn_ref[...] = part

            @pl.when(i > 0)
            def _():
                dgn_ref[...] += part
        if has_y:
            dy, gterm = _rms_bwd(y_ref[...], gp_ref[...], dt)
            dy_ref[...] = dy.astype(BF16)
            part2 = jnp.sum(gterm, axis=0, keepdims=True)

            @pl.when(i == 0)
            def _():
                dgp_ref[...] = part2

            @pl.when(i > 0)
            def _():
                dgp_ref[...] += part2

    row = pl.BlockSpec((tm, D), lambda i: (i, 0))
    vec = pl.BlockSpec((1, D), lambda i: (0, 0))
    ins, in_specs = [d_direct], [row]
    if has_h:
        ins += [dh, xn, gn]
        in_specs += [row, row, vec]
    if has_y:
        ins += [y, gp]
        in_specs += [row, vec]
    out_specs, out_shape = [], []
    if has_h:
        out_specs.append(row)
        out_shape.append(jax.ShapeDtypeStruct((S, D), F32))
    if has_y:
        out_specs.append(row)
        out_shape.append(jax.ShapeDtypeStruct((S, D), BF16))
    if has_h:
        out_specs.append(vec)
        out_shape.append(jax.ShapeDtypeStruct((1, D), F32))
    if has_y:
        out_specs.append(vec)
        out_shape.append(jax.ShapeDtypeStruct((1, D), F32))
    outs = list(pl.pallas_call(
        body, grid=(S // tm,), in_specs=in_specs, out_specs=out_specs, out_shape=out_shape,
        compiler_params=_cp(1), name=name,
    )(*ins))
    dt = outs.pop(0) if has_h else d_direct
    dy = outs.pop(0) if has_y else None
    dgn = outs.pop(0) if has_h else None
    dgp = outs.pop(0) if has_y else None
    return dt, dy, dgn, dgp


def loss_head(xl, target, *, name):
    S, D = xl.shape
    tm = _tile(S, TM_NORM, 16)

    def body(x_ref, t_ref, dx_ref, l_ref):
        i = pl.program_id(0)
        e = x_ref[...] - t_ref[...]
        dx_ref[...] = e * (1.0 / D)
        part = jnp.sum(e * e)

        @pl.when(i == 0)
        def _():
            l_ref[...] = jnp.zeros_like(l_ref) + part

        @pl.when(i > 0)
        def _():
            l_ref[...] += part

    row = pl.BlockSpec((tm, D), lambda i: (i, 0))
    return pl.pallas_call(
        body, grid=(S // tm,), in_specs=[row, row],
        out_specs=[row, pl.BlockSpec((8, LANES), lambda i: (0, 0))],
        out_shape=[jax.ShapeDtypeStruct((S, D), F32), jax.ShapeDtypeStruct((8, LANES), F32)],
        compiler_params=_cp(1), name=name,
    )(xl, target)


def _tap_sum(src, base, offs, wrows, r0, rc, c0):
    acc = None
    for k, off in enumerate(offs):
        t = src[base + r0 + off: base + r0 + off + rc, c0:c0 + LANES]
        if wrows is not None:
            t = t * wrows[k]
        acc = t if acc is None else acc + t
    return acc


def _tap_wgrad(out_ref, o0, a, a_base, b, b_base, offs, n, c0, first):
    for k, off in enumerate(offs):
        acc = None
        for r0, rc in _chunks(n):
            t = (a[a_base + r0: a_base + r0 + rc, c0:c0 + LANES]
                 * b[b_base + r0 + off: b_base + r0 + off + rc, c0:c0 + LANES])
            t = jnp.sum(t, axis=0, keepdims=True)
            acc = t if acc is None else acc + t
        _acc_store(out_ref, (slice(k, k + 1), slice(o0 + c0, o0 + c0 + LANES)), acc, first)


def _shift_copies(sh, src, n, shifts, c0):
    for b in shifts:
        for r0, rc in _chunks(n):
            sh[b, r0:r0 + rc, :] = src[r0 + b:r0 + b + rc, c0:c0 + LANES]


def _tap(sh, src, o, r0, rc, c0):
    a, b = divmod(o, 8)
    if b == 0:
        return src[r0 + o:r0 + o + rc, c0:c0 + LANES]
    return sh[b, r0 + 8 * a:r0 + 8 * a + rc, :]


def _acc_store(ref, idx, val, first):
    del first
    ref[idx] += val


def _zero_first(first, *refs):
    @pl.when(first)
    def _():
        for ref in refs:
            ref[...] = jnp.zeros_like(ref)


def _row_counts(t0, rc, w):
    t = t0 + lax.broadcasted_iota(jnp.int32, (rc, LANES), 0)
    return jnp.minimum(t + 1, w).astype(F32)


def mixer_fwd(u, wa, wb, bb, lg, lb, pw, ps, *, name):
    S, DIN = u.shape
    WA, WB, WC = wa.shape[1], wb.shape[1], ps.shape[1]
    DMIX = WA + WB + WC
    tm = _tile(S, TM_MIXER, HALO_MIX)
    HB = HALO_MIX
    r = tm // HB
    oCg, oVa, oVal, oGate, oC = WA, 2 * WA, 3 * WA, 3 * WA + WB, 3 * WA + 2 * WB
    offs_a = [k - (K_A - 1) for k in range(K_A)]
    offs_b = [k - (K_B - 1) for k in range(K_B)]

    def body(u_ref, uh_ref, wa_ref, wb_ref, bb_ref, lg_ref, lb_ref, pw_ref, ps_ref, y_ref, cbuf,
             pbuf, gbuf, shbuf, xbuf, plbuf):
        i = pl.program_id(0)
        hv = jnp.where(i > 0, 1.0, 0.0).astype(F32)

        def fill(src, dst0, n, scale):
            for r0, rc in _chunks(n):
                rows, drows = slice(r0, r0 + rc), slice(dst0 + r0, dst0 + r0 + rc)
                for c0 in range(0, WA, LANES):
                    v = (src[rows, oCg + c0:oCg + c0 + LANES].astype(F32)
                         * src[rows, oVa + c0:oVa + c0 + LANES].astype(F32))
                    pbuf[drows, c0:c0 + LANES] = v if scale is None else v * scale
                for c0 in range(0, WB, LANES):
                    v = (src[rows, oVal + c0:oVal + c0 + LANES].astype(F32)
                         * _sigmoid(src[rows, oGate + c0:oGate + c0 + LANES].astype(F32)))
                    gbuf[drows, c0:c0 + LANES] = v if scale is None else v * scale
                for c0 in range(0, WC, LANES):
                    v = src[rows, oC + c0:oC + c0 + LANES].astype(F32)
                    xbuf[drows, c0:c0 + LANES] = v if scale is None else v * scale

        fill(uh_ref, 0, HB, hv)
        fill(u_ref, HB, tm, None)

        for c0 in range(0, WA, LANES):
            w = [wa_ref[k:k + 1, c0:c0 + LANES] for k in range(K_A)]
            for r0, rc in _chunks(tm):
                q = _tap_sum(pbuf, HB, offs_a, w, r0, rc, c0)
                bg = u_ref[r0:r0 + rc, c0:c0 + LANES].astype(F32)
                y_ref[r0:r0 + rc, c0:c0 + LANES] = (bg * q).astype(BF16)

        for c0 in range(0, WB, LANES):
            w = [wb_ref[k:k + 1, c0:c0 + LANES] for k in range(K_B)]
            bias = bb_ref[:, c0:c0 + LANES]
            _shift_copies(shbuf, gbuf, tm + 24, range(1, 8), c0)
            for r0, rc in _chunks(tm):
                acc = bias
                for k in range(K_B):
                    acc = acc + _tap(shbuf, gbuf, HB - (K_B - 1) + k, r0, rc, c0) * w[k]
                cbuf[r0:r0 + rc, c0:c0 + LANES] = acc
        for r0, rc in _chunks(tm, 32):
            cb = cbuf[r0:r0 + rc, :]
            mu = jnp.mean(cb, axis=-1, keepdims=True)
            d = cb - mu
            n = d * lax.rsqrt(jnp.mean(d * d, axis=-1, keepdims=True) + LN_EPS)
            z = n * lg_ref[...] + lb_ref[...]
            y_ref[r0:r0 + rc, WA:WA + WB] = (z * _sigmoid(z)).astype(BF16)

        for g, win in enumerate(POOL_WINDOWS):
            c0 = g * LANES
            for r0, rc in _chunks(tm):
                s = _tap_sum(xbuf, HB, [-j for j in range(win)], None, r0, rc, c0)
                pooled = s / _row_counts(i * tm + r0, rc, win) - xbuf[HB + r0:HB + r0 + rc, c0:c0 + LANES]
                plbuf[r0:r0 + rc, c0:c0 + LANES] = pooled.astype(BF16)
            mixed = jnp.dot(plbuf[:, c0:c0 + LANES], pw_ref[g], preferred_element_type=F32)
            y_ref[:, WA + WB + c0:WA + WB + c0 + LANES] = (mixed * ps_ref[:, c0:c0 + LANES]).astype(BF16)

    full = lambda a: pl.BlockSpec(a.shape, lambda i: (0,) * a.ndim)
    return pl.pallas_call(
        body, grid=(S // tm,),
        in_specs=[pl.BlockSpec((tm, DIN), lambda i: (i, 0)),
                  pl.BlockSpec((HB, DIN), lambda i: (jnp.maximum(i * r - 1, 0), 0)),
                  full(wa), full(wb), full(bb), full(lg), full(lb), full(pw), full(ps)],
        out_specs=[pl.BlockSpec((tm, DMIX), lambda i: (i, 0)), pl.BlockSpec((tm, WB), lambda i: (i, 0))],
        out_shape=[jax.ShapeDtypeStruct((S, DMIX), BF16), jax.ShapeDtypeStruct((S, WB), F32)],
        scratch_shapes=[pltpu.VMEM((HB + tm, WA), F32), pltpu.VMEM((HB + tm, WB), F32),
                        pltpu.VMEM((8, tm + 24, LANES), F32), pltpu.VMEM((HB + tm, WC), F32),
                        pltpu.VMEM((tm, WC), BF16)],
        compiler_params=_cp(1), name=name,
    )(u, u, wa, wb, bb, lg, lb, pw, ps)


def mixer_bwd(u, cb, dy, wa, wb, bb, lg, lb, pw, ps, *, name):
    S, DIN = u.shape
    WA, WB, WC = wa.shape[1], wb.shape[1], ps.shape[1]
    NG = WC // LANES
    DMIX = WA + WB + WC
    tm = _tile(S, TM_MIXER, HALO_MIX)
    HB = HALO_MIX
    r = tm // HB
    nI = S // tm
    nH = S // HB
    oCg, oVa, oVal, oGate, oC = WA, 2 * WA, 3 * WA, 3 * WA + WB, 3 * WA + 2 * WB
    offs_a = [k - (K_A - 1) for k in range(K_A)]
    offs_b = [k - (K_B - 1) for k in range(K_B)]
    adj_a = [(K_A - 1) - k for k in range(K_A)]
    adj_b = [(K_B - 1) - k for k in range(K_B)]

    def body(u_ref, ub_ref, ua_ref, cb_ref, cba_ref, dy_ref, dya_ref,
             wa_ref, wb_ref, bb_ref, lg_ref, lb_ref, pw_ref, ps_ref,
             du_ref, dwa_ref, dwb_ref, dbb_ref, dlg_ref, dlb_ref, dpw_ref, dps_ref,
             pbuf, dqbuf, gbuf, dcbuf, shbuf, xbuf, plbuf, dmbuf, dplbuf, dpcbuf):
        i = pl.program_id(0)
        first = i == 0
        hvb = jnp.where(i > 0, 1.0, 0.0).astype(F32)
        hva = jnp.where(i < nI - 1, 1.0, 0.0).astype(F32)
        _zero_first(first, dwa_ref, dwb_ref, dbb_ref, dlg_ref, dlb_ref, dpw_ref, dps_ref)

        def fill(src, dst0, n, scale, main):
            for r0, rc in _chunks(n):
                rows, drows = slice(r0, r0 + rc), slice(dst0 + r0, dst0 + r0 + rc)
                for c0 in range(0, WA, LANES):
                    v = (src[rows, oCg + c0:oCg + c0 + LANES].astype(F32)
                         * src[rows, oVa + c0:oVa + c0 + LANES].astype(F32))
                    pbuf[drows, c0:c0 + LANES] = v if scale is None else v * scale
                for c0 in range(0, WC, LANES):
                    v = src[rows, oC + c0:oC + c0 + LANES].astype(F32)
                    xbuf[drows, c0:c0 + LANES] = v if scale is None else v * scale
                if main:
                    for c0 in range(0, WB, LANES):
                        gbuf[rows, c0:c0 + LANES] = (
                            src[rows, oVal + c0:oVal + c0 + LANES].astype(F32)
                            * _sigmoid(src[rows, oGate + c0:oGate + c0 + LANES].astype(F32)))

        fill(ub_ref, 0, HB, hvb, False)
        fill(u_ref, HB, tm, None, True)

        for r0, rc in _chunks(tm):
            for c0 in range(0, WA, LANES):
                dqbuf[r0:r0 + rc, c0:c0 + LANES] = (dy_ref[r0:r0 + rc, c0:c0 + LANES].astype(F32)
                                                     * u_ref[r0:r0 + rc, c0:c0 + LANES].astype(F32))
        for c0 in range(0, WA, LANES):
            dqbuf[tm:tm + HB, c0:c0 + LANES] = (dya_ref[:, c0:c0 + LANES].astype(F32)
                                                * ua_ref[:, c0:c0 + LANES].astype(F32)) * hva
        for c0 in range(0, WA, LANES):
            w = [wa_ref[k:k + 1, c0:c0 + LANES] for k in range(K_A)]
            for r0, rc in _chunks(tm):
                rows = slice(r0, r0 + rc)
                q = _tap_sum(pbuf, HB, offs_a, w, r0, rc, c0)
                du_ref[rows, c0:c0 + LANES] = (dy_ref[rows, c0:c0 + LANES].astype(F32) * q).astype(BF16)
                dp = _tap_sum(dqbuf, 0, adj_a, w, r0, rc, c0)
                cg = u_ref[rows, oCg + c0:oCg + c0 + LANES].astype(F32)
                va = u_ref[rows, oVa + c0:oVa + c0 + LANES].astype(F32)
                du_ref[rows, oCg + c0:oCg + c0 + LANES] = (dp * va).astype(BF16)
                du_ref[rows, oVa + c0:oVa + c0 + LANES] = (dp * cg).astype(BF16)
            _tap_wgrad(dwa_ref, 0, dqbuf, 0, pbuf, HB, offs_a, tm, c0, first)

        def ln_chunk(r0, rc, cb, dyb, scale, main):
            mu = jnp.mean(cb, axis=-1, keepdims=True)
            d = cb - mu
            rs = lax.rsqrt(jnp.mean(d * d, axis=-1, keepdims=True) + LN_EPS)
            n = d * rs
            z = n * lg_ref[...] + lb_ref[...]
            sg = _sigmoid(z)
            dz = dyb * (sg * (1.0 + z * (1.0 - sg)))
            dn = dz * lg_ref[...]
            dcb = rs * (dn - jnp.mean(dn, axis=-1, keepdims=True) - n * jnp.mean(dn * n, axis=-1, keepdims=True))
            if scale is not None:
                dcb = dcb * scale
            dcbuf[r0:r0 + rc, :] = dcb
            if main:
                return (jnp.sum(dz * n, axis=0, keepdims=True), jnp.sum(dz, axis=0, keepdims=True),
                        jnp.sum(dcb, axis=0, keepdims=True))
            return None

        sums = None
        for r0, rc in _chunks(tm, 32):
            part = ln_chunk(r0, rc, cb_ref[r0:r0 + rc, :], dy_ref[r0:r0 + rc, WA:WA + WB].astype(F32), None, True)
            sums = part if sums is None else tuple(a + b for a, b in zip(sums, part))
        ln_chunk(tm, HB, cba_ref[...], dya_ref[:, WA:WA + WB].astype(F32), hva, False)
        _acc_store(dlg_ref, (slice(None), slice(None)), sums[0], first)
        _acc_store(dlb_ref, (slice(None), slice(None)), sums[1], first)
        _acc_store(dbb_ref, (slice(None), slice(None)), sums[2], first)

        for c0 in range(0, WB, LANES):
            w = [wb_ref[k:k + 1, c0:c0 + LANES] for k in range(K_B)]
            _shift_copies(shbuf, dcbuf, tm + 24, range(1, 8), c0)
            for r0, rc in _chunks(tm):
                rows = slice(r0, r0 + rc)
                dglu = None
                for k in range(K_B):
                    t = _tap(shbuf, dcbuf, adj_b[k], r0, rc, c0) * w[k]
                    dglu = t if dglu is None else dglu + t
                val = u_ref[rows, oVal + c0:oVal + c0 + LANES].astype(F32)
                sg = _sigmoid(u_ref[rows, oGate + c0:oGate + c0 + LANES].astype(F32))
                du_ref[rows, oVal + c0:oVal + c0 + LANES] = (dglu * sg).astype(BF16)
                du_ref[rows, oGate + c0:oGate + c0 + LANES] = (dglu * val * sg * (1.0 - sg)).astype(BF16)
            for k in range(K_B):
                acc = None
                for r0, rc in _chunks(tm):
                    t = _tap(shbuf, dcbuf, adj_b[k], r0, rc, c0) * gbuf[r0:r0 + rc, c0:c0 + LANES]
                    acc = t if acc is None else acc + t
                _acc_store(dwb_ref, (slice(k, k + 1), slice(c0, c0 + LANES)),
                           jnp.sum(acc, axis=0, keepdims=True), first)

        for g, win in enumerate(POOL_WINDOWS):
            c0 = g * LANES
            cols = slice(c0, c0 + LANES)
            ycols = slice(WA + WB + c0, WA + WB + c0 + LANES)
            for r0, rc in _chunks(tm):
                s = _tap_sum(xbuf, HB, [-j for j in range(win)], None, r0, rc, c0)
                pooled = s / _row_counts(i * tm + r0, rc, win) - xbuf[HB + r0:HB + r0 + rc, cols]
                plbuf[r0:r0 + rc, cols] = pooled.astype(BF16)
            mixed = jnp.dot(plbuf[:, cols], pw_ref[g], preferred_element_type=F32)
            dyc = dy_ref[:, ycols].astype(F32)
            _acc_store(dps_ref, (slice(None), cols), jnp.sum(dyc * mixed, axis=0, keepdims=True), first)
            dmbuf[0:tm, :] = (dyc * ps_ref[:, cols]).astype(BF16)
            dmbuf[tm:tm + HB, :] = (dya_ref[:, ycols].astype(F32) * ps_ref[:, cols] * hva).astype(BF16)
            dpw = lax.dot_general(plbuf[:, cols], dmbuf[0:tm, :], (((0,), (0,)), ((), ())),
                                  preferred_element_type=F32)
            _acc_store(dpw_ref, (g, slice(None), slice(None)), dpw, first)
            dplbuf[...] = lax.dot_general(dmbuf[...], pw_ref[g], (((1,), (1,)), ((), ())),
                                          preferred_element_type=F32)
            for r0, rc in _chunks(tm + HB):
                dpcbuf[r0:r0 + rc, :] = dplbuf[r0:r0 + rc, :] / _row_counts(i * tm + r0, rc, win)
            for r0, rc in _chunks(tm):
                duc = _tap_sum(dpcbuf, 0, list(range(win)), None, r0, rc, 0) - dplbuf[r0:r0 + rc, :]
                du_ref[r0:r0 + rc, oC + c0:oC + c0 + LANES] = duc.astype(BF16)

    full = lambda a: pl.BlockSpec(a.shape, lambda i: (0,) * a.ndim)
    acc = lambda shape: pl.BlockSpec(shape, lambda i: (0,) * len(shape))
    small = [(K_A, WA), (K_B, WB), (1, WB), (1, WB), (1, WB), (NG, LANES, LANES), (1, WC)]
    outs = pl.pallas_call(
        body, grid=(nI,),
        in_specs=[pl.BlockSpec((tm, DIN), lambda i: (i, 0)),
                  pl.BlockSpec((HB, DIN), lambda i: (jnp.maximum(i * r - 1, 0), 0)),
                  pl.BlockSpec((HB, DIN), lambda i: (jnp.minimum((i + 1) * r, nH - 1), 0)),
                  pl.BlockSpec((tm, WB), lambda i: (i, 0)),
                  pl.BlockSpec((HB, WB), lambda i: (jnp.minimum((i + 1) * r, nH - 1), 0)),
                  pl.BlockSpec((tm, DMIX), lambda i: (i, 0)),
                  pl.BlockSpec((HB, DMIX), lambda i: (jnp.minimum((i + 1) * r, nH - 1), 0)),
                  full(wa), full(wb), full(bb), full(lg), full(lb), full(pw), full(ps)],
        out_specs=[pl.BlockSpec((tm, DIN), lambda i: (i, 0))] + [acc(s) for s in small],
        out_shape=[jax.ShapeDtypeStruct((S, DIN), BF16)] + [jax.ShapeDtypeStruct(s, F32) for s in small],
        scratch_shapes=[pltpu.VMEM((HB + tm, WA), F32), pltpu.VMEM((tm + HB, WA), F32),
                        pltpu.VMEM((tm, WB), F32), pltpu.VMEM((tm + HB, WB), F32),
                        pltpu.VMEM((8, tm + 24, LANES), F32), pltpu.VMEM((HB + tm, WC), F32),
                        pltpu.VMEM((tm, WC), BF16), pltpu.VMEM((tm + HB, LANES), BF16),
                        pltpu.VMEM((tm + HB, LANES), F32), pltpu.VMEM((tm + HB, LANES), F32)],
        compiler_params=_cp(1), name=name,
    )(u, u, u, cb, cb, dy, dy, wa, wb, bb, lg, lb, pw, ps)
    return outs


def ffn_fwd(up, wf, bf, *, name):
    S, F2 = up.shape
    F = F2 // 2
    tm = _tile(S, TM_FFN, HALO_FFN)
    HB = HALO_FFN
    r = tm // HB
    CW = _tile(F, 512)
    offs = [k - (K_F - 1) for k in range(K_F)]

    def body(up_ref, uph_ref, wf_ref, bf_ref, a_ref, upc_ref, ebuf):
        i = pl.program_id(0)
        hv = jnp.where(i > 0, 1.0, 0.0).astype(F32)
        for c0 in range(0, F, CW):
            for h, off in ((0, c0), (1, F + c0)):
                ebuf[h, 0:HB, :] = uph_ref[:, off:off + CW].astype(F32) * hv
                for r0, rc in _chunks(tm):
                    ebuf[h, HB + r0:HB + r0 + rc, :] = up_ref[r0:r0 + rc, off:off + CW].astype(F32)
            for l0 in range(0, CW, LANES):
                cg, cv = c0 + l0, F + c0 + l0
                wg = [wf_ref[k:k + 1, cg:cg + LANES] for k in range(K_F)]
                wv = [wf_ref[k:k + 1, cv:cv + LANES] for k in range(K_F)]
                bg = bf_ref[:, cg:cg + LANES]
                bv = bf_ref[:, cv:cv + LANES]
                for r0, rc in _chunks(tm):
                    gt = _tap_sum(ebuf.at[0], HB, offs, wg, r0, rc, l0) + bg
                    vl = _tap_sum(ebuf.at[1], HB, offs, wv, r0, rc, l0) + bv
                    a_ref[r0:r0 + rc, cg:cg + LANES] = (gt * _sigmoid(gt) * vl).astype(BF16)
                    upc_ref[r0:r0 + rc, cg:cg + LANES] = gt.astype(BF16)
                    upc_ref[r0:r0 + rc, cv:cv + LANES] = vl.astype(BF16)

    full = lambda a: pl.BlockSpec(a.shape, lambda i: (0,) * a.ndim)
    return pl.pallas_call(
        body, grid=(S // tm,),
        in_specs=[pl.BlockSpec((tm, F2), lambda i: (i, 0)),
                  pl.BlockSpec((HB, F2), lambda i: (jnp.maximum(i * r - 1, 0), 0)),
                  full(wf), full(bf)],
        out_specs=[pl.BlockSpec((tm, F), lambda i: (i, 0)), pl.BlockSpec((tm, F2), lambda i: (i, 0))],
        out_shape=[jax.ShapeDtypeStruct((S, F), BF16), jax.ShapeDtypeStruct((S, F2), BF16)],
        scratch_shapes=[pltpu.VMEM((2, HB + tm, CW), F32)],
        compiler_params=_cp(1), name=name,
    )(up, up, wf, bf)


def ffn_bwd(up, upc, da, wf, *, name):
    S, F2 = up.shape
    F = F2 // 2
    tm = _tile(S, TM_FFN_BWD, HALO_FFN)
    HB = HALO_FFN
    r = tm // HB
    nI = S // tm
    nH = S // HB
    CW = _tile(F, 512)
    adj = [(K_F - 1) - k for k in range(K_F)]

    def body(up_ref, upc_ref, upca_ref, da_ref, daa_ref, wf_ref, dup_ref, dwf_ref, dbf_ref, dbuf, shbuf):
        i = pl.program_id(0)
        first = i == 0
        hva = jnp.where(i < nI - 1, 1.0, 0.0).astype(F32)
        _zero_first(first, dwf_ref, dbf_ref)
        for c0 in range(0, F, CW):
            for l0 in range(0, CW, LANES):
                cg, cv = c0 + l0, F + c0 + l0
                wg = [wf_ref[k:k + 1, cg:cg + LANES] for k in range(K_F)]
                wv = [wf_ref[k:k + 1, cv:cv + LANES] for k in range(K_F)]
                sg_sum, sv_sum = None, None
                for r0, rc in _chunks(tm + HB):
                    if r0 < tm:
                        gt = upc_ref[r0:r0 + rc, cg:cg + LANES].astype(F32)
                        vl = upc_ref[r0:r0 + rc, cv:cv + LANES].astype(F32)
                        d = da_ref[r0:r0 + rc, cg:cg + LANES].astype(F32)
                    else:
                        gt = upca_ref[:, cg:cg + LANES].astype(F32)
                        vl = upca_ref[:, cv:cv + LANES].astype(F32)
                        d = daa_ref[:, cg:cg + LANES].astype(F32) * hva
                    s = _sigmoid(gt)
                    dg = d * vl * (s * (1.0 + gt * (1.0 - s)))
                    dv = d * (gt * s)
                    dbuf[0, r0:r0 + rc, :] = dg
                    dbuf[1, r0:r0 + rc, :] = dv
                    if r0 < tm:
                        pg, pv = jnp.sum(dg, axis=0, keepdims=True), jnp.sum(dv, axis=0, keepdims=True)
                        sg_sum = pg if sg_sum is None else sg_sum + pg
                        sv_sum = pv if sv_sum is None else sv_sum + pv
                _acc_store(dbf_ref, (slice(None), slice(cg, cg + LANES)), sg_sum, first)
                _acc_store(dbf_ref, (slice(None), slice(cv, cv + LANES)), sv_sum, first)
                for h, w, col in ((0, wg, cg), (1, wv, cv)):
                    d_h, sh_h = dbuf.at[h], shbuf.at[h]
                    _shift_copies(sh_h, d_h, tm, (1, 2), 0)
                    accs = [None] * K_F
                    for r0, rc in _chunks(tm):
                        taps = [_tap(sh_h, d_h, adj[k], r0, rc, 0) for k in range(K_F)]
                        dup_ref[r0:r0 + rc, col:col + LANES] = (
                            taps[0] * w[0] + taps[1] * w[1] + taps[2] * w[2]).astype(BF16)
                        uv = up_ref[r0:r0 + rc, col:col + LANES].astype(F32)
                        for k in range(K_F):
                            t = taps[k] * uv
                            accs[k] = t if accs[k] is None else accs[k] + t
                    for k in range(K_F):
                        _acc_store(dwf_ref, (slice(k, k + 1), slice(col, col + LANES)),
                                   jnp.sum(accs[k], axis=0, keepdims=True), first)

    full = lambda a: pl.BlockSpec(a.shape, lambda i: (0,) * a.ndim)
    return pl.pallas_call(
        body, grid=(nI,),
        in_specs=[pl.BlockSpec((tm, F2), lambda i: (i, 0)),
                  pl.BlockSpec((tm, F2), lambda i: (i, 0)),
                  pl.BlockSpec((HB, F2), lambda i: (jnp.minimum((i + 1) * r, nH - 1), 0)),
                  pl.BlockSpec((tm, F), lambda i: (i, 0)),
                  pl.BlockSpec((HB, F), lambda i: (jnp.minimum((i + 1) * r, nH - 1), 0)),
                  full(wf)],
        out_specs=[pl.BlockSpec((tm, F2), lambda i: (i, 0)),
                   pl.BlockSpec((K_F, F2), lambda i: (0, 0)), pl.BlockSpec((1, F2), lambda i: (0, 0))],
        out_shape=[jax.ShapeDtypeStruct((S, F2), BF16), jax.ShapeDtypeStruct((K_F, F2), F32),
                   jax.ShapeDtypeStruct((1, F2), F32)],
        scratch_shapes=[pltpu.VMEM((2, tm + HB, LANES), F32), pltpu.VMEM((2, 3, tm, LANES), F32)],
        compiler_params=_cp(1), name=name,
    )(up, upc, upc, da, da, wf)


HBM = pl.BlockSpec(memory_space=pltpu.HBM)


def _place():
    return lax.axis_index("x"), lax.axis_index("y"), lax.axis_index("c")


def allgather8(buf, *, name):
    R, C = buf.shape

    def body(x_ref, o_ref, send_sems, recv_sems, local_sem):
        x, y, c = _place()
        me = 4 * x + 2 * y + c
        mine = pltpu.make_async_copy(x_ref, o_ref.at[me], local_sem)
        mine.start()
        sends = []
        for k in range(1, 8):
            fx, fy, fc = (k >> 2) & 1, (k >> 1) & 1, k & 1
            px, py, pc = (x + fx) % 2, (y + fy) % 2, (c + fc) % 2
            cp = pltpu.make_async_remote_copy(
                src_ref=x_ref, dst_ref=o_ref.at[me], send_sem=send_sems.at[k - 1], recv_sem=recv_sems.at[k - 1],
                device_id=(px, py, pc), device_id_type=MESH)
            cp.start()
            sends.append(cp)
        for k in range(1, 8):
            fx, fy, fc = (k >> 2) & 1, (k >> 1) & 1, k & 1
            peer = 4 * ((x + fx) % 2) + 2 * ((y + fy) % 2) + (c + fc) % 2
            pltpu.make_async_remote_copy(
                src_ref=x_ref, dst_ref=o_ref.at[peer], send_sem=send_sems.at[k - 1], recv_sem=recv_sems.at[k - 1],
                device_id=(x, y, c), device_id_type=MESH).wait_recv()
        for cp in sends:
            cp.wait_send()
        mine.wait()

    return pl.pallas_call(
        body, in_specs=[HBM], out_specs=HBM, out_shape=jax.ShapeDtypeStruct((8, R, C), buf.dtype),
        scratch_shapes=[pltpu.SemaphoreType.DMA((7,)), pltpu.SemaphoreType.DMA((7,)), pltpu.SemaphoreType.DMA],
        name=name,
    )(buf)


def gather_chip_shards(shards, *, name):
    n = len(shards)

    def body(*refs):
        ins, outs = refs[:n], refs[n:2 * n]
        send_sems, recv_sems = refs[2 * n:]
        x, y, c = _place()
        b = 2 * x + y
        chips = [(1 - x, y), (x, 1 - y), (1 - x, 1 - y)]
        sends = []

        def half(a, which):
            rh = ins[a].shape[0] // 2
            return pl.ds(pl.multiple_of(which * rh, 16), rh)

        def copy(a, k, src, dst, to):
            return pltpu.make_async_remote_copy(
                src_ref=src, dst_ref=dst, send_sem=send_sems.at[7 * a + k], recv_sem=recv_sems.at[7 * a + k],
                device_id=to, device_id_type=MESH)

        for a in range(n):
            for j, (cx, cy) in enumerate(chips):
                cp = copy(a, j, ins[a].at[half(a, c)], outs[a].at[b, half(a, c)], (cx, cy, c))
                cp.start()
                sends.append(cp)
        for a in range(n):
            cp = copy(a, 6, ins[a], outs[a].at[b], (x, y, 1 - c))
            cp.start()
            sends.append(cp)
        for a in range(n):
            for j, (cx, cy) in enumerate(chips):
                got = outs[a].at[2 * cx + cy, half(a, c)]
                copy(a, j, got, got, (x, y, c)).wait_recv()
                cp = copy(a, 3 + j, got, got, (x, y, 1 - c))
                cp.start()
                sends.append(cp)
        for a in range(n):
            for j, (cx, cy) in enumerate(chips):
                got = outs[a].at[2 * cx + cy, half(a, 1 - c)]
                copy(a, 3 + j, got, got, (x, y, c)).wait_recv()
            copy(a, 6, ins[a], outs[a].at[b], (x, y, c)).wait_recv()
        for cp in sends:
            cp.wait_send()

    return pl.pallas_call(
        body, in_specs=[HBM] * n, out_specs=[HBM] * n,
        out_shape=[jax.ShapeDtypeStruct((4,) + s.shape, s.dtype) for s in shards],
        scratch_shapes=[pltpu.SemaphoreType.DMA((7 * n,)), pltpu.SemaphoreType.DMA((7 * n,))],
        name=name,
    )(*shards)


def sibling_swap_halves(gs, *, name):
    n = len(gs)

    def body(*refs):
        ins, outs = refs[:n], refs[n:2 * n]
        send_sems, recv_sems = refs[2 * n:]
        x, y, c = _place()
        cps = []
        for a in range(n):
            rh = ins[a].shape[1] // 2
            src = ins[a].at[:, pl.ds(pl.multiple_of((1 - c) * rh, 16), rh)]
            cp = pltpu.make_async_remote_copy(
                src_ref=src, dst_ref=outs[a], send_sem=send_sems.at[a], recv_sem=recv_sems.at[a],
                device_id=(x, y, 1 - c), device_id_type=MESH)
            cp.start()
            cps.append(cp)
        for cp in cps:
            cp.wait()

    return pl.pallas_call(
        body, in_specs=[HBM] * n, out_specs=[HBM] * n,
        out_shape=[jax.ShapeDtypeStruct((4, g.shape[1] // 2, g.shape[2]), g.dtype) for g in gs],
        scratch_shapes=[pltpu.SemaphoreType.DMA((n,)), pltpu.SemaphoreType.DMA((n,))],
        name=name,
    )(*gs)


SEM =pl.BlockSpec(memory_space=pltpu.SEMAPHORE)
SPLIT_COPY = pltpu.CompilerParams(has_side_effects=pltpu.SideEffectType.DATAFLOW_SIDE_EFFECTING)


def _split_start(srcs, lands, n_copies, issue, *, name):
    n, m = len(srcs), len(lands)

    def body(*refs):
        ins, lnd = refs[:n], refs[n:n + m]
        send_sems, recv_sems = refs[n + m], refs[n + m + 1]
        token = refs[-1]

        def copy(k, src, dst, to):
            return pltpu.make_async_remote_copy(src_ref=src, dst_ref=dst, send_sem=send_sems.at[k],
                                                recv_sem=recv_sems.at[k], device_id=to, device_id_type=MESH)

        for cp in issue(ins, lnd, copy):
            cp.start()
        token[...] = jnp.zeros_like(token)

    outs = pl.pallas_call(
        body, name=name,
        out_shape=(pltpu.SemaphoreType.DMA((n_copies,)), pltpu.SemaphoreType.DMA((n_copies,)),
                   *[pltpu.HBM(a.shape, a.dtype) for a in list(srcs) + list(lands)],
                   jax.ShapeDtypeStruct((8, LANES), F32)),
        in_specs=[HBM] * (n + m),
        out_specs=(SEM, SEM, *([HBM] * (n + m)), pl.BlockSpec(memory_space=pltpu.VMEM)),
        input_output_aliases={i: 2 + i for i in range(n + m)},
        compiler_params=SPLIT_COPY,
    )(*[pltpu.with_memory_space_constraint(a, pltpu.HBM) for a in list(srcs) + list(lands)])
    return outs[0], outs[1], list(outs[2:2 + n]), list(outs[2 + n:2 + n + m]), outs[-1]


def _split_wait(send_sems, recv_sems, srcs, lands, after, issue, *, name):
    n, m = len(srcs), len(lands)

    def body(*refs):
        ins, lnd = refs[:n], refs[n:n + m]
        s_sems, r_sems = refs[n + m], refs[n + m + 1]

        def copy(k, src, dst, to):
            return pltpu.make_async_remote_copy(src_ref=src, dst_ref=dst, send_sem=s_sems.at[k],
                                                recv_sem=r_sems.at[k], device_id=to, device_id_type=MESH)

        for cp in issue(ins, lnd, copy):
            cp.wait_send()
            cp.wait_recv()

    outs = pl.pallas_call(
        body, name=name,
        out_shape=tuple(pltpu.HBM(a.shape, a.dtype) for a in list(srcs) + list(lands)),
        in_specs=[HBM] * (n + m) + [SEM, SEM, pl.BlockSpec(memory_space=pl.ANY)],
        out_specs=tuple([HBM] * (n + m)),
        input_output_aliases={i: i for i in range(n + m)},
        compiler_params=SPLIT_COPY,
    )(*srcs, *lands, send_sems, recv_sems, after)
    return list(outs[:n]), list(outs[n:n + m])


def _gather_direct_copies(received):
    def issue(ins, lnd, copy):
        x, y, c = _place()
        b = 2 * x + y
        chips = [(1 - x, y), (x, 1 - y), (1 - x, 1 - y)]
        cps = []
        for a in range(len(ins)):
            for j, (cx, cy) in enumerate(chips):
                slot = 2 * cx + cy if received else b
                cps.append(copy(4 * a + j, ins[a], lnd[a].at[slot], (cx, cy, c)))
            cps.append(copy(4 * a + 3, ins[a], lnd[a].at[b], (x, y, 1 - c)))
        return cps
    return issue


def _exchange_copies(received):
    def issue(ins, lnd, copy):
        x, y, c = _place()
        b = 2 * x + y
        chips = [(1 - x, y), (x, 1 - y), (1 - x, 1 - y)]
        cps = []
        for a in range(len(ins)):
            for j, (cx, cy) in enumerate(chips):
                slot = 2 * cx + cy if received else b
                cps.append(copy(3 * a + j, ins[a].at[2 * cx + cy], lnd[a].at[slot], (cx, cy, c)))
        return cps
    return issue


def sibling_join_halves(fs, *, name):
    n = len(fs)

    def body(*refs):
        ins, outs = refs[:n], refs[n:2 * n]
        send_sems, recv_sems = refs[2 * n:]
        x, y, c = _place()
        sends = []
        for a in range(n):
            rh = ins[a].shape[0] // 2
            mine = pl.ds(pl.multiple_of(c * rh, 8), rh)
            cp = pltpu.make_async_remote_copy(
                src_ref=ins[a].at[mine], dst_ref=outs[a].at[mine], send_sem=send_sems.at[a],
                recv_sem=recv_sems.at[a], device_id=(x, y, 1 - c), device_id_type=MESH)
            cp.start()
            sends.append(cp)
        for a in range(n):
            rh = ins[a].shape[0] // 2
            other = pl.ds(pl.multiple_of((1 - c) * rh, 8), rh)
            pltpu.make_async_remote_copy(
                src_ref=ins[a].at[other], dst_ref=outs[a].at[other], send_sem=send_sems.at[a],
                recv_sem=recv_sems.at[a], device_id=(x, y, c), device_id_type=MESH).wait_recv()
        for cp in sends:
            cp.wait_send()

    return pl.pallas_call(
        body, in_specs=[HBM] * n, out_specs=[HBM] * n,
        out_shape=[jax.ShapeDtypeStruct(f.shape, f.dtype) for f in fs],
        input_output_aliases={a: a for a in range(n)},
        scratch_shapes=[pltpu.SemaphoreType.DMA((n,)), pltpu.SemaphoreType.DMA((n,))],
        name=name,
    )(*fs)


def pair_add(g, t, core, *, name):
    _, R, C = g.shape
    rh = R // 2
    tr = _tile(rh, 256, 16)
    nh = rh // tr

    def body(c_ref, g_ref, t_ref, o_ref):
        o_ref[...] = (g_ref[...].astype(F32) + t_ref[...].astype(F32)).astype(BF16)

    return pl.pallas_call(
        body,
        grid_spec=pltpu.PrefetchScalarGridSpec(
            num_scalar_prefetch=1, grid=(4, nh),
            in_specs=[pl.BlockSpec((None, tr, C), lambda b, i, c_ref: (b, c_ref[0] * nh + i, 0)),
                      pl.BlockSpec((None, tr, C), lambda b, i, c_ref: (b, i, 0))],
            out_specs=pl.BlockSpec((None, tr, C), lambda b, i, c_ref: (b, i, 0))),
        out_shape=jax.ShapeDtypeStruct((4, rh, C), BF16), compiler_params=_cp(2), name=name,
    )(core, g, t)


def sum_slots(q, *, name):
    N, R, C = q.shape
    tr = _tile(R, 256, 16)

    def body(q_ref, o_ref):
        acc = q_ref[0].astype(F32)
        for s in range(1, N):
            acc = acc + q_ref[s].astype(F32)
        o_ref[...] = acc

    return pl.pallas_call(
        body, grid=(R // tr,), in_specs=[pl.BlockSpec((N, tr, C), lambda i: (0, i, 0))],
        out_specs=pl.BlockSpec((tr, C), lambda i: (i, 0)),
        out_shape=jax.ShapeDtypeStruct((R, C), F32), compiler_params=_cp(1), name=name,
    )(q)


def sum_own_and_received(p, q, place, *, name):
    _, rh, C = p.shape
    tr = _tile(rh, 256, 16)
    nh = rh // tr

    def body(s0, s1, s2, s3, cr, p_ref, q1_ref, q2_ref, q3_ref, o_ref):
        o_ref[...] = (p_ref[...].astype(F32) + q1_ref[...].astype(F32)
                      + q2_ref[...].astype(F32) + q3_ref[...].astype(F32))

    def slot(d):
        return pl.BlockSpec((None, tr, C), lambda i, *pc: (pc[d][0], i, 0))

    return pl.pallas_call(
        body,
        grid_spec=pltpu.PrefetchScalarGridSpec(
            num_scalar_prefetch=5, grid=(nh,),
            in_specs=[slot(0), slot(1), slot(2), slot(3)],
            out_specs=pl.BlockSpec((tr, C), lambda i, *pc: (pc[4][0] * nh + i, 0))),
        out_shape=jax.ShapeDtypeStruct((2 * rh, C), F32), compiler_params=_cp(1), name=name,
    )(*place, p, q, q, q)


def _adam_math(w, g, m, v):
    m = ADAM_B1 * m + (1.0 - ADAM_B1) * g
    v = ADAM_B2 * v + (1.0 - ADAM_B2) * (g * g)
    m_hat = m / (1.0 - ADAM_B1 ** ADAM_STEP)
    v_hat = v / (1.0 - ADAM_B2 ** ADAM_STEP)
    delta = -ADAM_LR * (m_hat / (jnp.sqrt(v_hat) + ADAM_EPS) + ADAM_WD * w)
    return delta, m, v


def adam_stacked(w, m, v, grads, *, first_layer=0, earlier=None, after=None, name):
    L, R, C = w.shape
    n = len(grads)
    tr = _tile(R, max(8, ADAM_BLOCK_BYTES // (4 * C)), 8)
    nr = R // tr
    tail = list(earlier) if earlier is not None else []
    if after is not None:
        tail.append(after)
    n_in = 3 + n + len(tail)

    def body(*refs):
        w_ref, m_ref, v_ref = refs[:3]
        g_refs = refs[3:3 + n]
        go_ref, d_ref, mo_ref, vo_ref = refs[n_in:]
        lid = pl.program_id(0)
        for l in range(n):
            @pl.when(lid == l)
            def _(l=l):
                g = g_refs[l][...]
                d, mn, vn = _adam_math(w_ref[...], g, m_ref[...], v_ref[...])
                go_ref[...] = g
                d_ref[...] = d
                mo_ref[...] = mn
                vo_ref[...] = vn

    st = pl.BlockSpec((None, tr, C), lambda l, i: (l + first_layer, i, 0))
    g_specs = [pl.BlockSpec((tr, C), functools.partial(lambda l, i, ll: (jnp.where(l == ll, i, 0), 0), ll=ll))
               for ll in range(n)]
    extra = [pl.BlockSpec(memory_space=pl.ANY)] * len(tail)
    return pl.pallas_call(
        body, grid=(n, nr), in_specs=[st, st, st] + g_specs + extra, out_specs=[st] * 4,
        out_shape=[jax.ShapeDtypeStruct((L, R, C), F32)] * 4,
        input_output_aliases={3 + n + k: k for k in range(4)} if earlier is not None else {},
        compiler_params=_cp(2), name=name,
    )(w, m, v, *grads, *tail)


def adam_flat(w, g, m, v, *, name):
    R, C = w.shape
    tr = _tile(R, 512, 8)

    def body(w_ref, g_ref, m_ref, v_ref, d_ref, mo_ref, vo_ref):
        d, mn, vn = _adam_math(w_ref[...], g_ref[...], m_ref[...], v_ref[...])
        d_ref[...] = d
        mo_ref[...] = mn
        vo_ref[...] = vn

    row = pl.BlockSpec((tr, C), lambda i: (i, 0))
    return pl.pallas_call(
        body, grid=(R // tr,), in_specs=[row] * 4, out_specs=[row] * 3,
        out_shape=[jax.ShapeDtypeStruct((R, C), F32)] * 3, compiler_params=_cp(1), name=name,
    )(w, g, m, v)


PACK_ROWS = 64


def _pack(arrays):
    flat = jnp.concatenate([a.reshape(-1).astype(F32) for a in arrays])
    n = flat.shape[0]
    unit = PACK_ROWS * LANES
    pad = (-n) % unit
    return jnp.pad(flat, (0, pad)).reshape(-1, LANES)


def _unpack(buf, shapes):
    flat = buf.reshape(-1)
    out, o = [], 0
    for s in shapes:
        n = 1
        for d in s:
            n *= d
        out.append(flat[o:o + n].reshape(s))
        o += n
    return out


def kernel(x, norm_mix_pre, norm_mix_post, norm_ffn_pre, norm_ffn_post, w_in, conv_a_w, conv_b_w, conv_b_bias, ln_b_gain, ln_b_bias, pool_w, pool_scale, w_out, w_up, conv_ffn_w, conv_ffn_bias, w_down, loss_target, m_norm_mix_pre, m_norm_mix_post, m_norm_ffn_pre, m_norm_ffn_post, m_w_in, m_conv_a_w, m_conv_b_w, m_conv_b_bias, m_ln_b_gain, m_ln_b_bias, m_pool_w, m_pool_scale, m_w_out, m_w_up, m_conv_ffn_w, m_conv_ffn_bias, m_w_down, v_norm_mix_pre, v_norm_mix_post, v_norm_ffn_pre, v_norm_ffn_post, v_w_in, v_conv_a_w, v_conv_b_w, v_conv_b_bias, v_ln_b_gain, v_ln_b_bias, v_pool_w, v_pool_scale, v_w_out, v_w_up, v_conv_ffn_w, v_conv_ffn_bias, v_w_down):
    L = w_in.shape[0]
    S, D = x.shape[1], x.shape[2]
    WA, WB, WC = 4 * conv_a_w.shape[2], 4 * conv_b_w.shape[2], pool_scale.shape[1]
    DIN, DMIX, F2 = 4 * w_in.shape[2], 4 * w_out.shape[1], 4 * w_up.shape[2]
    F = F2 // 2
    NG = WC // LANES
    xi, yi, ci = _place()
    chip = 2 * xi + yi
    core = jnp.reshape(ci, (1,)).astype(jnp.int32)

    conv_shapes = [(L, K_A, WA // 4), (L, K_B, WB // 4), (L, K_F, F2 // 4)]
    conv_all = allgather8(_pack([conv_a_w, conv_b_w, conv_ffn_w]), name="gather_conv_taps")
    per_chip = [_unpack(conv_all[2 * b], conv_shapes) for b in range(4)]
    wa_full, wb_full, wf_full = [jnp.concatenate([per_chip[b][k] for b in range(4)], axis=2) for k in range(3)]
    pw_bf = pool_w.astype(BF16)

    def shards_of(l):
        return [w_in[l].astype(BF16), w_up[l].astype(BF16), w_out[l].astype(BF16), w_down[l].astype(BF16)]

    def assemble(g_in, g_up, g_out, g_down):
        return (jnp.concatenate([g_in[b] for b in range(4)], axis=1),
                g_up,
                g_out.reshape(DMIX, D), g_down.reshape(F, D))

    w_full = [assemble(*gather_chip_shards(shards_of(0), name="gather_layer_weights"))]

    def vec(a, l):
        return a[l].reshape(1, -1)

    x0 = x.reshape(S, D)
    h1 = norm_fwd(x0, vec(norm_mix_pre, 0), name="norm_first")
    saved = []
    for l in range(L):
        Win, Wup, Wout, Wdown = w_full[l]
        token = None
        if l + 1 < L:
            srcs = shards_of(l + 1)
            lands = [lax.empty((4,) + s.shape, s.dtype) for s in srcs]
            s_sems, r_sems, srcs, lands, token = _split_start(
                srcs, lands, 4 * len(srcs), _gather_direct_copies(False), name="gather_start_layer%d" % (l + 1))
        u = matmul(h1, Win, out_dtype=BF16, tm=1024, tn=2176, tk=2048, j_outer=True, after=token, name="mm_in")
        ymix, cb = mixer_fwd(u, wa_full[l], wb_full[l], vec(conv_b_bias, l), vec(ln_b_gain, l), vec(ln_b_bias, l),
                             pw_bf[l], vec(pool_scale, l), name="mixer_fwd")
        y = matmul(ymix, Wout, out_dtype=F32, tm=512, tn=2048, tk=2048, name="mm_out")
        x1, h2 = resid_norm_fwd(x0, y, vec(norm_mix_post, l), vec(norm_ffn_pre, l), emit_h=True, name="resid_norm_mid")
        up = matmul(h2, Wup, out_dtype=BF16, tm=512, tn=2816, tk=2048, j_outer=True, name="mm_up")
        a, upc = ffn_fwd(up, wf_full[l], vec(conv_ffn_bias, l), name="ffn_fwd")
        f = matmul(a, Wdown, out_dtype=F32, tm=1024, tn=2048, tk=1408, name="mm_down")
        last = l == L - 1
        x2, h_next = resid_norm_fwd(x1, f, vec(norm_ffn_post, l), vec(norm_mix_pre, 0 if last else l + 1),
                                    emit_h=not last, name="resid_norm_last" if last else "resid_norm_end")
        saved.append((x0, h1, u, ymix, y, x1, h2, up, a, f, cb, upc))
        if l + 1 < L:
            w_full.append(assemble(*_split_wait(s_sems, r_sems, srcs, lands, x2, _gather_direct_copies(True),
                                                name="gather_wait_layer%d" % (l + 1))[1]))
        x0, h1 = x2, h_next

    dx, lsum = loss_head(x0, loss_target.reshape(S, D), name="loss_head")
    loss = lax.psum(lsum[0, 0] * (0.5 / D), ("x", "y", "c"))

    small = [None] * L
    big = [None] * L
    dt = dx
    _, df, _, dg4 = norm_bwd(dt, None, None, None, saved[L - 1][9], vec(norm_ffn_post, L - 1), name="norm_bwd_top")
    place = [jnp.reshape(v, (1,)).astype(jnp.int32) for v in
             (2 * xi + yi, 2 * (1 - xi) + yi, 2 * xi + (1 - yi), 2 * (1 - xi) + (1 - yi), ci)]

    def finish_exchange(pend, after):
        lp, s_sems, r_sems, ps, qs = pend
        ps, qs = _split_wait(s_sems, r_sems, ps, qs, after, _exchange_copies(True),
                             name="grad_exchange_wait_layer%d" % lp)
        fh = [sum_own_and_received(p, q, place, name="grad_sum_chips_%d" % k) for k, (p, q) in enumerate(zip(ps, qs))]
        big[lp] = sibling_join_halves(fh, name="grad_join_halves")

    pending, token = None, None
    for l in reversed(range(L)):
        Win, Wup, Wout, Wdown = w_full[l]
        x0, h1, u, ymix, y, x1, h2, up, a, f, cb, upc = saved[l]
        da = matmul(df, Wdown, tb=True, out_dtype=BF16, tm=1024, tn=1408, tk=2048, j_outer=True, after=token,
                    name="mm_down_dx")
        g_down = matmul(a, df, ta=True, out_dtype=BF16, tm=1408, tn=1024, tk=2048, name="mm_down_dw")
        dup, dwf, dbf = ffn_bwd(up, upc, da, wf_full[l], name="ffn_bwd")
        dh2 = matmul(dup, Wup, tb=True, out_dtype=F32, tm=512, tn=2048, tk=2816, name="mm_up_dx")
        g_up = matmul(h2, dup, ta=True, out_dtype=BF16, tm=1024, tn=1408, tk=2048, groups=4, name="mm_up_dw")
        dt, dy, dg3, dg2 = norm_bwd(dt, dh2, x1, vec(norm_ffn_pre, l), y, vec(norm_mix_post, l), name="norm_bwd_mid")
        dymix = matmul(dy, Wout, tb=True, out_dtype=BF16, tm=1024, tn=2048, tk=2048, name="mm_out_dx")
        g_out = matmul(ymix, dy, ta=True, out_dtype=BF16, tm=1024, tn=1024, tk=2048, name="mm_out_dw")
        du, dwa, dwb, dbb, dlg, dlb, dpw, dps = mixer_bwd(
            u, cb, dymix, wa_full[l], wb_full[l], vec(conv_b_bias, l), vec(ln_b_gain, l), vec(ln_b_bias, l),
            pw_bf[l], vec(pool_scale, l), name="mixer_bwd")
        dh1 = matmul(du, Win, tb=True, out_dtype=F32, tm=512, tn=2048, tk=2176, name="mm_in_dx")
        g_in = matmul(h1, du, ta=True, out_dtype=BF16, tm=1024, tn=2176, tk=1024, name="mm_in_dw")
        dg4_here = dg4
        if l > 0:
            dt, df, dg1, dg4 = norm_bwd(dt, dh1, x0, vec(norm_mix_pre, l), saved[l - 1][9], vec(norm_ffn_post, l - 1),
                                        name="norm_bwd_end")
        else:
            dt, _, dg1, _ = norm_bwd(dt, dh1, x0, vec(norm_mix_pre, 0), None, None, name="norm_bwd_bottom")
        small[l] = dict(norm_mix_pre=dg1, norm_mix_post=dg2, norm_ffn_pre=dg3, norm_ffn_post=dg4_here,
                        conv_a_w=dwa, conv_b_w=dwb, conv_b_bias=dbb, ln_b_gain=dlg, ln_b_bias=dlb,
                        pool_w=dpw, pool_scale=dps, conv_ffn_w=dwf, conv_ffn_bias=dbf)

        gs = [g_in.reshape(D, 4, DIN // 4).transpose(1, 0, 2), g_up,
              g_out.reshape(4, DMIX // 4, D), g_down.reshape(4, F // 4, D)]
        ts = sibling_swap_halves(gs, name="grad_swap_halves")
        ps = [pair_add(g, t, core, name="grad_pair_add_%d" % k) for k, (g, t) in enumerate(zip(gs, ts))]
        if pending is not None:
            finish_exchange(pending, dt)
        qs = [lax.empty(p.shape, p.dtype) for p in ps]
        s_sems, r_sems, ps, qs, token = _split_start(ps, qs, 3 * len(ps), _exchange_copies(False),
                                                     name="grad_exchange_start_layer%d" % l)
        pending = (l, s_sems, r_sems, ps, qs)
    grad_x = dt.reshape(1, S, D)

    rep_names = ["norm_mix_pre", "norm_mix_post", "norm_ffn_pre", "norm_ffn_post", "conv_b_bias", "ln_b_gain",
                 "ln_b_bias", "pool_w", "pool_scale", "conv_ffn_bias"]
    shd_names = ["conv_a_w", "conv_b_w", "conv_ffn_w"]
    given = dict(
        norm_mix_pre=(norm_mix_pre, m_norm_mix_pre, v_norm_mix_pre), norm_mix_post=(norm_mix_post, m_norm_mix_post, v_norm_mix_post),
        norm_ffn_pre=(norm_ffn_pre, m_norm_ffn_pre, v_norm_ffn_pre), norm_ffn_post=(norm_ffn_post, m_norm_ffn_post, v_norm_ffn_post),
        conv_b_bias=(conv_b_bias, m_conv_b_bias, v_conv_b_bias), ln_b_gain=(ln_b_gain, m_ln_b_gain, v_ln_b_gain),
        ln_b_bias=(ln_b_bias, m_ln_b_bias, v_ln_b_bias), pool_w=(pool_w, m_pool_w, v_pool_w),
        pool_scale=(pool_scale, m_pool_scale, v_pool_scale), conv_ffn_bias=(conv_ffn_bias, m_conv_ffn_bias, v_conv_ffn_bias),
        conv_a_w=(conv_a_w, m_conv_a_w, v_conv_a_w), conv_b_w=(conv_b_w, m_conv_b_w, v_conv_b_w),
        conv_ffn_w=(conv_ffn_w, m_conv_ffn_w, v_conv_ffn_w))
    full_shape = dict(conv_a_w=(L, K_A, WA), conv_b_w=(L, K_B, WB), conv_ffn_w=(L, K_F, F2))
    for nme in rep_names:
        full_shape[nme] = given[nme][0].shape
    names = rep_names + shd_names
    stacked = [jnp.stack([small[l][nme] for l in range(L)]).reshape(full_shape[nme]) for nme in names]
    parts = allgather8(_pack(stacked), name="gather_small_grads")
    totals = _unpack(sum_slots(parts, name="sum_small_grads"), [full_shape[nme] for nme in names])
    total = dict(zip(names, totals))
    for nme in shd_names:
        wd = full_shape[nme][2] // 4
        total[nme] = lax.dynamic_slice_in_dim(total[nme], chip * wd, wd, axis=2)
    shapes = [given[nme][0].shape for nme in names]
    d_s, m_s, v_s = adam_flat(_pack([given[nme][0] for nme in names]), _pack([total[nme] for nme in names]),
                              _pack([given[nme][1] for nme in names]), _pack([given[nme][2] for nme in names]),
                              name="adam_small")
    res = dict(zip(names, zip([total[nme] for nme in names], _unpack(d_s, shapes), _unpack(m_s, shapes),
                              _unpack(v_s, shapes))))

    large = [("w_in", (w_in, m_w_in, v_w_in)), ("w_up", (w_up, m_w_up, v_w_up)),
             ("w_out", (w_out, m_w_out, v_w_out)), ("w_down", (w_down, m_w_down, v_w_down))]
    assert L > 1
    upper = [adam_stacked(*trio, [big[l][k] for l in range(1, L)], first_layer=1, after=token,
                          name="adam_upper_" + nme)
             for k, (nme, trio) in enumerate(large)]
    finish_exchange(pending, upper[-1][1])
    for k, (nme, trio) in enumerate(large):
        res[nme] = adam_stacked(*trio, [big[0][k]], earlier=upper[k], name="adam_first_" + nme)

    order = ["norm_mix_pre", "norm_mix_post", "norm_ffn_pre", "norm_ffn_post", "w_in", "conv_a_w", "conv_b_w",
             "conv_b_bias", "ln_b_gain", "ln_b_bias", "pool_w", "pool_scale", "w_out", "w_up", "conv_ffn_w",
             "conv_ffn_bias", "w_down"]
    outs = [loss, grad_x]
    for k in range(4):
        outs += [res[nme][k] for nme in order]
    return tuple(outs)
```
